```python
import math
import jax, jax.numpy as jnp
from jax import lax
import numpy as np

D_MODEL = 1024
BATCH = 8
SEQ = 2048
DEPTH = 1
DEC_BATCH = 128
DEC_SEQ = 8
PAST_LEN = 16384
PAGE_SIZE = 128

EPS = 1e-6
D_FF = ((8 * D_MODEL // 3 + 127) // 128) * 128
GLA_WIDTH = D_MODEL // 2
GLA_HEADS = 4
GLA_DV = GLA_WIDTH // GLA_HEADS
GLA_DK = GLA_DV // 2
GLA_QK_WIDTH = GLA_HEADS * GLA_DK
GLA_GATE_RANK = 16
GLA_TAU = 16.0
GLA_CHUNK = 16
S5_WIDTH = D_MODEL - GLA_WIDTH
S5_P = 16
S5_GROUPS = S5_WIDTH // S5_P
S5_N = 64
S5_DT_MIN = 0.001
S5_DT_MAX = 0.1
MIX_WIDTH = GLA_WIDTH + S5_WIDTH
CROSS_HEADS = 4
CROSS_DH = D_MODEL // CROSS_HEADS
MEM_LEN = 256
IN_SIZES = (GLA_QK_WIDTH, GLA_QK_WIDTH, GLA_WIDTH, GLA_GATE_RANK, GLA_WIDTH, S5_WIDTH)
IN_WIDTH = sum(IN_SIZES)
SPLIT_POINTS = tuple(int(s) for s in np.cumsum(IN_SIZES)[:-1])

kernel_name = "hymba_gla_s5_macaron_decode_step"


def rms_norm(x, g):
    xf = x.astype(jnp.float32)
    y = xf * lax.rsqrt(jnp.mean(xf * xf, axis=-1, keepdims=True) + EPS)
    return (y * g.astype(jnp.float32)).astype(x.dtype)


def swiglu(x, w_gu, w_down):
    g, u = jnp.split(x @ w_gu, 2, axis=-1)
    return (jax.nn.silu(g) * u) @ w_down


def gla_recurrence(q, k, v, log_a, s0):
    bsz, t, h, _ = q.shape
    dv = v.shape[-1]
    n_chunks = -(-t // GLA_CHUNK)
    pad = n_chunks * GLA_CHUNK - t

    def to_chunks(a):
        a = jnp.pad(a, ((0, 0), (0, pad), (0, 0), (0, 0)))
        return a.reshape(bsz, n_chunks, GLA_CHUNK, h, a.shape[-1]).transpose(1, 0, 3, 2, 4)

    qc, kc, vc, lc = to_chunks(q), to_chunks(k), to_chunks(v), to_chunks(log_a)
    bc = jnp.cumsum(lc, axis=3)
    causal = jnp.tril(jnp.ones((GLA_CHUNK, GLA_CHUNK), dtype=bool))[:, :, None]

    def step(s, inp):
        qi, ki, vi, bi = inp
        o_inter = jnp.einsum('bhtd,bhde->bhte', qi * jnp.exp(bi), s)
        rel = bi[:, :, :, None, :] - bi[:, :, None, :, :]
        decay = jnp.exp(jnp.where(causal, rel, -jnp.inf))
        scores = jnp.einsum('bhtd,bhsd,bhtsd->bhts', qi, ki, decay)
        o_intra = jnp.einsum('bhts,bhse->bhte', scores, vi)
        b_last = bi[:, :, -1:, :]
        s_new = jnp.exp(b_last[:, :, 0, :])[..., None] * s + jnp.einsum(
            'bhsd,bhse->bhde', ki * jnp.exp(b_last - bi), vi)
        return s_new, o_inter + o_intra

    s_final, o = lax.scan(step, s0, (qc, kc, vc, bc))
    o = o.transpose(1, 0, 3, 2, 4).reshape(bsz, n_chunks * GLA_CHUNK, h, dv)[:, :t]
    return o, s_final


def s5_discretise(lambda_re, lambda_im, log_dt, b_re, b_im):
    lr = lambda_re.astype(jnp.float32)
    li = lambda_im.astype(jnp.float32)
    dt = jnp.exp(log_dt.astype(jnp.float32))[:, None]
    mag = jnp.exp(dt * lr)
    ab_re = mag * jnp.cos(dt * li)
    ab_im = mag * jnp.sin(dt * li)
    den = lr * lr + li * li
    coef_re = ((ab_re - 1.0) * lr + ab_im * li) / den
    coef_im = (ab_im * lr - (ab_re - 1.0) * li) / den
    br = b_re.astype(jnp.float32)
    bi = b_im.astype(jnp.float32)
    bb_re = coef_re[..., None] * br - coef_im[..., None] * bi
    bb_im = coef_re[..., None] * bi + coef_im[..., None] * br
    return ab_re, ab_im, bb_re, bb_im


def complex_affine_combine(e1, e2):
    a1r, a1i, b1r, b1i = e1
    a2r, a2i, b2r, b2i = e2
    return (a2r * a1r - a2i * a1i,
            a2r * a1i + a2i * a1r,
            a2r * b1r - a2i * b1i + b2r,
            a2r * b1i + a2i * b1r + b2i)


def s5_mixer(u, x0_re, x0_im, lambda_re, lambda_im, log_dt, b_re, b_im, c_re, c_im, d_skip, w_glu, b_glu):
    ab_re, ab_im, bb_re, bb_im = s5_discretise(lambda_re, lambda_im, log_dt, b_re, b_im)
    bu_re = jnp.einsum('btgp,gnp->btgn', u, bb_re)
    bu_im = jnp.einsum('btgp,gnp->btgn', u, bb_im)
    first_re = bu_re[:, :1] + (ab_re * x0_re - ab_im * x0_im)[:, None]
    first_im = bu_im[:, :1] + (ab_re * x0_im + ab_im * x0_re)[:, None]
    bu_re = jnp.concatenate([first_re, bu_re[:, 1:]], axis=1)
    bu_im = jnp.concatenate([first_im, bu_im[:, 1:]], axis=1)
    a_re = jnp.broadcast_to(ab_re, bu_re.shape)
    a_im = jnp.broadcast_to(ab_im, bu_im.shape)
    _, _, xr, xi = lax.associative_scan(complex_affine_combine, (a_re, a_im, bu_re, bu_im), axis=1)
    y = (jnp.einsum('btgn,gpn->btgp', xr, c_re.astype(jnp.float32))
         - jnp.einsum('btgn,gpn->btgp', xi, c_im.astype(jnp.float32))
         + d_skip.astype(jnp.float32) * u)
    z = jax.nn.gelu(y)
    out = z * jax.nn.sigmoid(jnp.einsum('btgp,gpq->btgq', z, w_glu.astype(jnp.float32))
                             + b_glu.astype(jnp.float32))
    return out, xr[:, -1], xi[:, -1]


def memory_kv(mem, g_mem, w_ckv):
    bsz = mem.shape[0]
    k, v = jnp.split(rms_norm(mem, g_mem) @ w_ckv, 2, axis=-1)
    return (k.reshape(bsz, MEM_LEN, CROSS_HEADS, CROSS_DH),
            v.reshape(bsz, MEM_LEN, CROSS_HEADS, CROSS_DH))


def cross_attend(xn, mem_k, mem_v, w_cq, w_co):
    bsz, t, _ = xn.shape
    q = (xn @ w_cq).reshape(bsz, t, CROSS_HEADS, CROSS_DH)
    s = jnp.einsum('bthd,bmhd->bhtm', q, mem_k).astype(jnp.float32) * (CROSS_DH ** -0.5)
    p = jax.nn.softmax(s, axis=-1).astype(mem_v.dtype)
    o = jnp.einsum('bhtm,bmhd->bthd', p, mem_v).reshape(bsz, t, D_MODEL)
    return o @ w_co


def trunk_layer(x, s_gla0, s5_re0, s5_im0, mem_k, mem_v, w):
    bsz, t, _ = x.shape
    f32 = jnp.float32
    h = x + 0.5 * swiglu(rms_norm(x, w['g_ffn1']), w['w_ffn1_gu'], w['w_ffn1_down'])
    u = rms_norm(h, w['g_mix'])
    q, k, v, zg, r, us5 = jnp.split(u @ w['w_in'], SPLIT_POINTS, axis=-1)
    q = q.reshape(bsz, t, GLA_HEADS, GLA_DK).astype(f32) * (GLA_DK ** -0.5)
    k = k.reshape(bsz, t, GLA_HEADS, GLA_DK).astype(f32)
    v = v.reshape(bsz, t, GLA_HEADS, GLA_DV).astype(f32)
    log_a = jax.nn.log_sigmoid(zg.astype(f32) @ w['w_gla_gate'].astype(f32)
                               + w['b_gla_gate'].astype(f32)) / GLA_TAU
    log_a = log_a.reshape(bsz, t, GLA_HEADS, GLA_DK)
    o, s_gla = gla_recurrence(q, k, v, log_a, s_gla0.astype(f32))
    o = rms_norm(o, w['g_gla_out']) * jax.nn.silu(r.reshape(bsz, t, GLA_HEADS, GLA_DV).astype(f32))
    o_s5, s5_re, s5_im = s5_mixer(
        us5.reshape(bsz, t, S5_GROUPS, S5_P).astype(f32), s5_re0.astype(f32), s5_im0.astype(f32),
        w['s5_lambda_re'], w['s5_lambda_im'], w['s5_log_dt'], w['s5_b_re'], w['s5_b_im'],
        w['s5_c_re'], w['s5_c_im'], w['s5_d'], w['s5_w_glu'], w['s5_b_glu'])
    mix = jnp.concatenate([o.reshape(bsz, t, GLA_WIDTH), o_s5.reshape(bsz, t, S5_WIDTH)], axis=-1)
    h = h + mix.astype(x.dtype) @ w['w_out']
    h = h + cross_attend(rms_norm(h, w['g_cross']), mem_k, mem_v, w['w_cq'], w['w_co'])
    h = h + 0.5 * swiglu(rms_norm(h, w['g_ffn2']), w['w_ffn2_gu'], w['w_ffn2_down'])
    return h, s_gla, s5_re, s5_im


def setup_inputs(seed: int = 0) -> dict:
    key = jax.random.key(seed)
    ks = iter(jax.random.split(key, 48))
    f32 = jnp.float32

    def nrm(shape, scale):
        return jax.random.normal(next(ks), shape, f32) * scale

    def gain(shape):
        return jnp.ones(shape, f32) + nrm(shape, 0.02)

    L = DEPTH
    inp = {}
    inp['x_prompt'] = nrm((BATCH, SEQ, D_MODEL), 1.0)
    inp['x_sample'] = nrm((DEC_BATCH, DEC_SEQ, D_MODEL), 1.0)
    inp['state_gla'] = nrm((L, DEC_BATCH, GLA_HEADS, GLA_DK, GLA_DV), 1.0)
    inp['state_s5_re'] = nrm((L, DEC_BATCH, S5_GROUPS, S5_N), 0.1)
    inp['state_s5_im'] = nrm((L, DEC_BATCH, S5_GROUPS, S5_N), 0.1)
    inp['cache_mem_k'] = nrm((L, DEC_BATCH, MEM_LEN, CROSS_HEADS, CROSS_DH), 1.0)
    inp['cache_mem_v'] = nrm((L, DEC_BATCH, MEM_LEN, CROSS_HEADS, CROSS_DH), 1.0)
    inp['mem_prompt'] = nrm((BATCH, MEM_LEN, D_MODEL), 1.0)
    inp['g_ffn1'] = gain((L, D_MODEL))
    inp['w_ffn1_gu'] = nrm((L, D_MODEL, 2 * D_FF), D_MODEL ** -0.5)
    inp['w_ffn1_down'] = nrm((L, D_FF, D_MODEL), D_FF ** -0.5)
    inp['g_mix'] = gain((L, D_MODEL))
    inp['w_in'] = nrm((L, D_MODEL, IN_WIDTH), D_MODEL ** -0.5)
    inp['w_gla_gate'] = nrm((L, GLA_GATE_RANK, GLA_QK_WIDTH), GLA_GATE_RANK ** -0.5)
    inp['b_gla_gate'] = nrm((L, GLA_QK_WIDTH), 0.1)
    inp['g_gla_out'] = gain((L, GLA_DV))
    inp['s5_lambda_re'] = -0.5 + nrm((L, S5_GROUPS, S5_N), 0.01)
    inp['s5_lambda_im'] = (math.pi * jnp.broadcast_to(jnp.arange(S5_N, dtype=f32), (L, S5_GROUPS, S5_N))
                           + nrm((L, S5_GROUPS, S5_N), 0.01))
    inp['s5_log_dt'] = jax.random.uniform(next(ks), (L, S5_GROUPS), f32,
                                          math.log(S5_DT_MIN), math.log(S5_DT_MAX))
    inp['s5_b_re'] = nrm((L, S5_GROUPS, S5_N, S5_P), (2.0 * S5_P) ** -0.5)
    inp['s5_b_im'] = nrm((L, S5_GROUPS, S5_N, S5_P), (2.0 * S5_P) ** -0.5)
    inp['s5_c_re'] = nrm((L, S5_GROUPS, S5_P, S5_N), (2.0 * S5_N) ** -0.5)
    inp['s5_c_im'] = nrm((L, S5_GROUPS, S5_P, S5_N), (2.0 * S5_N) ** -0.5)
    inp['s5_d'] = nrm((L, S5_GROUPS, S5_P), 1.0)
    inp['s5_w_glu'] = nrm((L, S5_GROUPS, S5_P, S5_P), S5_P ** -0.5)
    inp['s5_b_glu'] = nrm((L, S5_GROUPS, S5_P), 0.1)
    inp['w_out'] = nrm((L, MIX_WIDTH, D_MODEL), MIX_WIDTH ** -0.5)
    inp['g_cross'] = gain((L, D_MODEL))
    inp['g_mem'] = gain((L, D_MODEL))
    inp['w_cq'] = nrm((L, D_MODEL, D_MODEL), D_MODEL ** -0.5)
    inp['w_ckv'] = nrm((L, D_MODEL, 2 * D_MODEL), D_MODEL ** -0.5)
    inp['w_co'] = nrm((L, D_MODEL, D_MODEL), D_MODEL ** -0.5)
    inp['g_ffn2'] = gain((L, D_MODEL))
    inp['w_ffn2_gu'] = nrm((L, D_MODEL, 2 * D_FF), D_MODEL ** -0.5)
    inp['w_ffn2_down'] = nrm((L, D_FF, D_MODEL), D_FF ** -0.5)
    inp['g_final'] = gain((D_MODEL,))
    return inp


def reference(x_prompt, x_sample, state_gla, state_s5_re, state_s5_im, cache_mem_k, cache_mem_v,
              mem_prompt, g_ffn1, w_ffn1_gu, w_ffn1_down, g_mix, w_in, w_gla_gate, b_gla_gate,
              g_gla_out, s5_lambda_re, s5_lambda_im, s5_log_dt, s5_b_re, s5_b_im, s5_c_re, s5_c_im,
              s5_d, s5_w_glu, s5_b_glu, w_out, g_cross, g_mem, w_cq, w_ckv, w_co, g_ffn2,
              w_ffn2_gu, w_ffn2_down, g_final):
    bp = x_prompt.shape[0]
    f32 = jnp.float32
    hp, hs = x_prompt, x_sample
    gla_p, re_p, im_p, mk_p, mv_p = [], [], [], [], []
    gla_s, re_s, im_s = [], [], []
    for l in range(DEPTH):
        w = {
            'g_ffn1': g_ffn1[l], 'w_ffn1_gu': w_ffn1_gu[l], 'w_ffn1_down': w_ffn1_down[l],
            'g_mix': g_mix[l], 'w_in': w_in[l], 'w_gla_gate': w_gla_gate[l], 'b_gla_gate': b_gla_gate[l],
            'g_gla_out': g_gla_out[l], 's5_lambda_re': s5_lambda_re[l], 's5_lambda_im': s5_lambda_im[l],
            's5_log_dt': s5_log_dt[l], 's5_b_re': s5_b_re[l], 's5_b_im': s5_b_im[l],
            's5_c_re': s5_c_re[l], 's5_c_im': s5_c_im[l], 's5_d': s5_d[l], 's5_w_glu': s5_w_glu[l],
            's5_b_glu': s5_b_glu[l], 'w_out': w_out[l], 'g_cross': g_cross[l], 'w_cq': w_cq[l],
            'w_co': w_co[l], 'g_ffn2': g_ffn2[l], 'w_ffn2_gu': w_ffn2_gu[l], 'w_ffn2_down': w_ffn2_down[l],
        }
        mk, mv = memory_kv(mem_prompt, g_mem[l], w_ckv[l])
        hp, sg, sr, si = trunk_layer(
            hp,
            jnp.zeros((bp, GLA_HEADS, GLA_DK, GLA_DV), f32),
            jnp.zeros((bp, S5_GROUPS, S5_N), f32),
            jnp.zeros((bp, S5_GROUPS, S5_N), f32),
            mk, mv, w)
        gla_p.append(sg); re_p.append(sr); im_p.append(si); mk_p.append(mk); mv_p.append(mv)
        hs, sg, sr, si = trunk_layer(hs, state_gla[l], state_s5_re[l], state_s5_im[l],
                                     cache_mem_k[l], cache_mem_v[l], w)
        gla_s.append(sg); re_s.append(sr); im_s.append(si)
    y_prompt = rms_norm(hp, g_final)
    y_sample = rms_norm(hs, g_final)
    return (y_prompt, y_sample,
            jnp.stack(gla_p), jnp.stack(re_p), jnp.stack(im_p), jnp.stack(mk_p), jnp.stack(mv_p),
            jnp.stack(gla_s), jnp.stack(re_s), jnp.stack(im_s))
```

```python
import functools
import math

import jax
import jax.numpy as jnp
from jax import lax
from jax.experimental import pallas as pl
from jax.experimental.pallas import tpu as pltpu

F32 = jnp.float32
BF16 = jnp.bfloat16

D_MODEL = 1024
DEPTH = 1
EPS = 1e-6
D_FF = 2816
GLA_WIDTH = 512
GLA_HEADS = 4
GLA_DV = 128
GLA_DK = 64
GLA_QK_WIDTH = 256
GLA_GATE_RANK = 16
GLA_TAU = 16.0
S5_WIDTH = 512
S5_P = 16
S5_GROUPS = 32
S5_N = 64
S5_STATE = S5_GROUPS * S5_N
CROSS_HEADS = 4
CROSS_DH = 256
MEM_LEN = 256

LANES = 128
SUBLANES = 8
MXU_DIM = 256
VMEM_LIMIT_BYTES = 56 * 1024 * 1024

FF_CHUNK = MXU_DIM
GLA_CHUNK_ROWS = 64
S5_LANE_CHUNK = 512


def _cparams(n_grid_axes):
    return pltpu.CompilerParams(
        dimension_semantics=("arbitrary",) * n_grid_axes,
        vmem_limit_bytes=VMEM_LIMIT_BYTES,
    )


def _const_spec(shape):
    zeros = (0,) * len(shape)
    return pl.BlockSpec(shape, lambda *_: zeros, pipeline_mode=pl.Buffered(1))


def _dot(a, b):
    return jnp.dot(a, b, preferred_element_type=F32)


def _dot_nt(a, b):
    return lax.dot_general(a, b, (((1,), (1,)), ((), ())), preferred_element_type=F32)


def _dot_tn(a, b):
    return lax.dot_general(a, b, (((0,), (0,)), ((), ())), preferred_element_type=F32)


def _rms(x, g):
    return x * lax.rsqrt(jnp.mean(x * x, axis=-1, keepdims=True) + EPS) * g


def _swiglu(xn_bf, wgu_ref, wdn_ref):
    acc = None
    for c in range(D_FF // FF_CHUNK):
        lo, hi = c * FF_CHUNK, (c + 1) * FF_CHUNK
        g = _dot(xn_bf, wgu_ref[:, lo:hi])
        u = _dot(xn_bf, wgu_ref[:, D_FF + lo:D_FF + hi])
        a = (jax.nn.silu(g) * u).astype(BF16)
        d = _dot(a, wdn_ref[lo:hi, :])
        acc = d if acc is None else acc + d
    return acc


def _log_sigmoid(z):
    return jnp.minimum(z, 0.0) - jnp.log1p(jnp.exp(-jnp.abs(z)))


def _pre_kernel(x_ref, g1_ref, wgu_ref, wdn_ref, gm_ref, win_ref, wzg_ref, wgate_ref, bgate_ref,
                h_ref, qk_ref, v_ref, la_ref, r_ref, us5_ref):
    x = x_ref[...]
    xn = _rms(x, g1_ref[...]).astype(BF16)
    h = x + 0.5 * _swiglu(xn, wgu_ref, wdn_ref)
    h_ref[...] = h
    un = _rms(h, gm_ref[...]).astype(BF16)
    p = _dot(un, win_ref[...])
    qk_ref[...] = p[:, 0:512]
    v_ref[...] = p[:, 512:1024]
    r_ref[...] = p[:, 1024:1536]
    us5_ref[...] = p[:, 1536:2048]
    zg = _dot(un, wzg_ref[...])
    z = _dot(zg.astype(BF16), wgate_ref[...]) + bgate_ref[...]
    la_ref[...] = _log_sigmoid(z) * (1.0 / GLA_TAU)


def _pre_call(x2d, g1, wgu, wdn, gm, win, wzg, wgate, bgate, tm):
    n = x2d.shape[0]
    row = lambda w: pl.BlockSpec((tm, w), lambda i: (i, 0))
    out_shapes = [jax.ShapeDtypeStruct((n, w), F32) for w in (D_MODEL, 512, 512, 256, 512, 512)]
    return pl.pallas_call(
        _pre_kernel,
        grid=(n // tm,),
        in_specs=[row(D_MODEL), _const_spec(g1.shape), _const_spec(wgu.shape), _const_spec(wdn.shape),
                  _const_spec(gm.shape), _const_spec(win.shape), _const_spec(wzg.shape),
                  _const_spec(wgate.shape), _const_spec(bgate.shape)],
        out_specs=[row(D_MODEL), row(512), row(512), row(256), row(512), row(512)],
        out_shape=out_shapes,
        compiler_params=_cparams(1),
        name="pre_ffn_proj",
    )(x2d, g1, wgu, wdn, gm, win, wzg, wgate, bgate)


def _memkv_kernel(m_ref, g_ref, w_ref, k_ref, v_ref):
    mn = _rms(m_ref[...], g_ref[...]).astype(BF16)
    kv = _dot(mn, w_ref[...])
    k_ref[...] = kv[:, :D_MODEL]
    v_ref[...] = kv[:, D_MODEL:]


def _memkv_call(mem2d, g, w, tm):
    n = mem2d.shape[0]
    row = pl.BlockSpec((tm, D_MODEL), lambda i: (i, 0))
    return pl.pallas_call(
        _memkv_kernel,
        grid=(n // tm,),
        in_specs=[row, _const_spec(g.shape), _const_spec(w.shape)],
        out_specs=[row, row],
        out_shape=[jax.ShapeDtypeStruct((n, D_MODEL), F32)] * 2,
        compiler_params=_cparams(1),
        name="memory_kv",
    )(mem2d, g, w)


def _mid_kernel(h_ref, og_ref, os_ref, wo_ref, gc_ref, wq_ref, h2_ref, q_ref):
    h2 = (h_ref[...] + _dot(og_ref[...].astype(BF16), wo_ref[0:GLA_WIDTH, :])
          + _dot(os_ref[...].astype(BF16), wo_ref[GLA_WIDTH:, :]))
    h2_ref[...] = h2
    hn = _rms(h2, gc_ref[...]).astype(BF16)
    q_ref[...] = _dot(hn, wq_ref[...])


def _mid_call(h, og, os_, wo, gc, wq, tm):
    n = h.shape[0]
    row = lambda w: pl.BlockSpec((tm, w), lambda i: (i, 0))
    return pl.pallas_call(
        _mid_kernel,
        grid=(n // tm,),
        in_specs=[row(D_MODEL), row(512), row(512), _const_spec(wo.shape), _const_spec(gc.shape),
                  _const_spec(wq.shape)],
        out_specs=[row(D_MODEL), row(D_MODEL)],
        out_shape=[jax.ShapeDtypeStruct((n, D_MODEL), F32)] * 2,
        compiler_params=_cparams(1),
        name="mix_out_cross_q",
    )(h, og, os_, wo, gc, wq)


def _post_kernel(h2_ref, o_ref, wco_ref, g2_ref, wgu_ref, wdn_ref, gf_ref, y_ref):
    h3 = h2_ref[...] + _dot(o_ref[...].astype(BF16), wco_ref[...])
    hn = _rms(h3, g2_ref[...]).astype(BF16)
    h4 = h3 + 0.5 * _swiglu(hn, wgu_ref, wdn_ref)
    y_ref[...] = _rms(h4, gf_ref[...])


def _post_call(h2, o, wco, g2, wgu, wdn, gf, tm):
    n = h2.shape[0]
    row = pl.BlockSpec((tm, D_MODEL), lambda i: (i, 0))
    return pl.pallas_call(
        _post_kernel,
        grid=(n // tm,),
        in_specs=[row, row, _const_spec(wco.shape), _const_spec(g2.shape), _const_spec(wgu.shape),
                  _const_spec(wdn.shape), _const_spec(gf.shape)],
        out_specs=row,
        out_shape=jax.ShapeDtypeStruct((n, D_MODEL), F32),
        compiler_params=_cparams(1),
        name="post_ffn_final",
    )(h2, o, wco, g2, wgu, wdn, gf)


def _attn_kernel(q_ref, k_ref, v_ref, o_ref, *, n_seq):
    cast = (lambda a: a.astype(BF16)) if q_ref.shape[1] % 16 == 0 else (lambda a: a)
    for j in range(n_seq):
        q = q_ref[j]
        outs = []
        for hd in range(CROSS_HEADS):
            sl = slice(hd * CROSS_DH, (hd + 1) * CROSS_DH)
            s = _dot_nt(cast(q[:, sl]), cast(k_ref[j, :, sl])) * (CROSS_DH ** -0.5)
            s = s - jnp.max(s, axis=-1, keepdims=True)
            e = jnp.exp(s)
            p = e / jnp.sum(e, axis=-1, keepdims=True)
            outs.append(_dot(cast(p), cast(v_ref[j, :, sl])))
        o_ref[j] = jnp.concatenate(outs, axis=-1).astype(o_ref.dtype)


def _attn_call(q3, k3, v3, n_seq, tq):
    s, t, _ = q3.shape
    qspec = pl.BlockSpec((n_seq, tq, D_MODEL), lambda i, j: (i, j, 0))
    kvspec = pl.BlockSpec((n_seq, MEM_LEN, D_MODEL), lambda i, j: (i, 0, 0))
    return pl.pallas_call(
        functools.partial(_attn_kernel, n_seq=n_seq),
        grid=(s // n_seq, t // tq),
        in_specs=[qspec, kvspec, kvspec],
        out_specs=qspec,
        out_shape=jax.ShapeDtypeStruct(q3.shape, BF16 if tq % 16 == 0 else F32),
        compiler_params=_cparams(2),
        name="cross_attention",
    )(q3, k3, v3)


def _split3(x):
    hi = x.astype(BF16)
    r1 = x - hi.astype(F32)
    mid = r1.astype(BF16)
    lo = (r1 - mid.astype(F32)).astype(BF16)
    return hi, mid, lo


def _dot01(m01, x):
    hi, mid, lo = _split3(x)
    return _dot(m01, hi) + _dot(m01, mid) + _dot(m01, lo)


def _iota2(shape, dim):
    return lax.broadcasted_iota(jnp.int32, shape, dim)


def _gla_kernel(*refs, seq_len, n_chunks, n_state, has_s0):
    if has_s0:
        qk_ref, v_ref, la_ref, r_ref, g_ref, s0_ref, o_ref, st_ref, s_scr = refs
    else:
        qk_ref, v_ref, la_ref, r_ref, g_ref, o_ref, st_ref, s_scr = refs
        s0_ref = None
    c_rows = GLA_CHUNK_ROWS
    n_sub = c_rows // seq_len
    t_idx = pl.program_id(1)

    @pl.when(t_idx == 0)
    def _():
        if has_s0:
            s_scr[...] = s0_ref[...]
        else:
            s_scr[...] = jnp.zeros_like(s_scr)

    log2 = lambda n: int(math.log2(n))
    ri = _iota2((c_rows, c_rows), 0)
    ci = _iota2((c_rows, c_rows), 1)
    tri = ((ri >> log2(seq_len)) == (ci >> log2(seq_len))) & (ci <= ri)
    tri = jnp.where(tri, 1.0, 0.0).astype(BF16)
    ai = _iota2((c_rows, GLA_HEADS * c_rows), 0)
    asx = _iota2((c_rows, GLA_HEADS * c_rows), 1) & (c_rows - 1)
    rowq = _iota2((c_rows, GLA_QK_WIDTH), 0)
    kr = _iota2((GLA_HEADS * c_rows, GLA_QK_WIDTH), 0)
    kl = _iota2((GLA_HEADS * c_rows, GLA_QK_WIDTH), 1)
    k_head_mask = (kr >> log2(c_rows)) == (kl >> log2(GLA_DK))
    vr = _iota2((GLA_HEADS * c_rows, GLA_WIDTH), 0)
    vl = _iota2((GLA_HEADS * c_rows, GLA_WIDTH), 1)
    v_head_mask = (vr >> log2(c_rows)) == (vl >> log2(GLA_DV))
    sr = _iota2((GLA_HEADS * GLA_DV, GLA_QK_WIDTH), 0)
    slane = _iota2((GLA_HEADS * GLA_DV, GLA_QK_WIDTH), 1)
    s_head_mask = (sr >> log2(GLA_DV)) == (slane >> log2(GLA_DK))
    u_lane_head = _iota2((GLA_DV, GLA_QK_WIDTH), 1) >> log2(GLA_DK)

    def head_blocks(x, mask):
        return jnp.where(mask, jnp.concatenate([x] * GLA_HEADS, axis=0), 0.0)

    g_out = g_ref[...]
    for c in range(n_chunks):
        rows = slice(c * c_rows, (c + 1) * c_rows)
        q = qk_ref[0, rows, 0:GLA_QK_WIDTH] * (GLA_DK ** -0.5)
        k = qk_ref[0, rows, GLA_QK_WIDTH:2 * GLA_QK_WIDTH]
        v = v_ref[0, rows, :]
        la = la_ref[0, rows, :]
        b = _dot01(tri, la)

        a_mat = jnp.where(ai == asx, _dot_nt(q.astype(BF16), head_blocks(k, k_head_mask).astype(BF16)), 0.0)
        last = b
        m = 1
        while m < seq_len:
            prev_last = pltpu.roll(last, m, axis=0)
            eq = jnp.minimum(b - prev_last, 0.0)
            ek = jnp.minimum(last - b, 0.0)
            qm = (q * jnp.exp(eq)).astype(BF16)
            km = head_blocks(k * jnp.exp(ek), k_head_mask).astype(BF16)
            a_lvl = _dot_nt(qm, km)
            iblk = ai >> log2(m)
            lvl_mask = ((iblk - (asx >> log2(m))) == 1) & ((iblk & 1) == 1)
            a_mat = jnp.where(lvl_mask, a_lvl, a_mat)
            even = ((rowq >> log2(m)) & 1) == 0
            last = jnp.where(even, pltpu.roll(last, c_rows - m, axis=0), last)
            m *= 2
        o_intra = _dot(a_mat.astype(BF16), head_blocks(v, v_head_mask).astype(BF16))

        q_dec = q * jnp.exp(b)
        k_dec = k * jnp.exp(jnp.minimum(last - b, 0.0))
        o_parts = []
        for j in range(n_sub):
            sl = slice(j * seq_len, (j + 1) * seq_len)
            s_t = s_scr[j]
            s_blocks = head_blocks(s_t, s_head_mask)
            o_parts.append(_dot_nt(q_dec[sl], s_blocks))
            u_full = _dot_tn(v[sl], k_dec[sl])
            u_t = jnp.zeros((GLA_DV, GLA_QK_WIDTH), F32)
            for hd in range(GLA_HEADS):
                u_t = jnp.where(u_lane_head == hd, u_full[hd * GLA_DV:(hd + 1) * GLA_DV, :], u_t)
            decay = jnp.exp(last[j * seq_len:j * seq_len + 1, :])
            s_scr[j] = s_t * decay + u_t
        o_inter = o_parts[0] if n_sub == 1 else jnp.concatenate(o_parts, axis=0)
        o = o_intra + o_inter

        r = r_ref[0, rows, :]
        outs = []
        for hd in range(GLA_HEADS):
            sl = slice(hd * GLA_DV, (hd + 1) * GLA_DV)
            outs.append(_rms(o[:, sl], g_out) * jax.nn.silu(r[:, sl]))
        o_ref[0, rows, :] = jnp.concatenate(outs, axis=-1).astype(o_ref.dtype)

    @pl.when(t_idx == pl.num_programs(1) - 1)
    def _():
        st_ref[...] = s_scr[...]


def _gla_call(qk3, v3, la3, r3, g_out, s0_t, seq_len, tile_rows, n_state):
    n_outer, rows, _ = qk3.shape
    n_chunks = tile_rows // GLA_CHUNK_ROWS
    spec = lambda w: pl.BlockSpec((1, tile_rows, w), lambda i, t: (i, t, 0))
    st_spec = pl.BlockSpec((n_state, GLA_DV, GLA_QK_WIDTH), lambda i, t: (i, 0, 0))
    has_s0 = s0_t is not None
    in_specs = [spec(512), spec(512), spec(256), spec(512), _const_spec(g_out.shape)]
    args = [qk3, v3, la3, r3, g_out]
    if has_s0:
        in_specs.append(st_spec)
        args.append(s0_t)
    return pl.pallas_call(
        functools.partial(_gla_kernel, seq_len=seq_len, n_chunks=n_chunks, n_state=n_state, has_s0=has_s0),
        grid=(n_outer, rows // tile_rows),
        in_specs=in_specs,
        out_specs=[spec(512), st_spec],
        out_shape=[jax.ShapeDtypeStruct((n_outer, rows, GLA_WIDTH), BF16),
                   jax.ShapeDtypeStruct((n_outer * n_state, GLA_DV, GLA_QK_WIDTH), F32)],
        scratch_shapes=[pltpu.VMEM((n_state, GLA_DV, GLA_QK_WIDTH), F32)],
        compiler_params=_cparams(2),
        name="gla_mixer",
    )(*args)


def _s5_disc_kernel(lre_ref, lim_ref, ldt_ref, bre_ref, bim_ref, are_ref, aim_ref, bbre_ref, bbim_ref):
    lr = lre_ref[...]
    li = lim_ref[...]
    dt = jnp.exp(ldt_ref[...])
    mag = jnp.exp(dt * lr)
    ab_re = mag * jnp.cos(dt * li)
    ab_im = mag * jnp.sin(dt * li)
    den = lr * lr + li * li
    coef_re = ((ab_re - 1.0) * lr + ab_im * li) / den
    coef_im = (ab_im * lr - (ab_re - 1.0) * li) / den
    are_ref[...] = ab_re
    aim_ref[...] = ab_im
    br = bre_ref[...]
    bi = bim_ref[...]
    cr = coef_re[:, None, :]
    cim = coef_im[:, None, :]
    bbre_ref[...] = cr * br - cim * bi
    bbim_ref[...] = cr * bi + cim * br


def _s5_disc_call(lre, lim, ldt, bre_t, bim_t):
    gn = jax.ShapeDtypeStruct((S5_GROUPS, S5_N), F32)
    gpn = jax.ShapeDtypeStruct((S5_GROUPS, S5_P, S5_N), F32)
    return pl.pallas_call(
        _s5_disc_kernel,
        out_shape=[gn, gn, gpn, gpn],
        name="s5_discretise",
    )(lre, lim, ldt, bre_t, bim_t)


def _s5_kernel(u_ref, x0re_ref, x0im_ref, are_ref, aim_ref, bre_ref, bim_ref, c_ref, d_ref, wglu_ref,
               bglu_ref, o_ref, xre_ref, xim_ref, sre_scr, sim_scr, xre_scr, xim_scr, *, n_seq, t_tile):
    half_w = S5_WIDTH // 2
    half_s = S5_STATE // 2
    rows = t_tile * n_seq

    @pl.when(pl.program_id(0) == 0)
    def _():
        xre_scr[...] = x0re_ref[...]
        xim_scr[...] = x0im_ref[...]

    u = u_ref[...].reshape(rows, S5_WIDTH)
    ub = u.astype(BF16)
    for hf in range(2):
        uh = ub[:, hf * half_w:(hf + 1) * half_w]
        sre_scr[:, :, hf * half_s:(hf + 1) * half_s] = _dot(uh, bre_ref[hf]).reshape(t_tile, n_seq, half_s)
        sim_scr[:, :, hf * half_s:(hf + 1) * half_s] = _dot(uh, bim_ref[hf]).reshape(t_tile, n_seq, half_s)

    for lc in range(S5_STATE // S5_LANE_CHUNK):
        lanes = pl.ds(lc * S5_LANE_CHUNK, S5_LANE_CHUNK)
        ar = jnp.broadcast_to(are_ref[:, lanes], (SUBLANES, S5_LANE_CHUNK))
        ai = jnp.broadcast_to(aim_ref[:, lanes], (SUBLANES, S5_LANE_CHUNK))

        def seq_body(sc, carry, lanes=lanes, ar=ar, ai=ai):
            srows = pl.ds(pl.multiple_of(sc * SUBLANES, SUBLANES), SUBLANES)

            def t_body(t, x):
                xr, xi = x
                nxr = ar * xr - ai * xi + sre_scr[t, srows, lanes]
                nxi = ar * xi + ai * xr + sim_scr[t, srows, lanes]
                sre_scr[t, srows, lanes] = nxr
                sim_scr[t, srows, lanes] = nxi
                return nxr, nxi

            xr, xi = lax.fori_loop(0, t_tile, t_body, (xre_scr[srows, lanes], xim_scr[srows, lanes]))
            xre_scr[srows, lanes] = xr
            xim_scr[srows, lanes] = xi
            return carry

        lax.fori_loop(0, n_seq // SUBLANES, seq_body, 0)

    xr_all = sre_scr[...].reshape(rows, S5_STATE).astype(BF16)
    xi_all = sim_scr[...].reshape(rows, S5_STATE).astype(BF16)
    ys = []
    for hf in range(2):
        st = slice(hf * half_s, (hf + 1) * half_s)
        ys.append(_dot(xr_all[:, st], c_ref[hf, 0:half_s, :]) + _dot(xi_all[:, st], c_ref[hf, half_s:, :]))
    y = jnp.concatenate(ys, axis=-1) + d_ref[...] * u
    z = jax.nn.gelu(y)
    gate = jax.nn.sigmoid(_dot(z.astype(BF16), wglu_ref[...]) + bglu_ref[...])
    o_ref[...] = (z * gate).astype(o_ref.dtype).reshape(t_tile, n_seq, S5_WIDTH)

    @pl.when(pl.program_id(0) == pl.num_programs(0) - 1)
    def _():
        xre_ref[...] = xre_scr[...]
        xim_ref[...] = xim_scr[...]


def _s5_call(u_t, x0re, x0im, are, aim, bre, bim, cmat, dskip, wglu, bglu, t_tile):
    t, n_seq, _ = u_t.shape
    uspec = pl.BlockSpec((t_tile, n_seq, S5_WIDTH), lambda i: (i, 0, 0))
    xspec = pl.BlockSpec((n_seq, S5_STATE), lambda i: (0, 0))
    consts = [are, aim, bre, bim, cmat, dskip, wglu, bglu]
    return pl.pallas_call(
        functools.partial(_s5_kernel, n_seq=n_seq, t_tile=t_tile),
        grid=(t // t_tile,),
        in_specs=[uspec, xspec, xspec] + [_const_spec(a.shape) for a in consts],
        out_specs=[uspec, xspec, xspec],
        out_shape=[jax.ShapeDtypeStruct((t, n_seq, S5_WIDTH), F32),
                   jax.ShapeDtypeStruct((n_seq, S5_STATE), F32),
                   jax.ShapeDtypeStruct((n_seq, S5_STATE), F32)],
        scratch_shapes=[pltpu.VMEM((t_tile, n_seq, S5_STATE), F32), pltpu.VMEM((t_tile, n_seq, S5_STATE), F32),
                        pltpu.VMEM((n_seq, S5_STATE), F32), pltpu.VMEM((n_seq, S5_STATE), F32)],
        compiler_params=_cparams(1),
        name="s5_mixer",
    )(u_t, x0re, x0im, *consts)


def _block_diag(blocks):
    g, a, b = blocks.shape
    eye = jnp.eye(g, dtype=blocks.dtype)
    return (eye[:, None, :, None] * blocks[:, :, None, :]).reshape(g * a, g * b)


def _halves_block_diag(blocks):
    g = blocks.shape[0]
    return jnp.stack([_block_diag(blocks[:g // 2]), _block_diag(blocks[g // 2:])])


def _trunk_group(x3, s_gla0, x0re, x0im, mem_k3, mem_v3, w, *, tm, gla_tile, gla_states, s5_t_tile,
                 attn_seqs, attn_tq):
    s, t, _ = x3.shape
    n = s * t
    h, qk, v, la, r, us5 = _pre_call(x3.reshape(n, D_MODEL), w['g_ffn1'], w['w_ffn1_gu'], w['w_ffn1_down'],
                                     w['g_mix'], w['w_in_main'], w['w_zg'], w['w_gate'], w['b_gate'], tm)
    if gla_states == 1:
        shp = lambda a: a.reshape(s, t, a.shape[-1])
    else:
        rows = gla_states * t
        shp = lambda a: a.reshape(n // rows, rows, a.shape[-1])
    o_gla, st_new = _gla_call(shp(qk), shp(v), shp(la), shp(r), w['g_gla_out'], s_gla0, t if gla_states > 1 else GLA_CHUNK_ROWS,
                              gla_tile, gla_states)
    u_t = jnp.transpose(us5.reshape(s, t, S5_WIDTH), (1, 0, 2))
    o_s5_t, xre, xim = _s5_call(u_t, x0re, x0im, w['s5_are'], w['s5_aim'], w['s5_bre'], w['s5_bim'], w['s5_c'],
                                w['s5_d'], w['s5_wglu'], w['s5_bglu'], s5_t_tile)
    o_s5 = jnp.transpose(o_s5_t, (1, 0, 2)).reshape(n, S5_WIDTH)
    h2, q = _mid_call(h, o_gla.reshape(n, GLA_WIDTH), o_s5, w['w_out'], w['g_cross'], w['w_cq'], tm)
    o_att = _attn_call(q.reshape(s, t, D_MODEL), mem_k3, mem_v3, attn_seqs, attn_tq)
    y = _post_call(h2, o_att.reshape(n, D_MODEL), w['w_co'], w['g_ffn2'], w['w_ffn2_gu'], w['w_ffn2_down'],
                   w['g_final'], tm)
    return y.reshape(s, t, D_MODEL), st_new, xre, xim


def kernel(x_prompt, x_sample, state_gla, state_s5_re, state_s5_im, cache_mem_k, cache_mem_v, mem_prompt, g_ffn1, w_ffn1_gu, w_ffn1_down, g_mix, w_in, w_gla_gate, b_gla_gate, g_gla_out, s5_lambda_re, s5_lambda_im, s5_log_dt, s5_b_re, s5_b_im, s5_c_re, s5_c_im, s5_d, s5_w_glu, s5_b_glu, w_out, g_cross, g_mem, w_cq, w_ckv, w_co, g_ffn2, w_ffn2_gu, w_ffn2_down, g_final):
    bp, tp, _ = x_prompt.shape
    bs, ts, _ = x_sample.shape
    l = 0
    row = lambda a: a.reshape(1, -1).astype(F32)
    bf = lambda a: a.astype(BF16)

    are, aim, bbre, bbim = _s5_disc_call(
        s5_lambda_re[l], s5_lambda_im[l], s5_log_dt[l].reshape(S5_GROUPS, 1),
        jnp.transpose(s5_b_re[l], (0, 2, 1)), jnp.transpose(s5_b_im[l], (0, 2, 1)))
    c_re_t = jnp.transpose(s5_c_re[l], (0, 2, 1))
    c_im_t = jnp.transpose(s5_c_im[l], (0, 2, 1))
    cre_h = _halves_block_diag(c_re_t)
    cim_h = _halves_block_diag(-c_im_t)
    w_in_l = w_in[l]
    zg_lo = 2 * GLA_QK_WIDTH + GLA_WIDTH
    zg_hi = zg_lo + GLA_GATE_RANK
    w = {
        'g_ffn1': row(g_ffn1[l]), 'w_ffn1_gu': bf(w_ffn1_gu[l]), 'w_ffn1_down': bf(w_ffn1_down[l]),
        'g_mix': row(g_mix[l]),
        'w_in_main': bf(jnp.concatenate([w_in_l[:, :zg_lo], w_in_l[:, zg_hi:]], axis=1)),
        'w_zg': bf(jnp.pad(w_in_l[:, zg_lo:zg_hi], ((0, 0), (0, LANES - GLA_GATE_RANK)))),
        'w_gate': bf(jnp.pad(w_gla_gate[l], ((0, LANES - GLA_GATE_RANK), (0, 0)))),
        'b_gate': row(b_gla_gate[l]),
        'g_gla_out': row(g_gla_out[l]),
        's5_are': are.reshape(1, S5_STATE), 's5_aim': aim.reshape(1, S5_STATE),
        's5_bre': bf(_halves_block_diag(bbre)), 's5_bim': bf(_halves_block_diag(bbim)),
        's5_c': bf(jnp.concatenate([cre_h, cim_h], axis=1)),
        's5_d': row(s5_d[l]), 's5_wglu': bf(_block_diag(s5_w_glu[l])), 's5_bglu': row(s5_b_glu[l]),
        'w_out': bf(w_out[l]), 'g_cross': row(g_cross[l]), 'w_cq': bf(w_cq[l]), 'w_co': bf(w_co[l]),
        'g_ffn2': row(g_ffn2[l]), 'w_ffn2_gu': bf(w_ffn2_gu[l]), 'w_ffn2_down': bf(w_ffn2_down[l]),
        'g_final': row(g_final),
    }

    mk, mv = _memkv_call(mem_prompt.reshape(bp * MEM_LEN, D_MODEL), row(g_mem[l]), bf(w_ckv[l]), 512)
    zeros_state = jnp.zeros((bp, S5_STATE), F32)
    y_p, st_p, re_p, im_p = _trunk_group(
        x_prompt, None, zeros_state, zeros_state, mk.reshape(bp, MEM_LEN, D_MODEL), mv.reshape(bp, MEM_LEN, D_MODEL),
        w, tm=512, gla_tile=256, gla_states=1, s5_t_tile=128, attn_seqs=1, attn_tq=512)

    seqs_per_chunk = GLA_CHUNK_ROWS // ts
    s0_t = jnp.transpose(state_gla[l], (0, 3, 1, 2)).reshape(bs, GLA_DV, GLA_QK_WIDTH)
    y_s, st_s, re_s, im_s = _trunk_group(
        x_sample, s0_t, state_s5_re[l].reshape(bs, S5_STATE), state_s5_im[l].reshape(bs, S5_STATE),
        cache_mem_k[l].reshape(bs, MEM_LEN, D_MODEL), cache_mem_v[l].reshape(bs, MEM_LEN, D_MODEL),
        w, tm=512, gla_tile=GLA_CHUNK_ROWS, gla_states=seqs_per_chunk, s5_t_tile=ts, attn_seqs=4, attn_tq=ts)

    def gla_state_out(st, nb):
        return jnp.transpose(st.reshape(nb, GLA_DV, GLA_HEADS, GLA_DK), (0, 2, 3, 1))[None]

    s5_out = lambda a, nb: a.reshape(1, nb, S5_GROUPS, S5_N)
    kv_out = lambda a: a.reshape(1, bp, MEM_LEN, CROSS_HEADS, CROSS_DH)
    return (y_p, y_s,
            gla_state_out(st_p, bp), s5_out(re_p, bp), s5_out(im_p, bp), kv_out(mk), kv_out(mv),
            gla_state_out(st_s, bs), s5_out(re_s, bs), s5_out(im_s, bs))
```

```python
import functools
import math

import jax
import jax.numpy as jnp
from jax import lax
from jax.experimental import pallas as pl
from jax.experimental.pallas import tpu as pltpu

F32 = jnp.float32
BF16 = jnp.bfloat16

D_MODEL = 1024
DEPTH = 1
EPS = 1e-6
D_FF = 2816
GLA_WIDTH = 512
GLA_HEADS = 4
GLA_DV = 128
GLA_DK = 64
GLA_QK_WIDTH = 256
GLA_GATE_RANK = 16
GLA_TAU = 16.0
S5_WIDTH = 512
S5_P = 16
S5_GROUPS = 32
S5_N = 64
S5_STATE = S5_GROUPS * S5_N
CROSS_HEADS = 4
CROSS_DH = 256
MEM_LEN = 256

LANES = 128
SUBLANES = 8
MXU_DIM = 256
VMEM_LIMIT_BYTES = 56 * 1024 * 1024

FF_CHUNK = MXU_DIM
GLA_CHUNK_ROWS = 64
S5_LANE_CHUNK = 512


def _cparams(n_grid_axes):
    return pltpu.CompilerParams(
        dimension_semantics=("arbitrary",) * n_grid_axes,
        vmem_limit_bytes=VMEM_LIMIT_BYTES,
    )


def _const_spec(shape):
    zeros = (0,) * len(shape)
    return pl.BlockSpec(shape, lambda *_: zeros, pipeline_mode=pl.Buffered(1))


def _dot(a, b):
    return jnp.dot(a, b, preferred_element_type=F32)


def _dot_nt(a, b):
    return lax.dot_general(a, b, (((1,), (1,)), ((), ())), preferred_element_type=F32)


def _dot_tn(a, b):
    return lax.dot_general(a, b, (((0,), (0,)), ((), ())), preferred_element_type=F32)


def _rms(x, g):
    return x * lax.rsqrt(jnp.mean(x * x, axis=-1, keepdims=True) + EPS) * g


def _swiglu(xn_bf, wgu_ref, wdn_ref):
    acc = None
    for c in range(D_FF // FF_CHUNK):
        lo, hi = c * FF_CHUNK, (c + 1) * FF_CHUNK
        g = _dot(xn_bf, wgu_ref[:, lo:hi])
        u = _dot(xn_bf, wgu_ref[:, D_FF + lo:D_FF + hi])
        a = (jax.nn.silu(g) * u).astype(BF16)
        d = _dot(a, wdn_ref[lo:hi, :])
        acc = d if acc is None else acc + d
    return acc


def _log_sigmoid(z):
    return jnp.minimum(z, 0.0) - jnp.log1p(jnp.exp(-jnp.abs(z)))


def _pre_kernel(x_ref, g1_ref, wgu_ref, wdn_ref, gm_ref, win_ref, wzg_ref, wgate_ref, bgate_ref,
                h_ref, qk_ref, v_ref, la_ref, r_ref, us5_ref):
    x = x_ref[...]
    xn = _rms(x, g1_ref[...]).astype(BF16)
    h = x + 0.5 * _swiglu(xn, wgu_ref, wdn_ref)
    h_ref[...] = h
    un = _rms(h, gm_ref[...]).astype(BF16)
    p = _dot(un, win_ref[...])
    qk_ref[...] = p[:, 0:512]
    v_ref[...] = p[:, 512:1024]
    r_ref[...] = p[:, 1024:1536]
    us5_ref[...] = p[:, 1536:2048]
    zg = _dot(un, wzg_ref[...])
    z = _dot(zg.astype(BF16), wgate_ref[...]) + bgate_ref[...]
    la_ref[...] = _log_sigmoid(z) * (1.0 / GLA_TAU)


def _pre_call(x2d, g1, wgu, wdn, gm, win, wzg, wgate, bgate, tm):
    n = x2d.shape[0]
    row = lambda w: pl.BlockSpec((tm, w), lambda i: (i, 0))
    out_shapes = [jax.ShapeDtypeStruct((n, w), F32) for w in (D_MODEL, 512, 512, 256, 512, 512)]
    return pl.pallas_call(
        _pre_kernel,
        grid=(n // tm,),
        in_specs=[row(D_MODEL), _const_spec(g1.shape), _const_spec(wgu.shape), _const_spec(wdn.shape),
                  _const_spec(gm.shape), _const_spec(win.shape), _const_spec(wzg.shape),
                  _const_spec(wgate.shape), _const_spec(bgate.shape)],
        out_specs=[row(D_MODEL), row(512), row(512), row(256), row(512), row(512)],
        out_shape=out_shapes,
        compiler_params=_cparams(1),
        name="pre_ffn_proj",
    )(x2d, g1, wgu, wdn, gm, win, wzg, wgate, bgate)


def _memkv_kernel(m_ref, g_ref, w_ref, k_ref, v_ref):
    mn = _rms(m_ref[...], g_ref[...]).astype(BF16)
    kv = _dot(mn, w_ref[...])
    k_ref[...] = kv[:, :D_MODEL]
    v_ref[...] = kv[:, D_MODEL:]


def _memkv_call(mem2d, g, w, tm):
    n = mem2d.shape[0]
    row = pl.BlockSpec((tm, D_MODEL), lambda i: (i, 0))
    return pl.pallas_call(
        _memkv_kernel,
        grid=(n // tm,),
        in_specs=[row, _const_spec(g.shape), _const_spec(w.shape)],
        out_specs=[row, row],
        out_shape=[jax.ShapeDtypeStruct((n, D_MODEL), F32)] * 2,
        compiler_params=_cparams(1),
        name="memory_kv",
    )(mem2d, g, w)


def _mid_kernel(h_ref, og_ref, os_ref, wo_ref, gc_ref, wq_ref, h2_ref, q_ref):
    h2 = (h_ref[...] + _dot(og_ref[...].astype(BF16), wo_ref[0:GLA_WIDTH, :])
          + _dot(os_ref[...].astype(BF16), wo_ref[GLA_WIDTH:, :]))
    h2_ref[...] = h2
    hn = _rms(h2, gc_ref[...]).astype(BF16)
    q_ref[...] = _dot(hn, wq_ref[...])


def _mid_call(h, og, os_, wo, gc, wq, tm):
    n = h.shape[0]
    row = lambda w: pl.BlockSpec((tm, w), lambda i: (i, 0))
    return pl.pallas_call(
        _mid_kernel,
        grid=(n // tm,),
        in_specs=[row(D_MODEL), row(512), row(512), _const_spec(wo.shape), _const_spec(gc.shape),
                  _const_spec(wq.shape)],
        out_specs=[row(D_MODEL), row(D_MODEL)],
        out_shape=[jax.ShapeDtypeStruct((n, D_MODEL), F32)] * 2,
        compiler_params=_cparams(1),
        name="mix_out_cross_q",
    )(h, og, os_, wo, gc, wq)


def _post_kernel(h2_ref, o_ref, wco_ref, g2_ref, wgu_ref, wdn_ref, gf_ref, y_ref):
    h3 = h2_ref[...] + _dot(o_ref[...].astype(BF16), wco_ref[...])
    hn = _rms(h3, g2_ref[...]).astype(BF16)
    h4 = h3 + 0.5 * _swiglu(hn, wgu_ref, wdn_ref)
    y_ref[...] = _rms(h4, gf_ref[...])


def _post_call(h2, o, wco, g2, wgu, wdn, gf, tm):
    n = h2.shape[0]
    row = pl.BlockSpec((tm, D_MODEL), lambda i: (i, 0))
    return pl.pallas_call(
        _post_kernel,
        grid=(n // tm,),
        in_specs=[row, row, _const_spec(wco.shape), _const_spec(g2.shape), _const_spec(wgu.shape),
                  _const_spec(wdn.shape), _const_spec(gf.shape)],
        out_specs=row,
        out_shape=jax.ShapeDtypeStruct((n, D_MODEL), F32),
        compiler_params=_cparams(1),
        name="post_ffn_final",
    )(h2, o, wco, g2, wgu, wdn, gf)


def _softmax(s):
    e = jnp.exp(s - jnp.max(s, axis=-1, keepdims=True))
    return e / jnp.sum(e, axis=-1, keepdims=True)


def _attn_packed_kernel(q_ref, k_ref, v_ref, o_ref, *, n_seq):
    t = q_ref.shape[1]
    shape = (CROSS_HEADS * t, D_MODEL)
    q_mask = (_iota2(shape, 0) // t) == (_iota2(shape, 1) // CROSS_DH)
    o_lane_head = _iota2((t, D_MODEL), 1) // CROSS_DH
    scores = []
    for j in range(n_seq):
        q_stack = jnp.where(q_mask, jnp.concatenate([q_ref[j]] * CROSS_HEADS, axis=0), 0.0).astype(BF16)
        scores.append(_dot_nt(q_stack, k_ref[j].astype(BF16)) * (CROSS_DH ** -0.5))
    probs = [_softmax(s).astype(BF16) for s in scores]
    for j in range(n_seq):
        full = _dot(probs[j], v_ref[j].astype(BF16))
        o = jnp.zeros((t, D_MODEL), F32)
        for hd in range(CROSS_HEADS):
            o = jnp.where(o_lane_head == hd, full[hd * t:(hd + 1) * t, :], o)
        o_ref[j] = o.astype(o_ref.dtype)


def _attn_kernel(q_ref, k_ref, v_ref, o_ref, *, n_seq):
    cast = lambda a: a.astype(BF16)
    for j in range(n_seq):
        q = q_ref[j]
        outs = []
        for hd in range(CROSS_HEADS):
            sl = slice(hd * CROSS_DH, (hd + 1) * CROSS_DH)
            s = _dot_nt(cast(q[:, sl]), cast(k_ref[j, :, sl])) * (CROSS_DH ** -0.5)
            p = _softmax(s)
            outs.append(_dot(cast(p), cast(v_ref[j, :, sl])))
        o_ref[j] = jnp.concatenate(outs, axis=-1).astype(o_ref.dtype)


def _attn_call(q3, k3, v3, n_seq, tq):
    s, t, _ = q3.shape
    qspec = pl.BlockSpec((n_seq, tq, D_MODEL), lambda i, j: (i, j, 0))
    kvspec = pl.BlockSpec((n_seq, MEM_LEN, D_MODEL), lambda i, j: (i, 0, 0))
    return pl.pallas_call(
        functools.partial(_attn_packed_kernel if tq * CROSS_HEADS <= MXU_DIM else _attn_kernel, n_seq=n_seq),
        grid=(s // n_seq, t // tq),
        in_specs=[qspec, kvspec, kvspec],
        out_specs=qspec,
        out_shape=jax.ShapeDtypeStruct(q3.shape, BF16 if tq % 16 == 0 else F32),
        compiler_params=_cparams(2),
        name="cross_attention",
    )(q3, k3, v3)


def _split3(x):
    hi = x.astype(BF16)
    r1 = x - hi.astype(F32)
    mid = r1.astype(BF16)
    lo = (r1 - mid.astype(F32)).astype(BF16)
    return hi, mid, lo


def _dot01(m01, x):
    hi, mid, lo = _split3(x)
    return _dot(m01, hi) + _dot(m01, mid) + _dot(m01, lo)


def _iota2(shape, dim):
    return lax.broadcasted_iota(jnp.int32, shape, dim)


def _gla_kernel(*refs, seq_len, n_chunks, n_state, has_s0):
    if has_s0:
        qk_ref, v_ref, la_ref, r_ref, g_ref, s0_ref, o_ref, st_ref, s_scr = refs
    else:
        qk_ref, v_ref, la_ref, r_ref, g_ref, o_ref, st_ref, s_scr = refs
        s0_ref = None
    c_rows = GLA_CHUNK_ROWS
    n_sub = c_rows // seq_len
    t_idx = pl.program_id(1)

    @pl.when(t_idx == 0)
    def _():
        if has_s0:
            s_scr[...] = s0_ref[...]
        else:
            s_scr[...] = jnp.zeros_like(s_scr)

    log2 = lambda n: int(math.log2(n))
    ri = _iota2((c_rows, c_rows), 0)
    ci = _iota2((c_rows, c_rows), 1)
    tri = ((ri >> log2(seq_len)) == (ci >> log2(seq_len))) & (ci <= ri)
    tri = jnp.where(tri, 1.0, 0.0).astype(BF16)
    ai = _iota2((c_rows, GLA_HEADS * c_rows), 0)
    asx = _iota2((c_rows, GLA_HEADS * c_rows), 1) & (c_rows - 1)
    rowq = _iota2((c_rows, GLA_QK_WIDTH), 0)
    kr = _iota2((GLA_HEADS * c_rows, GLA_QK_WIDTH), 0)
    kl = _iota2((GLA_HEADS * c_rows, GLA_QK_WIDTH), 1)
    k_head_mask = (kr >> log2(c_rows)) == (kl >> log2(GLA_DK))
    vr = _iota2((GLA_HEADS * c_rows, GLA_WIDTH), 0)
    vl = _iota2((GLA_HEADS * c_rows, GLA_WIDTH), 1)
    v_head_mask = (vr >> log2(c_rows)) == (vl >> log2(GLA_DV))
    sr = _iota2((GLA_HEADS * GLA_DV, GLA_QK_WIDTH), 0)
    slane = _iota2((GLA_HEADS * GLA_DV, GLA_QK_WIDTH), 1)
    s_head_mask = (sr >> log2(GLA_DV)) == (slane >> log2(GLA_DK))
    u_lane_head = _iota2((GLA_DV, GLA_QK_WIDTH), 1) >> log2(GLA_DK)

    def head_blocks(x, mask):
        return jnp.where(mask, jnp.concatenate([x] * GLA_HEADS, axis=0), 0.0)

    g_out = g_ref[...]
    for c in range(n_chunks):
        rows = slice(c * c_rows, (c + 1) * c_rows)
        q = qk_ref[0, rows, 0:GLA_QK_WIDTH] * (GLA_DK ** -0.5)
        k = qk_ref[0, rows, GLA_QK_WIDTH:2 * GLA_QK_WIDTH]
        v = v_ref[0, rows, :]
        la = la_ref[0, rows, :]
        b = _dot01(tri, la)

        a_mat = jnp.where(ai == asx, _dot_nt(q.astype(BF16), head_blocks(k, k_head_mask).astype(BF16)), 0.0)
        last = b
        m = 1
        while m < seq_len:
            prev_last = pltpu.roll(last, m, axis=0)
            eq = jnp.minimum(b - prev_last, 0.0)
            ek = jnp.minimum(last - b, 0.0)
            qm = (q * jnp.exp(eq)).astype(BF16)
            km = head_blocks(k * jnp.exp(ek), k_head_mask).astype(BF16)
            a_lvl = _dot_nt(qm, km)
            iblk = ai >> log2(m)
            lvl_mask = ((iblk - (asx >> log2(m))) == 1) & ((iblk & 1) == 1)
            a_mat = jnp.where(lvl_mask, a_lvl, a_mat)
            even = ((rowq >> log2(m)) & 1) == 0
            last = jnp.where(even, pltpu.roll(last, c_rows - m, axis=0), last)
            m *= 2
        o_intra = _dot(a_mat.astype(BF16), head_blocks(v, v_head_mask).astype(BF16))

        q_dec = q * jnp.exp(b)
        k_dec = k * jnp.exp(jnp.minimum(last - b, 0.0))
        o_parts = []
        for j in range(n_sub):
            sl = slice(j * seq_len, (j + 1) * seq_len)
            s_t = s_scr[j]
            s_blocks = head_blocks(s_t, s_head_mask)
            o_parts.append(_dot_nt(q_dec[sl], s_blocks))
            u_full = _dot_tn(v[sl], k_dec[sl])
            u_t = jnp.zeros((GLA_DV, GLA_QK_WIDTH), F32)
            for hd in range(GLA_HEADS):
                u_t = jnp.where(u_lane_head == hd, u_full[hd * GLA_DV:(hd + 1) * GLA_DV, :], u_t)
            decay = jnp.exp(last[j * seq_len:j * seq_len + 1, :])
            s_scr[j] = s_t * decay + u_t
        o_inter = o_parts[0] if n_sub == 1 else jnp.concatenate(o_parts, axis=0)
        o = o_intra + o_inter

        r = r_ref[0, rows, :]
        outs = []
        for hd in range(GLA_HEADS):
            sl = slice(hd * GLA_DV, (hd + 1) * GLA_DV)
            outs.append(_rms(o[:, sl], g_out) * jax.nn.silu(r[:, sl]))
        o_ref[0, rows, :] = jnp.concatenate(outs, axis=-1).astype(o_ref.dtype)

    @pl.when(t_idx == pl.num_programs(1) - 1)
    def _():
        st_ref[...] = s_scr[...]


def _gla_call(qk3, v3, la3, r3, g_out, s0_t, seq_len, tile_rows, n_state):
    n_outer, rows, _ = qk3.shape
    n_chunks = tile_rows // GLA_CHUNK_ROWS
    spec = lambda w: pl.BlockSpec((1, tile_rows, w), lambda i, t: (i, t, 0))
    st_spec = pl.BlockSpec((n_state, GLA_DV, GLA_QK_WIDTH), lambda i, t: (i, 0, 0))
    has_s0 = s0_t is not None
    in_specs = [spec(512), spec(512), spec(256), spec(512), _const_spec(g_out.shape)]
    args = [qk3, v3, la3, r3, g_out]
    if has_s0:
        in_specs.append(st_spec)
        args.append(s0_t)
    return pl.pallas_call(
        functools.partial(_gla_kernel, seq_len=seq_len, n_chunks=n_chunks, n_state=n_state, has_s0=has_s0),
        grid=(n_outer, rows // tile_rows),
        in_specs=in_specs,
        out_specs=[spec(512), st_spec],
        out_shape=[jax.ShapeDtypeStruct((n_outer, rows, GLA_WIDTH), BF16),
                   jax.ShapeDtypeStruct((n_outer * n_state, GLA_DV, GLA_QK_WIDTH), F32)],
        scratch_shapes=[pltpu.VMEM((n_state, GLA_DV, GLA_QK_WIDTH), F32)],
        compiler_params=_cparams(2),
        name="gla_mixer",
    )(*args)


def _s5_disc_kernel(lre_ref, lim_ref, ldt_ref, bre_ref, bim_ref, are_ref, aim_ref, bbre_ref, bbim_ref):
    lr = lre_ref[...]
    li = lim_ref[...]
    dt = jnp.exp(ldt_ref[...])
    mag = jnp.exp(dt * lr)
    ab_re = mag * jnp.cos(dt * li)
    ab_im = mag * jnp.sin(dt * li)
    den = lr * lr + li * li
    coef_re = ((ab_re - 1.0) * lr + ab_im * li) / den
    coef_im = (ab_im * lr - (ab_re - 1.0) * li) / den
    are_ref[...] = ab_re
    aim_ref[...] = ab_im
    br = bre_ref[...]
    bi = bim_ref[...]
    cr = coef_re[:, None, :]
    cim = coef_im[:, None, :]
    bbre_ref[...] = cr * br - cim * bi
    bbim_ref[...] = cr * bi + cim * br


def _s5_disc_call(lre, lim, ldt, bre_t, bim_t):
    gn = jax.ShapeDtypeStruct((S5_GROUPS, S5_N), F32)
    gpn = jax.ShapeDtypeStruct((S5_GROUPS, S5_P, S5_N), F32)
    return pl.pallas_call(
        _s5_disc_kernel,
        out_shape=[gn, gn, gpn, gpn],
        name="s5_discretise",
    )(lre, lim, ldt, bre_t, bim_t)


def _s5_kernel(u_ref, x0re_ref, x0im_ref, are_ref, aim_ref, bre_ref, bim_ref, c_ref, d_ref, wglu_ref,
               bglu_ref, o_ref, xre_ref, xim_ref, sre_scr, sim_scr, xre_scr, xim_scr, *, n_seq, t_tile):
    half_w = S5_WIDTH // 2
    half_s = S5_STATE // 2
    rows = t_tile * n_seq

    @pl.when(pl.program_id(0) == 0)
    def _():
        xre_scr[...] = x0re_ref[...]
        xim_scr[...] = x0im_ref[...]

    u = u_ref[...].reshape(rows, S5_WIDTH)
    ub = u.astype(BF16)
    for hf in range(2):
        uh = ub[:, hf * half_w:(hf + 1) * half_w]
        sre_scr[:, :, hf * half_s:(hf + 1) * half_s] = _dot(uh, bre_ref[hf]).reshape(t_tile, n_seq, half_s)
        sim_scr[:, :, hf * half_s:(hf + 1) * half_s] = _dot(uh, bim_ref[hf]).reshape(t_tile, n_seq, half_s)

    for lc in range(S5_STATE // S5_LANE_CHUNK):
        lanes = pl.ds(lc * S5_LANE_CHUNK, S5_LANE_CHUNK)
        ar = jnp.broadcast_to(are_ref[:, lanes], (SUBLANES, S5_LANE_CHUNK))
        ai = jnp.broadcast_to(aim_ref[:, lanes], (SUBLANES, S5_LANE_CHUNK))

        def seq_body(sc, carry, lanes=lanes, ar=ar, ai=ai):
            srows = pl.ds(pl.multiple_of(sc * SUBLANES, SUBLANES), SUBLANES)

            def t_body(t, x):
                xr, xi = x
                nxr = ar * xr - ai * xi + sre_scr[t, srows, lanes]
                nxi = ar * xi + ai * xr + sim_scr[t, srows, lanes]
                sre_scr[t, srows, lanes] = nxr
                sim_scr[t, srows, lanes] = nxi
                return nxr, nxi

            xr, xi = lax.fori_loop(0, t_tile, t_body, (xre_scr[srows, lanes], xim_scr[srows, lanes]))
            xre_scr[srows, lanes] = xr
            xim_scr[srows, lanes] = xi
            return carry

        lax.fori_loop(0, n_seq // SUBLANES, seq_body, 0)

    xr_all = sre_scr[...].reshape(rows, S5_STATE).astype(BF16)
    xi_all = sim_scr[...].reshape(rows, S5_STATE).astype(BF16)
    ys = []
    for hf in range(2):
        st = slice(hf * half_s, (hf + 1) * half_s)
        ys.append(_dot(xr_all[:, st], c_ref[hf, 0:half_s, :]) + _dot(xi_all[:, st], c_ref[hf, half_s:, :]))
    y = jnp.concatenate(ys, axis=-1) + d_ref[...] * u
    z = jax.nn.gelu(y)
    gate = jax.nn.sigmoid(_dot(z.astype(BF16), wglu_ref[...]) + bglu_ref[...])
    o_ref[...] = (z * gate).astype(o_ref.dtype).reshape(t_tile, n_seq, S5_WIDTH)

    @pl.when(pl.program_id(0) == pl.num_programs(0) - 1)
    def _():
        xre_ref[...] = xre_scr[...]
        xim_ref[...] = xim_scr[...]


def _s5_call(u_t, x0re, x0im, are, aim, bre, bim, cmat, dskip, wglu, bglu, t_tile):
    t, n_seq, _ = u_t.shape
    uspec = pl.BlockSpec((t_tile, n_seq, S5_WIDTH), lambda i: (i, 0, 0))
    xspec = pl.BlockSpec((n_seq, S5_STATE), lambda i: (0, 0))
    consts = [are, aim, bre, bim, cmat, dskip, wglu, bglu]
    return pl.pallas_call(
        functools.partial(_s5_kernel, n_seq=n_seq, t_tile=t_tile),
        grid=(t // t_tile,),
        in_specs=[uspec, xspec, xspec] + [_const_spec(a.shape) for a in consts],
        out_specs=[uspec, xspec, xspec],
        out_shape=[jax.ShapeDtypeStruct((t, n_seq, S5_WIDTH), F32),
                   jax.ShapeDtypeStruct((n_seq, S5_STATE), F32),
                   jax.ShapeDtypeStruct((n_seq, S5_STATE), F32)],
        scratch_shapes=[pltpu.VMEM((t_tile, n_seq, S5_STATE), F32), pltpu.VMEM((t_tile, n_seq, S5_STATE), F32),
                        pltpu.VMEM((n_seq, S5_STATE), F32), pltpu.VMEM((n_seq, S5_STATE), F32)],
        compiler_params=_cparams(1),
        name="s5_mixer",
    )(u_t, x0re, x0im, *consts)


def _block_diag(blocks):
    g, a, b = blocks.shape
    eye = jnp.eye(g, dtype=blocks.dtype)
    return (eye[:, None, :, None] * blocks[:, :, None, :]).reshape(g * a, g * b)


def _halves_block_diag(blocks):
    g = blocks.shape[0]
    return jnp.stack([_block_diag(blocks[:g // 2]), _block_diag(blocks[g // 2:])])


def _trunk_group(x3, s_gla0, x0re, x0im, mem_k3, mem_v3, w, *, tm, gla_tile, gla_states, s5_t_tile,
                 attn_seqs, attn_tq):
    s, t, _ = x3.shape
    n = s * t
    h, qk, v, la, r, us5 = _pre_call(x3.reshape(n, D_MODEL), w['g_ffn1'], w['w_ffn1_gu'], w['w_ffn1_down'],
                                     w['g_mix'], w['w_in_main'], w['w_zg'], w['w_gate'], w['b_gate'], tm)
    if gla_states == 1:
        shp = lambda a: a.reshape(s, t, a.shape[-1])
    else:
        rows = gla_states * t
        shp = lambda a: a.reshape(n // rows, rows, a.shape[-1])
    o_gla, st_new = _gla_call(shp(qk), shp(v), shp(la), shp(r), w['g_gla_out'], s_gla0, t if gla_states > 1 else GLA_CHUNK_ROWS,
                              gla_tile, gla_states)
    u_t = jnp.transpose(us5.reshape(s, t, S5_WIDTH), (1, 0, 2))
    o_s5_t, xre, xim = _s5_call(u_t, x0re, x0im, w['s5_are'], w['s5_aim'], w['s5_bre'], w['s5_bim'], w['s5_c'],
                                w['s5_d'], w['s5_wglu'], w['s5_bglu'], s5_t_tile)
    o_s5 = jnp.transpose(o_s5_t, (1, 0, 2)).reshape(n, S5_WIDTH)
    h2, q = _mid_call(h, o_gla.reshape(n, GLA_WIDTH), o_s5, w['w_out'], w['g_cross'], w['w_cq'], tm)
    o_att = _attn_call(q.reshape(s, t, D_MODEL), mem_k3, mem_v3, attn_seqs, attn_tq)
    y = _post_call(h2, o_att.reshape(n, D_MODEL), w['w_co'], w['g_ffn2'], w['w_ffn2_gu'], w['w_ffn2_down'],
                   w['g_final'], tm)
    return y.reshape(s, t, D_MODEL), st_new, xre, xim


def kernel(x_prompt, x_sample, state_gla, state_s5_re, state_s5_im, cache_mem_k, cache_mem_v, mem_prompt, g_ffn1, w_ffn1_gu, w_ffn1_down, g_mix, w_in, w_gla_gate, b_gla_gate, g_gla_out, s5_lambda_re, s5_lambda_im, s5_log_dt, s5_b_re, s5_b_im, s5_c_re, s5_c_im, s5_d, s5_w_glu, s5_b_glu, w_out, g_cross, g_mem, w_cq, w_ckv, w_co, g_ffn2, w_ffn2_gu, w_ffn2_down, g_final):
    bp, tp, _ = x_prompt.shape
    bs, ts, _ = x_sample.shape
    l = 0
    row = lambda a: a.reshape(1, -1).astype(F32)
    bf = lambda a: a.astype(BF16)

    are, aim, bbre, bbim = _s5_disc_call(
        s5_lambda_re[l], s5_lambda_im[l], s5_log_dt[l].reshape(S5_GROUPS, 1),
        jnp.transpose(s5_b_re[l], (0, 2, 1)), jnp.transpose(s5_b_im[l], (0, 2, 1)))
    c_re_t = jnp.transpose(s5_c_re[l], (0, 2, 1))
    c_im_t = jnp.transpose(s5_c_im[l], (0, 2, 1))
    cre_h = _halves_block_diag(c_re_t)
    cim_h = _halves_block_diag(-c_im_t)
    w_in_l = w_in[l]
    zg_lo = 2 * GLA_QK_WIDTH + GLA_WIDTH
    zg_hi = zg_lo + GLA_GATE_RANK
    w = {
        'g_ffn1': row(g_ffn1[l]), 'w_ffn1_gu': bf(w_ffn1_gu[l]), 'w_ffn1_down': bf(w_ffn1_down[l]),
        'g_mix': row(g_mix[l]),
        'w_in_main': bf(jnp.concatenate([w_in_l[:, :zg_lo], w_in_l[:, zg_hi:]], axis=1)),
        'w_zg': bf(jnp.pad(w_in_l[:, zg_lo:zg_hi], ((0, 0), (0, LANES - GLA_GATE_RANK)))),
        'w_gate': bf(jnp.pad(w_gla_gate[l], ((0, LANES - GLA_GATE_RANK), (0, 0)))),
        'b_gate': row(b_gla_gate[l]),
        'g_gla_out': row(g_gla_out[l]),
        's5_are': are.reshape(1, S5_STATE), 's5_aim': aim.reshape(1, S5_STATE),
        's5_bre': bf(_halves_block_diag(bbre)), 's5_bim': bf(_halves_block_diag(bbim)),
        's5_c': bf(jnp.concatenate([cre_h, cim_h], axis=1)),
        's5_d': row(s5_d[l]), 's5_wglu': bf(_block_diag(s5_w_glu[l])), 's5_bglu': row(s5_b_glu[l]),
        'w_out': bf(w_out[l]), 'g_cross': row(g_cross[l]), 'w_cq': bf(w_cq[l]), 'w_co': bf(w_co[l]),
        'g_ffn2': row(g_ffn2[l]), 'w_ffn2_gu': bf(w_ffn2_gu[l]), 'w_ffn2_down': bf(w_ffn2_down[l]),
        'g_final': row(g_final),
    }

    mk, mv = _memkv_call(mem_prompt.reshape(bp * MEM_LEN, D_MODEL), row(g_mem[l]), bf(w_ckv[l]), 512)
    zeros_state = jnp.zeros((bp, S5_STATE), F32)
    y_p, st_p, re_p, im_p = _trunk_group(
        x_prompt, None, zeros_state, zeros_state, mk.reshape(bp, MEM_LEN, D_MODEL), mv.reshape(bp, MEM_LEN, D_MODEL),
        w, tm=512, gla_tile=256, gla_states=1, s5_t_tile=128, attn_seqs=1, attn_tq=512)

    seqs_per_chunk = GLA_CHUNK_ROWS // ts
    s0_t = jnp.transpose(state_gla[l], (0, 3, 1, 2)).reshape(bs, GLA_DV, GLA_QK_WIDTH)
    y_s, st_s, re_s, im_s = _trunk_group(
        x_sample, s0_t, state_s5_re[l].reshape(bs, S5_STATE), state_s5_im[l].reshape(bs, S5_STATE),
        cache_mem_k.reshape(DEPTH * bs, MEM_LEN, D_MODEL)[l * bs:(l + 1) * bs],
        cache_mem_v.reshape(DEPTH * bs, MEM_LEN, D_MODEL)[l * bs:(l + 1) * bs],
        w, tm=512, gla_tile=GLA_CHUNK_ROWS, gla_states=seqs_per_chunk, s5_t_tile=ts, attn_seqs=4, attn_tq=ts)

    def gla_state_out(st, nb):
        return jnp.transpose(st.reshape(nb, GLA_DV, GLA_HEADS, GLA_DK), (0, 2, 3, 1))[None]

    s5_out = lambda a, nb: a.reshape(1, nb, S5_GROUPS, S5_N)
    kv_out = lambda a: a.reshape(1, bp, MEM_LEN, CROSS_HEADS, CROSS_DH)
    return (y_p, y_s,
            gla_state_out(st_p, bp), s5_out(re_p, bp), s5_out(im_p, bp), kv_out(mk), kv_out(mv),
            gla_state_out(st_s, bs), s5_out(re_s, bs), s5_out(im_s, bs))
```

```python
import functools
import math

import jax
import jax.numpy as jnp
from jax import lax
from jax.experimental import pallas as pl
from jax.experimental.pallas import tpu as pltpu

F32 = jnp.float32
BF16 = jnp.bfloat16

D_MODEL = 1024
DEPTH = 1
EPS = 1e-6
D_FF = 2816
GLA_WIDTH = 512
GLA_HEADS = 4
GLA_DV = 128
GLA_DK = 64
GLA_QK_WIDTH = 256
GLA_GATE_RANK = 16
GLA_TAU = 16.0
S5_WIDTH = 512
S5_P = 16
S5_GROUPS = 32
S5_N = 64
S5_STATE = S5_GROUPS * S5_N
CROSS_HEADS = 4
CROSS_DH = 256
MEM_LEN = 256

LANES = 128
SUBLANES = 8
MXU_DIM = 256
VMEM_LIMIT_BYTES = 56 * 1024 * 1024

FF_CHUNK = MXU_DIM
GLA_CHUNK_ROWS = 64
S5_LANE_CHUNK = 512


def _cparams(n_grid_axes):
    return pltpu.CompilerParams(
        dimension_semantics=("arbitrary",) * n_grid_axes,
        vmem_limit_bytes=VMEM_LIMIT_BYTES,
    )


def _const_spec(shape):
    zeros = (0,) * len(shape)
    return pl.BlockSpec(shape, lambda *_: zeros, pipeline_mode=pl.Buffered(1))


def _dot(a, b):
    return jnp.dot(a, b, preferred_element_type=F32)


def _dot_nt(a, b):
    return lax.dot_general(a, b, (((1,), (1,)), ((), ())), preferred_element_type=F32)


def _dot_tn(a, b):
    return lax.dot_general(a, b, (((0,), (0,)), ((), ())), preferred_element_type=F32)


def _rms(x, g):
    return x * lax.rsqrt(jnp.mean(x * x, axis=-1, keepdims=True) + EPS) * g


def _swiglu(xn_bf, wgu_ref, wdn_ref):
    acc = None
    for c in range(D_FF // FF_CHUNK):
        lo, hi = c * FF_CHUNK, (c + 1) * FF_CHUNK
        g = _dot(xn_bf, wgu_ref[:, lo:hi])
        u = _dot(xn_bf, wgu_ref[:, D_FF + lo:D_FF + hi])
        a = (jax.nn.silu(g) * u).astype(BF16)
        d = _dot(a, wdn_ref[lo:hi, :])
        acc = d if acc is None else acc + d
    return acc


def _log_sigmoid(z):
    return jnp.minimum(z, 0.0) - jnp.log1p(jnp.exp(-jnp.abs(z)))


def _pre_kernel(x_ref, g1_ref, wgu_ref, wdn_ref, gm_ref, win_ref, wzg_ref, wgate_ref, bgate_ref,
                h_ref, qk_ref, v_ref, la_ref, r_ref, us5_ref):
    x = x_ref[...]
    xn = _rms(x, g1_ref[...]).astype(BF16)
    h = x + 0.5 * _swiglu(xn, wgu_ref, wdn_ref)
    h_ref[...] = h
    un = _rms(h, gm_ref[...]).astype(BF16)
    p = _dot(un, win_ref[...])
    qk_ref[...] = p[:, 0:512]
    v_ref[...] = p[:, 512:1024]
    r_ref[...] = p[:, 1024:1536]
    us5_ref[...] = p[:, 1536:2048]
    zg = _dot(un, wzg_ref[...])
    z = _dot(zg.astype(BF16), wgate_ref[...]) + bgate_ref[...]
    la_ref[...] = _log_sigmoid(z) * (1.0 / GLA_TAU)


def _pre_call(x2d, g1, wgu, wdn, gm, win, wzg, wgate, bgate, tm):
    n = x2d.shape[0]
    row = lambda w: pl.BlockSpec((tm, w), lambda i: (i, 0))
    out_shapes = [jax.ShapeDtypeStruct((n, w), F32) for w in (D_MODEL, 512, 512, 256, 512, 512)]
    return pl.pallas_call(
        _pre_kernel,
        grid=(n // tm,),
        in_specs=[row(D_MODEL), _const_spec(g1.shape), _const_spec(wgu.shape), _const_spec(wdn.shape),
                  _const_spec(gm.shape), _const_spec(win.shape), _const_spec(wzg.shape),
                  _const_spec(wgate.shape), _const_spec(bgate.shape)],
        out_specs=[row(D_MODEL), row(512), row(512), row(256), row(512), row(512)],
        out_shape=out_shapes,
        compiler_params=_cparams(1),
        name="pre_ffn_proj",
    )(x2d, g1, wgu, wdn, gm, win, wzg, wgate, bgate)


def _memkv_kernel(m_ref, g_ref, w_ref, k_ref, v_ref, k4_ref, v4_ref):
    mn = _rms(m_ref[...], g_ref[...]).astype(BF16)
    kv = _dot(mn, w_ref[...])
    k_ref[...] = kv[:, :D_MODEL]
    v_ref[...] = kv[:, D_MODEL:]
    for hd in range(CROSS_HEADS):
        k4_ref[:, hd, :] = kv[:, hd * CROSS_DH:(hd + 1) * CROSS_DH]
        v4_ref[:, hd, :] = kv[:, D_MODEL + hd * CROSS_DH:D_MODEL + (hd + 1) * CROSS_DH]


def _memkv_call(mem2d, g, w, tm):
    n = mem2d.shape[0]
    row = pl.BlockSpec((tm, D_MODEL), lambda i: (i, 0))
    row4 = pl.BlockSpec((tm, CROSS_HEADS, CROSS_DH), lambda i: (i, 0, 0))
    return pl.pallas_call(
        _memkv_kernel,
        grid=(n // tm,),
        in_specs=[row, _const_spec(g.shape), _const_spec(w.shape)],
        out_specs=[row, row, row4, row4],
        out_shape=[jax.ShapeDtypeStruct((n, D_MODEL), F32)] * 2
        + [jax.ShapeDtypeStruct((n, CROSS_HEADS, CROSS_DH), F32)] * 2,
        compiler_params=_cparams(1),
        name="memory_kv",
    )(mem2d, g, w)


def _mid_kernel(h_ref, og_ref, os_ref, wo_ref, gc_ref, wq_ref, h2_ref, q_ref):
    h2 = (h_ref[...] + _dot(og_ref[...].astype(BF16), wo_ref[0:GLA_WIDTH, :])
          + _dot(os_ref[...].astype(BF16), wo_ref[GLA_WIDTH:, :]))
    h2_ref[...] = h2
    hn = _rms(h2, gc_ref[...]).astype(BF16)
    q_ref[...] = _dot(hn, wq_ref[...])


def _mid_call(h, og, os_, wo, gc, wq, tm):
    n = h.shape[0]
    row = lambda w: pl.BlockSpec((tm, w), lambda i: (i, 0))
    return pl.pallas_call(
        _mid_kernel,
        grid=(n // tm,),
        in_specs=[row(D_MODEL), row(512), row(512), _const_spec(wo.shape), _const_spec(gc.shape),
                  _const_spec(wq.shape)],
        out_specs=[row(D_MODEL), row(D_MODEL)],
        out_shape=[jax.ShapeDtypeStruct((n, D_MODEL), F32)] * 2,
        compiler_params=_cparams(1),
        name="mix_out_cross_q",
    )(h, og, os_, wo, gc, wq)


def _post_kernel(h2_ref, o_ref, wco_ref, g2_ref, wgu_ref, wdn_ref, gf_ref, y_ref):
    h3 = h2_ref[...] + _dot(o_ref[...].astype(BF16), wco_ref[...])
    hn = _rms(h3, g2_ref[...]).astype(BF16)
    h4 = h3 + 0.5 * _swiglu(hn, wgu_ref, wdn_ref)
    y_ref[...] = _rms(h4, gf_ref[...])


def _post_call(h2, o, wco, g2, wgu, wdn, gf, tm):
    n = h2.shape[0]
    row = pl.BlockSpec((tm, D_MODEL), lambda i: (i, 0))
    return pl.pallas_call(
        _post_kernel,
        grid=(n // tm,),
        in_specs=[row, row, _const_spec(wco.shape), _const_spec(g2.shape), _const_spec(wgu.shape),
                  _const_spec(wdn.shape), _const_spec(gf.shape)],
        out_specs=row,
        out_shape=jax.ShapeDtypeStruct((n, D_MODEL), F32),
        compiler_params=_cparams(1),
        name="post_ffn_final",
    )(h2, o, wco, g2, wgu, wdn, gf)


def _softmax(s):
    e = jnp.exp(s - jnp.max(s, axis=-1, keepdims=True))
    return e / jnp.sum(e, axis=-1, keepdims=True)


def _load_mem(ref, j):
    if len(ref.shape) == 3:
        return ref[j].astype(BF16)
    return jnp.concatenate([ref[j, :, hd, :] for hd in range(CROSS_HEADS)], axis=-1).astype(BF16)


def _attn_packed_kernel(q_ref, k_ref, v_ref, o_ref, *, n_seq):
    t = q_ref.shape[1]
    shape = (CROSS_HEADS * t, D_MODEL)
    q_mask = (_iota2(shape, 0) // t) == (_iota2(shape, 1) // CROSS_DH)
    o_lane_head = _iota2((t, D_MODEL), 1) // CROSS_DH
    scores = []
    for j in range(n_seq):
        q_stack = jnp.where(q_mask, jnp.concatenate([q_ref[j]] * CROSS_HEADS, axis=0), 0.0).astype(BF16)
        scores.append(_dot_nt(q_stack, _load_mem(k_ref, j)) * (CROSS_DH ** -0.5))
    probs = [_softmax(s).astype(BF16) for s in scores]
    for j in range(n_seq):
        full = _dot(probs[j], _load_mem(v_ref, j))
        o = jnp.zeros((t, D_MODEL), F32)
        for hd in range(CROSS_HEADS):
            o = jnp.where(o_lane_head == hd, full[hd * t:(hd + 1) * t, :], o)
        o_ref[j] = o.astype(o_ref.dtype)


def _attn_kernel(q_ref, k_ref, v_ref, o_ref, *, n_seq):
    cast = lambda a: a.astype(BF16)
    for j in range(n_seq):
        q = q_ref[j]
        outs = []
        for hd in range(CROSS_HEADS):
            sl = slice(hd * CROSS_DH, (hd + 1) * CROSS_DH)
            s = _dot_nt(cast(q[:, sl]), cast(k_ref[j, :, sl])) * (CROSS_DH ** -0.5)
            p = _softmax(s)
            outs.append(_dot(cast(p), cast(v_ref[j, :, sl])))
        o_ref[j] = jnp.concatenate(outs, axis=-1).astype(o_ref.dtype)


def _attn_call(q3, k3, v3, n_seq, tq):
    s, t, _ = q3.shape
    qspec = pl.BlockSpec((n_seq, tq, D_MODEL), lambda i, j: (i, j, 0))
    if k3.ndim == 3:
        kvspec = pl.BlockSpec((n_seq, MEM_LEN, D_MODEL), lambda i, j: (i, 0, 0))
    else:
        kvspec = pl.BlockSpec((n_seq, MEM_LEN, CROSS_HEADS, CROSS_DH), lambda i, j: (i, 0, 0, 0))
    return pl.pallas_call(
        functools.partial(_attn_packed_kernel if tq * CROSS_HEADS <= MXU_DIM else _attn_kernel, n_seq=n_seq),
        grid=(s // n_seq, t // tq),
        in_specs=[qspec, kvspec, kvspec],
        out_specs=qspec,
        out_shape=jax.ShapeDtypeStruct(q3.shape, BF16 if tq % 16 == 0 else F32),
        compiler_params=_cparams(2),
        name="cross_attention",
    )(q3, k3, v3)


def _split3(x):
    hi = x.astype(BF16)
    r1 = x - hi.astype(F32)
    mid = r1.astype(BF16)
    lo = (r1 - mid.astype(F32)).astype(BF16)
    return hi, mid, lo


def _dot01(m01, x):
    hi, mid, lo = _split3(x)
    return _dot(m01, hi) + _dot(m01, mid) + _dot(m01, lo)


def _iota2(shape, dim):
    return lax.broadcasted_iota(jnp.int32, shape, dim)


def _gla_kernel(*refs, seq_len, n_chunks, n_state, has_s0):
    if has_s0:
        qk_ref, v_ref, la_ref, r_ref, g_ref, s0_ref, o_ref, st_ref, s_scr = refs
    else:
        qk_ref, v_ref, la_ref, r_ref, g_ref, o_ref, st_ref, s_scr = refs
        s0_ref = None
    c_rows = GLA_CHUNK_ROWS
    n_sub = c_rows // seq_len
    t_idx = pl.program_id(1)

    @pl.when(t_idx == 0)
    def _():
        if has_s0:
            s_scr[...] = s0_ref[...]
        else:
            s_scr[...] = jnp.zeros_like(s_scr)

    log2 = lambda n: int(math.log2(n))
    ri = _iota2((c_rows, c_rows), 0)
    ci = _iota2((c_rows, c_rows), 1)
    tri = ((ri >> log2(seq_len)) == (ci >> log2(seq_len))) & (ci <= ri)
    tri = jnp.where(tri, 1.0, 0.0).astype(BF16)
    ai = _iota2((c_rows, GLA_HEADS * c_rows), 0)
    asx = _iota2((c_rows, GLA_HEADS * c_rows), 1) & (c_rows - 1)
    rowq = _iota2((c_rows, GLA_QK_WIDTH), 0)
    kr = _iota2((GLA_HEADS * c_rows, GLA_QK_WIDTH), 0)
    kl = _iota2((GLA_HEADS * c_rows, GLA_QK_WIDTH), 1)
    k_head_mask = (kr >> log2(c_rows)) == (kl >> log2(GLA_DK))
    vr = _iota2((GLA_HEADS * c_rows, GLA_WIDTH), 0)
    vl = _iota2((GLA_HEADS * c_rows, GLA_WIDTH), 1)
    v_head_mask = (vr >> log2(c_rows)) == (vl >> log2(GLA_DV))
    sr = _iota2((GLA_HEADS * GLA_DV, GLA_QK_WIDTH), 0)
    slane = _iota2((GLA_HEADS * GLA_DV, GLA_QK_WIDTH), 1)
    s_head_mask = (sr >> log2(GLA_DV)) == (slane >> log2(GLA_DK))
    u_lane_head = _iota2((GLA_DV, GLA_QK_WIDTH), 1) >> log2(GLA_DK)

    def head_blocks(x, mask):
        return jnp.where(mask, jnp.concatenate([x] * GLA_HEADS, axis=0), 0.0)

    g_out = g_ref[...]
    for c in range(n_chunks):
        rows = slice(c * c_rows, (c + 1) * c_rows)
        q = qk_ref[0, rows, 0:GLA_QK_WIDTH] * (GLA_DK ** -0.5)
        k = qk_ref[0, rows, GLA_QK_WIDTH:2 * GLA_QK_WIDTH]
        v = v_ref[0, rows, :]
        la = la_ref[0, rows, :]
        b = _dot01(tri, la)

        a_mat = jnp.where(ai == asx, _dot_nt(q.astype(BF16), head_blocks(k, k_head_mask).astype(BF16)), 0.0)
        last = b
        m = 1
        while m < seq_len:
            prev_last = pltpu.roll(last, m, axis=0)
            eq = jnp.minimum(b - prev_last, 0.0)
            ek = jnp.minimum(last - b, 0.0)
            qm = (q * jnp.exp(eq)).astype(BF16)
            km = head_blocks(k * jnp.exp(ek), k_head_mask).astype(BF16)
            a_lvl = _dot_nt(qm, km)
            iblk = ai >> log2(m)
            lvl_mask = ((iblk - (asx >> log2(m))) == 1) & ((iblk & 1) == 1)
            a_mat = jnp.where(lvl_mask, a_lvl, a_mat)
            even = ((rowq >> log2(m)) & 1) == 0
            last = jnp.where(even, pltpu.roll(last, c_rows - m, axis=0), last)
            m *= 2
        o_intra = _dot(a_mat.astype(BF16), head_blocks(v, v_head_mask).astype(BF16))

        q_dec = q * jnp.exp(b)
        k_dec = k * jnp.exp(jnp.minimum(last - b, 0.0))
        o_parts = []
        for j in range(n_sub):
            sl = slice(j * seq_len, (j + 1) * seq_len)
            s_t = s_scr[j]
            s_blocks = head_blocks(s_t, s_head_mask)
            o_parts.append(_dot_nt(q_dec[sl], s_blocks))
            u_full = _dot_tn(v[sl], k_dec[sl])
            u_t = jnp.zeros((GLA_DV, GLA_QK_WIDTH), F32)
            for hd in range(GLA_HEADS):
                u_t = jnp.where(u_lane_head == hd, u_full[hd * GLA_DV:(hd + 1) * GLA_DV, :], u_t)
            decay = jnp.exp(last[j * seq_len:j * seq_len + 1, :])
            s_scr[j] = s_t * decay + u_t
        o_inter = o_parts[0] if n_sub == 1 else jnp.concatenate(o_parts, axis=0)
        o = o_intra + o_inter

        r = r_ref[0, rows, :]
        outs = []
        for hd in range(GLA_HEADS):
            sl = slice(hd * GLA_DV, (hd + 1) * GLA_DV)
            outs.append(_rms(o[:, sl], g_out) * jax.nn.silu(r[:, sl]))
        o_ref[0, rows, :] = jnp.concatenate(outs, axis=-1).astype(o_ref.dtype)

    @pl.when(t_idx == pl.num_programs(1) - 1)
    def _():
        st_ref[...] = s_scr[...]


def _gla_call(qk3, v3, la3, r3, g_out, s0_t, seq_len, tile_rows, n_state):
    n_outer, rows, _ = qk3.shape
    n_chunks = tile_rows // GLA_CHUNK_ROWS
    spec = lambda w: pl.BlockSpec((1, tile_rows, w), lambda i, t: (i, t, 0))
    st_spec = pl.BlockSpec((n_state, GLA_DV, GLA_QK_WIDTH), lambda i, t: (i, 0, 0))
    has_s0 = s0_t is not None
    in_specs = [spec(512), spec(512), spec(256), spec(512), _const_spec(g_out.shape)]
    args = [qk3, v3, la3, r3, g_out]
    if has_s0:
        in_specs.append(st_spec)
        args.append(s0_t)
    return pl.pallas_call(
        functools.partial(_gla_kernel, seq_len=seq_len, n_chunks=n_chunks, n_state=n_state, has_s0=has_s0),
        grid=(n_outer, rows // tile_rows),
        in_specs=in_specs,
        out_specs=[spec(512), st_spec],
        out_shape=[jax.ShapeDtypeStruct((n_outer, rows, GLA_WIDTH), BF16),
                   jax.ShapeDtypeStruct((n_outer * n_state, GLA_DV, GLA_QK_WIDTH), F32)],
        scratch_shapes=[pltpu.VMEM((n_state, GLA_DV, GLA_QK_WIDTH), F32)],
        compiler_params=_cparams(2),
        name="gla_mixer",
    )(*args)


def _s5_disc_kernel(lre_ref, lim_ref, ldt_ref, bre_ref, bim_ref, are_ref, aim_ref, bbre_ref, bbim_ref):
    lr = lre_ref[...]
    li = lim_ref[...]
    dt = jnp.exp(ldt_ref[...])
    mag = jnp.exp(dt * lr)
    ab_re = mag * jnp.cos(dt * li)
    ab_im = mag * jnp.sin(dt * li)
    den = lr * lr + li * li
    coef_re = ((ab_re - 1.0) * lr + ab_im * li) / den
    coef_im = (ab_im * lr - (ab_re - 1.0) * li) / den
    are_ref[...] = ab_re
    aim_ref[...] = ab_im
    br = bre_ref[...]
    bi = bim_ref[...]
    cr = coef_re[:, None, :]
    cim = coef_im[:, None, :]
    bbre_ref[...] = cr * br - cim * bi
    bbim_ref[...] = cr * bi + cim * br


def _s5_disc_call(lre, lim, ldt, bre_t, bim_t):
    gn = jax.ShapeDtypeStruct((S5_GROUPS, S5_N), F32)
    gpn = jax.ShapeDtypeStruct((S5_GROUPS, S5_P, S5_N), F32)
    return pl.pallas_call(
        _s5_disc_kernel,
        out_shape=[gn, gn, gpn, gpn],
        name="s5_discretise",
    )(lre, lim, ldt, bre_t, bim_t)


def _to_time_major(u_ref, perm_scr, n_seq, t_tile):
    if n_seq == SUBLANES:
        n_slab = S5_WIDTH // LANES
        for j in range(n_seq):
            for sb in range(n_slab):
                perm_scr[sb, pl.ds(j, t_tile, stride=n_seq), :] = u_ref[j, :, sb * LANES:(sb + 1) * LANES]
        return jnp.concatenate([perm_scr[sb] for sb in range(n_slab)], axis=-1)
    return jnp.concatenate([u_ref[:, t, :] for t in range(t_tile)], axis=0)


def _from_time_major(y, o_ref, perm_scr, n_seq, t_tile):
    if n_seq == SUBLANES:
        n_slab = S5_WIDTH // LANES
        for sb in range(n_slab):
            perm_scr[sb] = y[:, sb * LANES:(sb + 1) * LANES]
        for j in range(n_seq):
            for sb in range(n_slab):
                o_ref[j, :, sb * LANES:(sb + 1) * LANES] = perm_scr[sb, pl.ds(j, t_tile, stride=n_seq), :]
    else:
        for t in range(t_tile):
            o_ref[:, t, :] = y[t * n_seq:(t + 1) * n_seq, :]


def _s5_kernel(u_ref, x0re_ref, x0im_ref, are_ref, aim_ref, bre_ref, bim_ref, c_ref, d_ref, wglu_ref,
               bglu_ref, o_ref, xre_ref, xim_ref, sre_scr, sim_scr, xre_scr, xim_scr, perm_scr, *, n_seq, t_tile):
    half_w = S5_WIDTH // 2
    half_s = S5_STATE // 2
    rows = t_tile * n_seq

    @pl.when(pl.program_id(0) == 0)
    def _():
        xre_scr[...] = x0re_ref[...]
        xim_scr[...] = x0im_ref[...]

    u = _to_time_major(u_ref, perm_scr, n_seq, t_tile)
    ub = u.astype(BF16)
    for hf in range(2):
        uh = ub[:, hf * half_w:(hf + 1) * half_w]
        sre_scr[:, :, hf * half_s:(hf + 1) * half_s] = _dot(uh, bre_ref[hf]).reshape(t_tile, n_seq, half_s)
        sim_scr[:, :, hf * half_s:(hf + 1) * half_s] = _dot(uh, bim_ref[hf]).reshape(t_tile, n_seq, half_s)

    for lc in range(S5_STATE // S5_LANE_CHUNK):
        lanes = pl.ds(lc * S5_LANE_CHUNK, S5_LANE_CHUNK)
        ar = jnp.broadcast_to(are_ref[:, lanes], (SUBLANES, S5_LANE_CHUNK))
        ai = jnp.broadcast_to(aim_ref[:, lanes], (SUBLANES, S5_LANE_CHUNK))

        def seq_body(sc, carry, lanes=lanes, ar=ar, ai=ai):
            srows = pl.ds(pl.multiple_of(sc * SUBLANES, SUBLANES), SUBLANES)

            def t_body(t, x):
                xr, xi = x
                nxr = ar * xr - ai * xi + sre_scr[t, srows, lanes]
                nxi = ar * xi + ai * xr + sim_scr[t, srows, lanes]
                sre_scr[t, srows, lanes] = nxr
                sim_scr[t, srows, lanes] = nxi
                return nxr, nxi

            xr, xi = lax.fori_loop(0, t_tile, t_body, (xre_scr[srows, lanes], xim_scr[srows, lanes]))
            xre_scr[srows, lanes] = xr
            xim_scr[srows, lanes] = xi
            return carry

        lax.fori_loop(0, n_seq // SUBLANES, seq_body, 0)

    xr_all = sre_scr[...].reshape(rows, S5_STATE).astype(BF16)
    xi_all = sim_scr[...].reshape(rows, S5_STATE).astype(BF16)
    ys = []
    for hf in range(2):
        st = slice(hf * half_s, (hf + 1) * half_s)
        ys.append(_dot(xr_all[:, st], c_ref[hf, 0:half_s, :]) + _dot(xi_all[:, st], c_ref[hf, half_s:, :]))
    y = jnp.concatenate(ys, axis=-1) + d_ref[...] * u
    z = jax.nn.gelu(y)
    gate = jax.nn.sigmoid(_dot(z.astype(BF16), wglu_ref[...]) + bglu_ref[...])
    _from_time_major(z * gate, o_ref, perm_scr, n_seq, t_tile)

    @pl.when(pl.program_id(0) == pl.num_programs(0) - 1)
    def _():
        xre_ref[...] = xre_scr[...]
        xim_ref[...] = xim_scr[...]


def _s5_call(u3, x0re, x0im, are, aim, bre, bim, cmat, dskip, wglu, bglu, t_tile):
    n_seq, t, _ = u3.shape
    uspec = pl.BlockSpec((n_seq, t_tile, S5_WIDTH), lambda i: (0, i, 0))
    perm_rows = t_tile * n_seq if n_seq == SUBLANES else SUBLANES
    xspec = pl.BlockSpec((n_seq, S5_STATE), lambda i: (0, 0))
    consts = [are, aim, bre, bim, cmat, dskip, wglu, bglu]
    return pl.pallas_call(
        functools.partial(_s5_kernel, n_seq=n_seq, t_tile=t_tile),
        grid=(t // t_tile,),
        in_specs=[uspec, xspec, xspec] + [_const_spec(a.shape) for a in consts],
        out_specs=[uspec, xspec, xspec],
        out_shape=[jax.ShapeDtypeStruct((n_seq, t, S5_WIDTH), F32),
                   jax.ShapeDtypeStruct((n_seq, S5_STATE), F32),
                   jax.ShapeDtypeStruct((n_seq, S5_STATE), F32)],
        scratch_shapes=[pltpu.VMEM((t_tile, n_seq, S5_STATE), F32), pltpu.VMEM((t_tile, n_seq, S5_STATE), F32),
                        pltpu.VMEM((n_seq, S5_STATE), F32), pltpu.VMEM((n_seq, S5_STATE), F32),
                        pltpu.VMEM((S5_WIDTH // LANES, perm_rows, LANES), F32)],
        compiler_params=_cparams(1),
        name="s5_mixer",
    )(u3, x0re, x0im, *consts)


def _block_diag(blocks):
    g, a, b = blocks.shape
    eye = jnp.eye(g, dtype=blocks.dtype)
    return (eye[:, None, :, None] * blocks[:, :, None, :]).reshape(g * a, g * b)


def _halves_block_diag(blocks):
    g = blocks.shape[0]
    return jnp.stack([_block_diag(blocks[:g // 2]), _block_diag(blocks[g // 2:])])


def _trunk_group(x3, s_gla0, x0re, x0im, mem_k3, mem_v3, w, *, tm, gla_tile, gla_states, s5_t_tile,
                 attn_seqs, attn_tq):
    s, t, _ = x3.shape
    n = s * t
    h, qk, v, la, r, us5 = _pre_call(x3.reshape(n, D_MODEL), w['g_ffn1'], w['w_ffn1_gu'], w['w_ffn1_down'],
                                     w['g_mix'], w['w_in_main'], w['w_zg'], w['w_gate'], w['b_gate'], tm)
    if gla_states == 1:
        shp = lambda a: a.reshape(s, t, a.shape[-1])
    else:
        rows = gla_states * t
        shp = lambda a: a.reshape(n // rows, rows, a.shape[-1])
    o_gla, st_new = _gla_call(shp(qk), shp(v), shp(la), shp(r), w['g_gla_out'], s_gla0, t if gla_states > 1 else GLA_CHUNK_ROWS,
                              gla_tile, gla_states)
    o_s5, xre, xim = _s5_call(us5.reshape(s, t, S5_WIDTH), x0re, x0im, w['s5_are'], w['s5_aim'], w['s5_bre'],
                              w['s5_bim'], w['s5_c'], w['s5_d'], w['s5_wglu'], w['s5_bglu'], s5_t_tile)
    o_s5 = o_s5.reshape(n, S5_WIDTH)
    h2, q = _mid_call(h, o_gla.reshape(n, GLA_WIDTH), o_s5, w['w_out'], w['g_cross'], w['w_cq'], tm)
    o_att = _attn_call(q.reshape(s, t, D_MODEL), mem_k3, mem_v3, attn_seqs, attn_tq)
    y = _post_call(h2, o_att.reshape(n, D_MODEL), w['w_co'], w['g_ffn2'], w['w_ffn2_gu'], w['w_ffn2_down'],
                   w['g_final'], tm)
    return y.reshape(s, t, D_MODEL), st_new, xre, xim


def kernel(x_prompt, x_sample, state_gla, state_s5_re, state_s5_im, cache_mem_k, cache_mem_v, mem_prompt, g_ffn1, w_ffn1_gu, w_ffn1_down, g_mix, w_in, w_gla_gate, b_gla_gate, g_gla_out, s5_lambda_re, s5_lambda_im, s5_log_dt, s5_b_re, s5_b_im, s5_c_re, s5_c_im, s5_d, s5_w_glu, s5_b_glu, w_out, g_cross, g_mem, w_cq, w_ckv, w_co, g_ffn2, w_ffn2_gu, w_ffn2_down, g_final):
    bp, tp, _ = x_prompt.shape
    bs, ts, _ = x_sample.shape
    l = 0
    row = lambda a: a.reshape(1, -1).astype(F32)
    bf = lambda a: a.astype(BF16)

    are, aim, bbre, bbim = _s5_disc_call(
        s5_lambda_re[l], s5_lambda_im[l], s5_log_dt[l].reshape(S5_GROUPS, 1),
        jnp.transpose(s5_b_re[l], (0, 2, 1)), jnp.transpose(s5_b_im[l], (0, 2, 1)))
    c_re_t = jnp.transpose(s5_c_re[l], (0, 2, 1))
    c_im_t = jnp.transpose(s5_c_im[l], (0, 2, 1))
    cre_h = _halves_block_diag(c_re_t)
    cim_h = _halves_block_diag(-c_im_t)
    w_in_l = w_in[l]
    zg_lo = 2 * GLA_QK_WIDTH + GLA_WIDTH
    zg_hi = zg_lo + GLA_GATE_RANK
    w = {
        'g_ffn1': row(g_ffn1[l]), 'w_ffn1_gu': bf(w_ffn1_gu[l]), 'w_ffn1_down': bf(w_ffn1_down[l]),
        'g_mix': row(g_mix[l]),
        'w_in_main': bf(jnp.concatenate([w_in_l[:, :zg_lo], w_in_l[:, zg_hi:]], axis=1)),
        'w_zg': bf(jnp.pad(w_in_l[:, zg_lo:zg_hi], ((0, 0), (0, LANES - GLA_GATE_RANK)))),
        'w_gate': bf(jnp.pad(w_gla_gate[l], ((0, LANES - GLA_GATE_RANK), (0, 0)))),
        'b_gate': row(b_gla_gate[l]),
        'g_gla_out': row(g_gla_out[l]),
        's5_are': are.reshape(1, S5_STATE), 's5_aim': aim.reshape(1, S5_STATE),
        's5_bre': bf(_halves_block_diag(bbre)), 's5_bim': bf(_halves_block_diag(bbim)),
        's5_c': bf(jnp.concatenate([cre_h, cim_h], axis=1)),
        's5_d': row(s5_d[l]), 's5_wglu': bf(_block_diag(s5_w_glu[l])), 's5_bglu': row(s5_b_glu[l]),
        'w_out': bf(w_out[l]), 'g_cross': row(g_cross[l]), 'w_cq': bf(w_cq[l]), 'w_co': bf(w_co[l]),
        'g_ffn2': row(g_ffn2[l]), 'w_ffn2_gu': bf(w_ffn2_gu[l]), 'w_ffn2_down': bf(w_ffn2_down[l]),
        'g_final': row(g_final),
    }

    mk, mv, mk4, mv4 = _memkv_call(mem_prompt.reshape(bp * MEM_LEN, D_MODEL), row(g_mem[l]), bf(w_ckv[l]), 512)
    zeros_state = jnp.zeros((bp, S5_STATE), F32)
    y_p, st_p, re_p, im_p = _trunk_group(
        x_prompt, None, zeros_state, zeros_state, mk.reshape(bp, MEM_LEN, D_MODEL), mv.reshape(bp, MEM_LEN, D_MODEL),
        w, tm=512, gla_tile=256, gla_states=1, s5_t_tile=128, attn_seqs=1, attn_tq=512)

    seqs_per_chunk = GLA_CHUNK_ROWS // ts
    s0_t = jnp.transpose(state_gla[l], (0, 3, 1, 2)).reshape(bs, GLA_DV, GLA_QK_WIDTH)
    y_s, st_s, re_s, im_s = _trunk_group(
        x_sample, s0_t, state_s5_re[l].reshape(bs, S5_STATE), state_s5_im[l].reshape(bs, S5_STATE),
        cache_mem_k.reshape(DEPTH * bs, MEM_LEN, CROSS_HEADS, CROSS_DH)[l * bs:(l + 1) * bs],
        cache_mem_v.reshape(DEPTH * bs, MEM_LEN, CROSS_HEADS, CROSS_DH)[l * bs:(l + 1) * bs],
        w, tm=512, gla_tile=GLA_CHUNK_ROWS, gla_states=seqs_per_chunk, s5_t_tile=ts, attn_seqs=4, attn_tq=ts)

    def gla_state_out(st, nb):
        return jnp.transpose(st.reshape(nb, GLA_DV, GLA_HEADS, GLA_DK), (0, 2, 3, 1))[None]

    s5_out = lambda a, nb: a.reshape(1, nb, S5_GROUPS, S5_N)
    kv_out = lambda a: a.reshape(1, bp, MEM_LEN, CROSS_HEADS, CROSS_DH)
    return (y_p, y_s,
            gla_state_out(st_p, bp), s5_out(re_p, bp), s5_out(im_p, bp), kv_out(mk4), kv_out(mv4),
            gla_state_out(st_s, bs), s5_out(re_s, bs), s5_out(im_s, bs))
```

```python
import functools
import math

import jax
import jax.numpy as jnp
from jax import lax
from jax.experimental import pallas as pl
from jax.experimental.pallas import tpu as pltpu

F32 = jnp.float32
BF16 = jnp.bfloat16

D_MODEL = 1024
DEPTH = 1
EPS = 1e-6
D_FF = 2816
GLA_WIDTH = 512
GLA_HEADS = 4
GLA_DV = 128
GLA_DK = 64
GLA_QK_WIDTH = 256
GLA_GATE_RANK = 16
GLA_TAU = 16.0
S5_WIDTH = 512
S5_P = 16
S5_GROUPS = 32
S5_N = 64
S5_STATE = S5_GROUPS * S5_N
CROSS_HEADS = 4
CROSS_DH = 256
MEM_LEN = 256

LANES = 128
SUBLANES = 8
MXU_DIM = 256
VMEM_LIMIT_BYTES = 56 * 1024 * 1024

FF_CHUNK = MXU_DIM
GLA_CHUNK_ROWS = 64
S5_LANE_CHUNK = 512


def _cparams(n_grid_axes):
    return pltpu.CompilerParams(
        dimension_semantics=("arbitrary",) * n_grid_axes,
        vmem_limit_bytes=VMEM_LIMIT_BYTES,
    )


def _const_spec(shape):
    zeros = (0,) * len(shape)
    return pl.BlockSpec(shape, lambda *_: zeros, pipeline_mode=pl.Buffered(1))


def _dot(a, b):
    return jnp.dot(a, b, preferred_element_type=F32)


def _dot_nt(a, b):
    return lax.dot_general(a, b, (((1,), (1,)), ((), ())), preferred_element_type=F32)


def _dot_tn(a, b):
    return lax.dot_general(a, b, (((0,), (0,)), ((), ())), preferred_element_type=F32)


def _rms(x, g):
    return x * lax.rsqrt(jnp.mean(x * x, axis=-1, keepdims=True) + EPS) * g


def _swiglu(xn_bf, wgu_ref, wdn_ref):
    acc = None
    for c in range(D_FF // FF_CHUNK):
        lo, hi = c * FF_CHUNK, (c + 1) * FF_CHUNK
        g = _dot(xn_bf, wgu_ref[:, lo:hi])
        u = _dot(xn_bf, wgu_ref[:, D_FF + lo:D_FF + hi])
        a = (jax.nn.silu(g) * u).astype(BF16)
        d = _dot(a, wdn_ref[lo:hi, :])
        acc = d if acc is None else acc + d
    return acc


def _log_sigmoid(z):
    return jnp.minimum(z, 0.0) - jnp.log1p(jnp.exp(-jnp.abs(z)))


def _pre_kernel(x_ref, g1_ref, wgu_ref, wdn_ref, gm_ref, win_ref, wzg_ref, wgate_ref, bgate_ref,
                h_ref, qk_ref, v_ref, la_ref, r_ref, us5_ref):
    x = x_ref[...]
    xn = _rms(x, g1_ref[...]).astype(BF16)
    h = x + 0.5 * _swiglu(xn, wgu_ref, wdn_ref)
    h_ref[...] = h
    un = _rms(h, gm_ref[...]).astype(BF16)
    p = _dot(un, win_ref[...])
    qk_ref[...] = p[:, 0:512]
    v_ref[...] = p[:, 512:1024]
    r_ref[...] = p[:, 1024:1536]
    us5_ref[...] = p[:, 1536:2048]
    zg = _dot(un, wzg_ref[...])
    z = _dot(zg.astype(BF16), wgate_ref[...]) + bgate_ref[...]
    la_ref[...] = _log_sigmoid(z) * (1.0 / GLA_TAU)


def _pre_call(x2d, g1, wgu, wdn, gm, win, wzg, wgate, bgate, tm):
    n = x2d.shape[0]
    row = lambda w: pl.BlockSpec((tm, w), lambda i: (i, 0))
    out_shapes = [jax.ShapeDtypeStruct((n, w), F32) for w in (D_MODEL, 512, 512, 256, 512, 512)]
    return pl.pallas_call(
        _pre_kernel,
        grid=(n // tm,),
        in_specs=[row(D_MODEL), _const_spec(g1.shape), _const_spec(wgu.shape), _const_spec(wdn.shape),
                  _const_spec(gm.shape), _const_spec(win.shape), _const_spec(wzg.shape),
                  _const_spec(wgate.shape), _const_spec(bgate.shape)],
        out_specs=[row(D_MODEL), row(512), row(512), row(256), row(512), row(512)],
        out_shape=out_shapes,
        compiler_params=_cparams(1),
        name="pre_ffn_proj",
    )(x2d, g1, wgu, wdn, gm, win, wzg, wgate, bgate)


def _memkv_kernel(m_ref, g_ref, w_ref, k_ref, v_ref, k4_ref, v4_ref):
    mn = _rms(m_ref[...], g_ref[...]).astype(BF16)
    kv = _dot(mn, w_ref[...])
    k_ref[...] = kv[:, :D_MODEL]
    v_ref[...] = kv[:, D_MODEL:]
    for hd in range(CROSS_HEADS):
        k4_ref[:, hd, :] = kv[:, hd * CROSS_DH:(hd + 1) * CROSS_DH]
        v4_ref[:, hd, :] = kv[:, D_MODEL + hd * CROSS_DH:D_MODEL + (hd + 1) * CROSS_DH]


def _memkv_call(mem2d, g, w, tm):
    n = mem2d.shape[0]
    row = pl.BlockSpec((tm, D_MODEL), lambda i: (i, 0))
    row4 = pl.BlockSpec((tm, CROSS_HEADS, CROSS_DH), lambda i: (i, 0, 0))
    return pl.pallas_call(
        _memkv_kernel,
        grid=(n // tm,),
        in_specs=[row, _const_spec(g.shape), _const_spec(w.shape)],
        out_specs=[row, row, row4, row4],
        out_shape=[jax.ShapeDtypeStruct((n, D_MODEL), F32)] * 2
        + [jax.ShapeDtypeStruct((n, CROSS_HEADS, CROSS_DH), F32)] * 2,
        compiler_params=_cparams(1),
        name="memory_kv",
    )(mem2d, g, w)


def _mid_kernel(h_ref, og_ref, os_ref, wo_ref, gc_ref, wq_ref, h2_ref, q_ref):
    h2 = (h_ref[...] + _dot(og_ref[...].astype(BF16), wo_ref[0:GLA_WIDTH, :])
          + _dot(os_ref[...].astype(BF16), wo_ref[GLA_WIDTH:, :]))
    h2_ref[...] = h2
    hn = _rms(h2, gc_ref[...]).astype(BF16)
    q_ref[...] = _dot(hn, wq_ref[...])


def _mid_call(h, og, os_, wo, gc, wq, tm):
    n = h.shape[0]
    row = lambda w: pl.BlockSpec((tm, w), lambda i: (i, 0))
    return pl.pallas_call(
        _mid_kernel,
        grid=(n // tm,),
        in_specs=[row(D_MODEL), row(512), row(512), _const_spec(wo.shape), _const_spec(gc.shape),
                  _const_spec(wq.shape)],
        out_specs=[row(D_MODEL), row(D_MODEL)],
        out_shape=[jax.ShapeDtypeStruct((n, D_MODEL), F32)] * 2,
        compiler_params=_cparams(1),
        name="mix_out_cross_q",
    )(h, og, os_, wo, gc, wq)


def _post_kernel(h2_ref, o_ref, wco_ref, g2_ref, wgu_ref, wdn_ref, gf_ref, y_ref):
    h3 = h2_ref[...] + _dot(o_ref[...].astype(BF16), wco_ref[...])
    hn = _rms(h3, g2_ref[...]).astype(BF16)
    h4 = h3 + 0.5 * _swiglu(hn, wgu_ref, wdn_ref)
    y_ref[...] = _rms(h4, gf_ref[...])


def _post_call(h2, o, wco, g2, wgu, wdn, gf, tm):
    n = h2.shape[0]
    row = pl.BlockSpec((tm, D_MODEL), lambda i: (i, 0))
    return pl.pallas_call(
        _post_kernel,
        grid=(n // tm,),
        in_specs=[row, row, _const_spec(wco.shape), _const_spec(g2.shape), _const_spec(wgu.shape),
                  _const_spec(wdn.shape), _const_spec(gf.shape)],
        out_specs=row,
        out_shape=jax.ShapeDtypeStruct((n, D_MODEL), F32),
        compiler_params=_cparams(1),
        name="post_ffn_final",
    )(h2, o, wco, g2, wgu, wdn, gf)


def _tail_kernel(h_ref, og_ref, os_ref, k_ref, v_ref, wo_ref, gc_ref, wq_ref, wco_ref, g2_ref, wgu_ref,
                 wdn_ref, gf_ref, y_ref):
    h2 = (h_ref[0] + _dot(og_ref[0].astype(BF16), wo_ref[0:GLA_WIDTH, :])
          + _dot(os_ref[0].astype(BF16), wo_ref[GLA_WIDTH:, :]))
    q = _dot(_rms(h2, gc_ref[...]).astype(BF16), wq_ref[...]).astype(BF16)
    outs = []
    for hd in range(CROSS_HEADS):
        sl = slice(hd * CROSS_DH, (hd + 1) * CROSS_DH)
        s = _dot_nt(q[:, sl], k_ref[0, :, sl].astype(BF16)) * (CROSS_DH ** -0.5)
        outs.append(_dot(_softmax(s).astype(BF16), v_ref[0, :, sl].astype(BF16)).astype(BF16))
    h3 = h2 + _dot(jnp.concatenate(outs, axis=-1), wco_ref[...])
    hn = _rms(h3, g2_ref[...]).astype(BF16)
    h4 = h3 + 0.5 * _swiglu(hn, wgu_ref, wdn_ref)
    y_ref[0] = _rms(h4, gf_ref[...])


def _tail_call(h3d, og3, os3, k3, v3, w, tm):
    s, t, _ = h3d.shape
    row = lambda wd: pl.BlockSpec((1, tm, wd), lambda i, j: (i, j, 0))
    kvspec = pl.BlockSpec((1, MEM_LEN, D_MODEL), lambda i, j: (i, 0, 0))
    consts = [w['w_out'], w['g_cross'], w['w_cq'], w['w_co'], w['g_ffn2'], w['w_ffn2_gu'], w['w_ffn2_down'],
              w['g_final']]
    return pl.pallas_call(
        _tail_kernel,
        grid=(s, t // tm),
        in_specs=[row(D_MODEL), row(GLA_WIDTH), row(S5_WIDTH), kvspec, kvspec] + [_const_spec(a.shape) for a in consts],
        out_specs=row(D_MODEL),
        out_shape=jax.ShapeDtypeStruct((s, t, D_MODEL), F32),
        compiler_params=_cparams(2),
        name="tail_attn_ffn",
    )(h3d, og3, os3, k3, v3, *consts)


def _softmax(s):
    e = jnp.exp(s - jnp.max(s, axis=-1, keepdims=True))
    return e / jnp.sum(e, axis=-1, keepdims=True)


def _load_mem(ref, j):
    if len(ref.shape) == 3:
        return ref[j].astype(BF16)
    return jnp.concatenate([ref[j, :, hd, :] for hd in range(CROSS_HEADS)], axis=-1).astype(BF16)


def _attn_packed_kernel(q_ref, k_ref, v_ref, o_ref, *, n_seq):
    t = q_ref.shape[1]
    shape = (CROSS_HEADS * t, D_MODEL)
    q_mask = (_iota2(shape, 0) // t) == (_iota2(shape, 1) // CROSS_DH)
    o_lane_head = _iota2((t, D_MODEL), 1) // CROSS_DH
    scores = []
    for j in range(n_seq):
        q_stack = jnp.where(q_mask, jnp.concatenate([q_ref[j]] * CROSS_HEADS, axis=0), 0.0).astype(BF16)
        scores.append(_dot_nt(q_stack, _load_mem(k_ref, j)) * (CROSS_DH ** -0.5))
    probs = [_softmax(s).astype(BF16) for s in scores]
    for j in range(n_seq):
        full = _dot(probs[j], _load_mem(v_ref, j))
        o = jnp.zeros((t, D_MODEL), F32)
        for hd in range(CROSS_HEADS):
            o = jnp.where(o_lane_head == hd, full[hd * t:(hd + 1) * t, :], o)
        o_ref[j] = o.astype(o_ref.dtype)


def _attn_kernel(q_ref, k_ref, v_ref, o_ref, *, n_seq):
    cast = lambda a: a.astype(BF16)
    for j in range(n_seq):
        q = q_ref[j]
        outs = []
        for hd in range(CROSS_HEADS):
            sl = slice(hd * CROSS_DH, (hd + 1) * CROSS_DH)
            s = _dot_nt(cast(q[:, sl]), cast(k_ref[j, :, sl])) * (CROSS_DH ** -0.5)
            p = _softmax(s)
            outs.append(_dot(cast(p), cast(v_ref[j, :, sl])))
        o_ref[j] = jnp.concatenate(outs, axis=-1).astype(o_ref.dtype)


def _attn_call(q3, k3, v3, n_seq, tq):
    s, t, _ = q3.shape
    qspec = pl.BlockSpec((n_seq, tq, D_MODEL), lambda i, j: (i, j, 0))
    if k3.ndim == 3:
        kvspec = pl.BlockSpec((n_seq, MEM_LEN, D_MODEL), lambda i, j: (i, 0, 0))
    else:
        kvspec = pl.BlockSpec((n_seq, MEM_LEN, CROSS_HEADS, CROSS_DH), lambda i, j: (i, 0, 0, 0))
    return pl.pallas_call(
        functools.partial(_attn_packed_kernel if tq * CROSS_HEADS <= MXU_DIM else _attn_kernel, n_seq=n_seq),
        grid=(s // n_seq, t // tq),
        in_specs=[qspec, kvspec, kvspec],
        out_specs=qspec,
        out_shape=jax.ShapeDtypeStruct(q3.shape, BF16 if tq % 16 == 0 else F32),
        compiler_params=_cparams(2),
        name="cross_attention",
    )(q3, k3, v3)


def _split3(x):
    hi = x.astype(BF16)
    r1 = x - hi.astype(F32)
    mid = r1.astype(BF16)
    lo = (r1 - mid.astype(F32)).astype(BF16)
    return hi, mid, lo


def _dot01(m01, x):
    hi, mid, lo = _split3(x)
    return _dot(m01, hi) + _dot(m01, mid) + _dot(m01, lo)


def _iota2(shape, dim):
    return lax.broadcasted_iota(jnp.int32, shape, dim)


def _gla_kernel(*refs, seq_len, n_chunks, n_state, has_s0):
    if has_s0:
        qk_ref, v_ref, la_ref, r_ref, g_ref, s0_ref, o_ref, st_ref, s_scr = refs
    else:
        qk_ref, v_ref, la_ref, r_ref, g_ref, o_ref, st_ref, s_scr = refs
        s0_ref = None
    c_rows = GLA_CHUNK_ROWS
    n_sub = c_rows // seq_len
    t_idx = pl.program_id(1)

    @pl.when(t_idx == 0)
    def _():
        if has_s0:
            s_scr[...] = s0_ref[...]
        else:
            s_scr[...] = jnp.zeros_like(s_scr)

    log2 = lambda n: int(math.log2(n))
    ri = _iota2((c_rows, c_rows), 0)
    ci = _iota2((c_rows, c_rows), 1)
    tri = ((ri >> log2(seq_len)) == (ci >> log2(seq_len))) & (ci <= ri)
    tri = jnp.where(tri, 1.0, 0.0).astype(BF16)
    ai = _iota2((c_rows, GLA_HEADS * c_rows), 0)
    asx = _iota2((c_rows, GLA_HEADS * c_rows), 1) & (c_rows - 1)
    rowq = _iota2((c_rows, GLA_QK_WIDTH), 0)
    kr = _iota2((GLA_HEADS * c_rows, GLA_QK_WIDTH), 0)
    kl = _iota2((GLA_HEADS * c_rows, GLA_QK_WIDTH), 1)
    k_head_mask = (kr >> log2(c_rows)) == (kl >> log2(GLA_DK))
    vr = _iota2((GLA_HEADS * c_rows, GLA_WIDTH), 0)
    vl = _iota2((GLA_HEADS * c_rows, GLA_WIDTH), 1)
    v_head_mask = (vr >> log2(c_rows)) == (vl >> log2(GLA_DV))
    sr = _iota2((GLA_HEADS * GLA_DV, GLA_QK_WIDTH), 0)
    slane = _iota2((GLA_HEADS * GLA_DV, GLA_QK_WIDTH), 1)
    s_head_mask = (sr >> log2(GLA_DV)) == (slane >> log2(GLA_DK))
    u_lane_head = _iota2((GLA_DV, GLA_QK_WIDTH), 1) >> log2(GLA_DK)

    def head_blocks(x, mask):
        return jnp.where(mask, jnp.concatenate([x] * GLA_HEADS, axis=0), 0.0)

    g_out = g_ref[...]
    for c in range(n_chunks):
        rows = slice(c * c_rows, (c + 1) * c_rows)
        q = qk_ref[0, rows, 0:GLA_QK_WIDTH] * (GLA_DK ** -0.5)
        k = qk_ref[0, rows, GLA_QK_WIDTH:2 * GLA_QK_WIDTH]
        v = v_ref[0, rows, :]
        la = la_ref[0, rows, :]
        b = _dot01(tri, la)

        a_mat = jnp.where(ai == asx, _dot_nt(q.astype(BF16), head_blocks(k, k_head_mask).astype(BF16)), 0.0)
        last = b
        m = 1
        while m < seq_len:
            prev_last = pltpu.roll(last, m, axis=0)
            eq = jnp.minimum(b - prev_last, 0.0)
            ek = jnp.minimum(last - b, 0.0)
            qm = (q * jnp.exp(eq)).astype(BF16)
            km = head_blocks(k * jnp.exp(ek), k_head_mask).astype(BF16)
            a_lvl = _dot_nt(qm, km)
            iblk = ai >> log2(m)
            lvl_mask = ((iblk - (asx >> log2(m))) == 1) & ((iblk & 1) == 1)
            a_mat = jnp.where(lvl_mask, a_lvl, a_mat)
            even = ((rowq >> log2(m)) & 1) == 0
            last = jnp.where(even, pltpu.roll(last, c_rows - m, axis=0), last)
            m *= 2
        o_intra = _dot(a_mat.astype(BF16), head_blocks(v, v_head_mask).astype(BF16))

        q_dec = q * jnp.exp(b)
        k_dec = k * jnp.exp(jnp.minimum(last - b, 0.0))
        o_parts = []
        for j in range(n_sub):
            sl = slice(j * seq_len, (j + 1) * seq_len)
            s_t = s_scr[j]
            s_blocks = head_blocks(s_t, s_head_mask)
            o_parts.append(_dot_nt(q_dec[sl], s_blocks))
            u_full = _dot_tn(v[sl], k_dec[sl])
            u_t = jnp.zeros((GLA_DV, GLA_QK_WIDTH), F32)
            for hd in range(GLA_HEADS):
                u_t = jnp.where(u_lane_head == hd, u_full[hd * GLA_DV:(hd + 1) * GLA_DV, :], u_t)
            decay = jnp.exp(last[j * seq_len:j * seq_len + 1, :])
            s_scr[j] = s_t * decay + u_t
        o_inter = o_parts[0] if n_sub == 1 else jnp.concatenate(o_parts, axis=0)
        o = o_intra + o_inter

        r = r_ref[0, rows, :]
        outs = []
        for hd in range(GLA_HEADS):
            sl = slice(hd * GLA_DV, (hd + 1) * GLA_DV)
            outs.append(_rms(o[:, sl], g_out) * jax.nn.silu(r[:, sl]))
        o_ref[0, rows, :] = jnp.concatenate(outs, axis=-1).astype(o_ref.dtype)

    @pl.when(t_idx == pl.num_programs(1) - 1)
    def _():
        st_ref[...] = s_scr[...]


def _gla_call(qk3, v3, la3, r3, g_out, s0_t, seq_len, tile_rows, n_state):
    n_outer, rows, _ = qk3.shape
    n_chunks = tile_rows // GLA_CHUNK_ROWS
    spec = lambda w: pl.BlockSpec((1, tile_rows, w), lambda i, t: (i, t, 0))
    st_spec = pl.BlockSpec((n_state, GLA_DV, GLA_QK_WIDTH), lambda i, t: (i, 0, 0))
    has_s0 = s0_t is not None
    in_specs = [spec(512), spec(512), spec(256), spec(512), _const_spec(g_out.shape)]
    args = [qk3, v3, la3, r3, g_out]
    if has_s0:
        in_specs.append(st_spec)
        args.append(s0_t)
    return pl.pallas_call(
        functools.partial(_gla_kernel, seq_len=seq_len, n_chunks=n_chunks, n_state=n_state, has_s0=has_s0),
        grid=(n_outer, rows // tile_rows),
        in_specs=in_specs,
        out_specs=[spec(512), st_spec],
        out_shape=[jax.ShapeDtypeStruct((n_outer, rows, GLA_WIDTH), BF16),
                   jax.ShapeDtypeStruct((n_outer * n_state, GLA_DV, GLA_QK_WIDTH), F32)],
        scratch_shapes=[pltpu.VMEM((n_state, GLA_DV, GLA_QK_WIDTH), F32)],
        compiler_params=_cparams(2),
        name="gla_mixer",
    )(*args)


def _s5_disc_kernel(lre_ref, lim_ref, ldt_ref, bre_ref, bim_ref, are_ref, aim_ref, bbre_ref, bbim_ref):
    lr = lre_ref[...]
    li = lim_ref[...]
    dt = jnp.exp(ldt_ref[...])
    mag = jnp.exp(dt * lr)
    ab_re = mag * jnp.cos(dt * li)
    ab_im = mag * jnp.sin(dt * li)
    den = lr * lr + li * li
    coef_re = ((ab_re - 1.0) * lr + ab_im * li) / den
    coef_im = (ab_im * lr - (ab_re - 1.0) * li) / den
    are_ref[...] = ab_re
    aim_ref[...] = ab_im
    br = bre_ref[...]
    bi = bim_ref[...]
    cr = coef_re[:, None, :]
    cim = coef_im[:, None, :]
    bbre_ref[...] = cr * br - cim * bi
    bbim_ref[...] = cr * bi + cim * br


def _s5_disc_call(lre, lim, ldt, bre_t, bim_t):
    gn = jax.ShapeDtypeStruct((S5_GROUPS, S5_N), F32)
    gpn = jax.ShapeDtypeStruct((S5_GROUPS, S5_P, S5_N), F32)
    return pl.pallas_call(
        _s5_disc_kernel,
        out_shape=[gn, gn, gpn, gpn],
        name="s5_discretise",
    )(lre, lim, ldt, bre_t, bim_t)


def _to_time_major(u_ref, perm_scr, n_seq, t_tile):
    if n_seq == SUBLANES:
        n_slab = S5_WIDTH // LANES
        for j in range(n_seq):
            for sb in range(n_slab):
                perm_scr[sb, pl.ds(j, t_tile, stride=n_seq), :] = u_ref[j, :, sb * LANES:(sb + 1) * LANES]
        return jnp.concatenate([perm_scr[sb] for sb in range(n_slab)], axis=-1)
    return jnp.concatenate([u_ref[:, t, :] for t in range(t_tile)], axis=0)


def _from_time_major(y, o_ref, perm_scr, n_seq, t_tile):
    if n_seq == SUBLANES:
        n_slab = S5_WIDTH // LANES
        for sb in range(n_slab):
            perm_scr[sb] = y[:, sb * LANES:(sb + 1) * LANES]
        for j in range(n_seq):
            for sb in range(n_slab):
                o_ref[j, :, sb * LANES:(sb + 1) * LANES] = perm_scr[sb, pl.ds(j, t_tile, stride=n_seq), :]
    else:
        for t in range(t_tile):
            o_ref[:, t, :] = y[t * n_seq:(t + 1) * n_seq, :]


def _s5_kernel(u_ref, x0re_ref, x0im_ref, are_ref, aim_ref, bre_ref, bim_ref, c_ref, d_ref, wglu_ref,
               bglu_ref, o_ref, xre_ref, xim_ref, sre_scr, sim_scr, xre_scr, xim_scr, perm_scr, *, n_seq, t_tile):
    half_w = S5_WIDTH // 2
    half_s = S5_STATE // 2
    rows = t_tile * n_seq

    @pl.when(pl.program_id(0) == 0)
    def _():
        xre_scr[...] = x0re_ref[...]
        xim_scr[...] = x0im_ref[...]

    u = _to_time_major(u_ref, perm_scr, n_seq, t_tile)
    ub = u.astype(BF16)
    for hf in range(2):
        uh = ub[:, hf * half_w:(hf + 1) * half_w]
        sre_scr[:, :, hf * half_s:(hf + 1) * half_s] = _dot(uh, bre_ref[hf]).reshape(t_tile, n_seq, half_s)
        sim_scr[:, :, hf * half_s:(hf + 1) * half_s] = _dot(uh, bim_ref[hf]).reshape(t_tile, n_seq, half_s)

    for lc in range(S5_STATE // S5_LANE_CHUNK):
        lanes = pl.ds(lc * S5_LANE_CHUNK, S5_LANE_CHUNK)
        ar = jnp.broadcast_to(are_ref[:, lanes], (SUBLANES, S5_LANE_CHUNK))
        ai = jnp.broadcast_to(aim_ref[:, lanes], (SUBLANES, S5_LANE_CHUNK))

        def seq_body(sc, carry, lanes=lanes, ar=ar, ai=ai):
            srows = pl.ds(pl.multiple_of(sc * SUBLANES, SUBLANES), SUBLANES)

            def t_body(t, x):
                xr, xi = x
                nxr = ar * xr - ai * xi + sre_scr[t, srows, lanes]
                nxi = ar * xi + ai * xr + sim_scr[t, srows, lanes]
                sre_scr[t, srows, lanes] = nxr
                sim_scr[t, srows, lanes] = nxi
                return nxr, nxi

            xr, xi = lax.fori_loop(0, t_tile, t_body, (xre_scr[srows, lanes], xim_scr[srows, lanes]))
            xre_scr[srows, lanes] = xr
            xim_scr[srows, lanes] = xi
            return carry

        lax.fori_loop(0, n_seq // SUBLANES, seq_body, 0)

    xr_all = sre_scr[...].reshape(rows, S5_STATE).astype(BF16)
    xi_all = sim_scr[...].reshape(rows, S5_STATE).astype(BF16)
    ys = []
    for hf in range(2):
        st = slice(hf * half_s, (hf + 1) * half_s)
        ys.append(_dot(xr_all[:, st], c_ref[hf, 0:half_s, :]) + _dot(xi_all[:, st], c_ref[hf, half_s:, :]))
    y = jnp.concatenate(ys, axis=-1) + d_ref[...] * u
    z = jax.nn.gelu(y)
    gate = jax.nn.sigmoid(_dot(z.astype(BF16), wglu_ref[...]) + bglu_ref[...])
    _from_time_major(z * gate, o_ref, perm_scr, n_seq, t_tile)

    @pl.when(pl.program_id(0) == pl.num_programs(0) - 1)
    def _():
        xre_ref[...] = xre_scr[...]
        xim_ref[...] = xim_scr[...]


def _s5_call(u3, x0re, x0im, are, aim, bre, bim, cmat, dskip, wglu, bglu, t_tile):
    n_seq, t, _ = u3.shape
    uspec = pl.BlockSpec((n_seq, t_tile, S5_WIDTH), lambda i: (0, i, 0))
    perm_rows = t_tile * n_seq if n_seq == SUBLANES else SUBLANES
    xspec = pl.BlockSpec((n_seq, S5_STATE), lambda i: (0, 0))
    consts = [are, aim, bre, bim, cmat, dskip, wglu, bglu]
    return pl.pallas_call(
        functools.partial(_s5_kernel, n_seq=n_seq, t_tile=t_tile),
        grid=(t // t_tile,),
        in_specs=[uspec, xspec, xspec] + [_const_spec(a.shape) for a in consts],
        out_specs=[uspec, xspec, xspec],
        out_shape=[jax.ShapeDtypeStruct((n_seq, t, S5_WIDTH), F32),
                   jax.ShapeDtypeStruct((n_seq, S5_STATE), F32),
                   jax.ShapeDtypeStruct((n_seq, S5_STATE), F32)],
        scratch_shapes=[pltpu.VMEM((t_tile, n_seq, S5_STATE), F32), pltpu.VMEM((t_tile, n_seq, S5_STATE), F32),
                        pltpu.VMEM((n_seq, S5_STATE), F32), pltpu.VMEM((n_seq, S5_STATE), F32),
                        pltpu.VMEM((S5_WIDTH // LANES, perm_rows, LANES), F32)],
        compiler_params=_cparams(1),
        name="s5_mixer",
    )(u3, x0re, x0im, *consts)


def _block_diag(blocks):
    g, a, b = blocks.shape
    eye = jnp.eye(g, dtype=blocks.dtype)
    return (eye[:, None, :, None] * blocks[:, :, None, :]).reshape(g * a, g * b)


def _halves_block_diag(blocks):
    g = blocks.shape[0]
    return jnp.stack([_block_diag(blocks[:g // 2]), _block_diag(blocks[g // 2:])])


def _trunk_group(x3, s_gla0, x0re, x0im, mem_k3, mem_v3, w, *, tm, gla_tile, gla_states, s5_t_tile,
                 attn_seqs, attn_tq):
    s, t, _ = x3.shape
    n = s * t
    h, qk, v, la, r, us5 = _pre_call(x3.reshape(n, D_MODEL), w['g_ffn1'], w['w_ffn1_gu'], w['w_ffn1_down'],
                                     w['g_mix'], w['w_in_main'], w['w_zg'], w['w_gate'], w['b_gate'], tm)
    if gla_states == 1:
        shp = lambda a: a.reshape(s, t, a.shape[-1])
    else:
        rows = gla_states * t
        shp = lambda a: a.reshape(n // rows, rows, a.shape[-1])
    o_gla, st_new = _gla_call(shp(qk), shp(v), shp(la), shp(r), w['g_gla_out'], s_gla0, t if gla_states > 1 else GLA_CHUNK_ROWS,
                              gla_tile, gla_states)
    o_s5, xre, xim = _s5_call(us5.reshape(s, t, S5_WIDTH), x0re, x0im, w['s5_are'], w['s5_aim'], w['s5_bre'],
                              w['s5_bim'], w['s5_c'], w['s5_d'], w['s5_wglu'], w['s5_bglu'], s5_t_tile)
    if t >= tm:
        y = _tail_call(h.reshape(s, t, D_MODEL), o_gla.reshape(s, t, GLA_WIDTH), o_s5, mem_k3, mem_v3, w, tm)
        return y, st_new, xre, xim
    o_s5 = o_s5.reshape(n, S5_WIDTH)
    h2, q = _mid_call(h, o_gla.reshape(n, GLA_WIDTH), o_s5, w['w_out'], w['g_cross'], w['w_cq'], tm)
    o_att = _attn_call(q.reshape(s, t, D_MODEL), mem_k3, mem_v3, attn_seqs, attn_tq)
    y = _post_call(h2, o_att.reshape(n, D_MODEL), w['w_co'], w['g_ffn2'], w['w_ffn2_gu'], w['w_ffn2_down'],
                   w['g_final'], tm)
    return y.reshape(s, t, D_MODEL), st_new, xre, xim


def kernel(x_prompt, x_sample, state_gla, state_s5_re, state_s5_im, cache_mem_k, cache_mem_v, mem_prompt, g_ffn1, w_ffn1_gu, w_ffn1_down, g_mix, w_in, w_gla_gate, b_gla_gate, g_gla_out, s5_lambda_re, s5_lambda_im, s5_log_dt, s5_b_re, s5_b_im, s5_c_re, s5_c_im, s5_d, s5_w_glu, s5_b_glu, w_out, g_cross, g_mem, w_cq, w_ckv, w_co, g_ffn2, w_ffn2_gu, w_ffn2_down, g_final):
    bp, tp, _ = x_prompt.shape
    bs, ts, _ = x_sample.shape
    l = 0
    row = lambda a: a.reshape(1, -1).astype(F32)
    bf = lambda a: a.astype(BF16)

    are, aim, bbre, bbim = _s5_disc_call(
        s5_lambda_re[l], s5_lambda_im[l], s5_log_dt[l].reshape(S5_GROUPS, 1),
        jnp.transpose(s5_b_re[l], (0, 2, 1)), jnp.transpose(s5_b_im[l], (0, 2, 1)))
    c_re_t = jnp.transpose(s5_c_re[l], (0, 2, 1))
    c_im_t = jnp.transpose(s5_c_im[l], (0, 2, 1))
    cre_h = _halves_block_diag(c_re_t)
    cim_h = _halves_block_diag(-c_im_t)
    w_in_l = w_in[l]
    zg_lo = 2 * GLA_QK_WIDTH + GLA_WIDTH
    zg_hi = zg_lo + GLA_GATE_RANK
    w = {
        'g_ffn1': row(g_ffn1[l]), 'w_ffn1_gu': bf(w_ffn1_gu[l]), 'w_ffn1_down': bf(w_ffn1_down[l]),
        'g_mix': row(g_mix[l]),
        'w_in_main': bf(jnp.concatenate([w_in_l[:, :zg_lo], w_in_l[:, zg_hi:]], axis=1)),
        'w_zg': bf(jnp.pad(w_in_l[:, zg_lo:zg_hi], ((0, 0), (0, LANES - GLA_GATE_RANK)))),
        'w_gate': bf(jnp.pad(w_gla_gate[l], ((0, LANES - GLA_GATE_RANK), (0, 0)))),
        'b_gate': row(b_gla_gate[l]),
        'g_gla_out': row(g_gla_out[l]),
        's5_are': are.reshape(1, S5_STATE), 's5_aim': aim.reshape(1, S5_STATE),
        's5_bre': bf(_halves_block_diag(bbre)), 's5_bim': bf(_halves_block_diag(bbim)),
        's5_c': bf(jnp.concatenate([cre_h, cim_h], axis=1)),
        's5_d': row(s5_d[l]), 's5_wglu': bf(_block_diag(s5_w_glu[l])), 's5_bglu': row(s5_b_glu[l]),
        'w_out': bf(w_out[l]), 'g_cross': row(g_cross[l]), 'w_cq': bf(w_cq[l]), 'w_co': bf(w_co[l]),
        'g_ffn2': row(g_ffn2[l]), 'w_ffn2_gu': bf(w_ffn2_gu[l]), 'w_ffn2_down': bf(w_ffn2_down[l]),
        'g_final': row(g_final),
    }

    mk, mv, mk4, mv4 = _memkv_call(mem_prompt.reshape(bp * MEM_LEN, D_MODEL), row(g_mem[l]), bf(w_ckv[l]), 512)
    zeros_state = jnp.zeros((bp, S5_STATE), F32)
    y_p, st_p, re_p, im_p = _trunk_group(
        x_prompt, None, zeros_state, zeros_state, mk.reshape(bp, MEM_LEN, D_MODEL), mv.reshape(bp, MEM_LEN, D_MODEL),
        w, tm=512, gla_tile=256, gla_states=1, s5_t_tile=128, attn_seqs=1, attn_tq=512)

    seqs_per_chunk = GLA_CHUNK_ROWS // ts
    s0_t = jnp.transpose(state_gla[l], (0, 3, 1, 2)).reshape(bs, GLA_DV, GLA_QK_WIDTH)
    y_s, st_s, re_s, im_s = _trunk_group(
        x_sample, s0_t, state_s5_re[l].reshape(bs, S5_STATE), state_s5_im[l].reshape(bs, S5_STATE),
        cache_mem_k.reshape(DEPTH * bs, MEM_LEN, CROSS_HEADS, CROSS_DH)[l * bs:(l + 1) * bs],
        cache_mem_v.reshape(DEPTH * bs, MEM_LEN, CROSS_HEADS, CROSS_DH)[l * bs:(l + 1) * bs],
        w, tm=512, gla_tile=GLA_CHUNK_ROWS, gla_states=seqs_per_chunk, s5_t_tile=ts, attn_seqs=4, attn_tq=ts)

    def gla_state_out(st, nb):
        return jnp.transpose(st.reshape(nb, GLA_DV, GLA_HEADS, GLA_DK), (0, 2, 3, 1))[None]

    s5_out = lambda a, nb: a.reshape(1, nb, S5_GROUPS, S5_N)
    kv_out = lambda a: a.reshape(1, bp, MEM_LEN, CROSS_HEADS, CROSS_DH)
    return (y_p, y_s,
            gla_state_out(st_p, bp), s5_out(re_p, bp), s5_out(im_p, bp), kv_out(mk4), kv_out(mv4),
            gla_state_out(st_s, bs), s5_out(re_s, bs), s5_out(im_s, bs))
```

```python
import functools
import math

import numpy as np
import jax
import jax.numpy as jnp
from jax import lax
from jax.experimental import pallas as pl
from jax.experimental.pallas import tpu as pltpu

F32 = jnp.float32
BF16 = jnp.bfloat16

D_MODEL = 1024
DEPTH = 1
EPS = 1e-6
D_FF = 2816
GLA_WIDTH = 512
GLA_HEADS = 4
GLA_DV = 128
GLA_DK = 64
GLA_QK_WIDTH = 256
GLA_GATE_RANK = 16
GLA_TAU = 16.0
S5_WIDTH = 512
S5_P = 16
S5_GROUPS = 32
S5_N = 64
S5_STATE = S5_GROUPS * S5_N
CROSS_HEADS = 4
CROSS_DH = 256
MEM_LEN = 256

LANES = 128
SUBLANES = 8
MXU_DIM = 256
VMEM_LIMIT_BYTES = 56 * 1024 * 1024

FF_CHUNK = MXU_DIM
GLA_CHUNK_ROWS = 64
S5_LANE_CHUNK = 512


def _cparams(n_grid_axes):
    return pltpu.CompilerParams(
        dimension_semantics=("arbitrary",) * n_grid_axes,
        vmem_limit_bytes=VMEM_LIMIT_BYTES,
    )


def _const_spec(shape):
    zeros = (0,) * len(shape)
    return pl.BlockSpec(shape, lambda *_: zeros, pipeline_mode=pl.Buffered(1))


def _dot(a, b):
    return jnp.dot(a, b, preferred_element_type=F32)


def _dot_nt(a, b):
    return lax.dot_general(a, b, (((1,), (1,)), ((), ())), preferred_element_type=F32)


def _dot_tn(a, b):
    return lax.dot_general(a, b, (((0,), (0,)), ((), ())), preferred_element_type=F32)


def _rms(x, g):
    return x * lax.rsqrt(jnp.mean(x * x, axis=-1, keepdims=True) + EPS) * g


def _swiglu(xn_bf, wgu_ref, wdn_ref):
    acc = None
    for c in range(D_FF // FF_CHUNK):
        lo, hi = c * FF_CHUNK, (c + 1) * FF_CHUNK
        g = _dot(xn_bf, wgu_ref[:, lo:hi])
        u = _dot(xn_bf, wgu_ref[:, D_FF + lo:D_FF + hi])
        a = (jax.nn.silu(g) * u).astype(BF16)
        d = _dot(a, wdn_ref[lo:hi, :])
        acc = d if acc is None else acc + d
    return acc


def _log_sigmoid(z):
    return jnp.minimum(z, 0.0) - jnp.log1p(jnp.exp(-jnp.abs(z)))


def _pre_kernel(x_ref, g1_ref, wgu_ref, wdn_ref, gm_ref, win_ref, wzg_ref, wgate_ref, bgate_ref,
                h_ref, qk_ref, v_ref, la_ref, r_ref, us5_ref):
    x = x_ref[...]
    xn = _rms(x, g1_ref[...]).astype(BF16)
    h = x + 0.5 * _swiglu(xn, wgu_ref, wdn_ref)
    h_ref[...] = h
    un = _rms(h, gm_ref[...]).astype(BF16)
    p = _dot(un, win_ref[...])
    qk_ref[...] = p[:, 0:512]
    v_ref[...] = p[:, 512:1024]
    r_ref[...] = p[:, 1024:1536]
    us5_ref[...] = p[:, 1536:2048]
    zg = _dot(un, wzg_ref[...])
    z = _dot(zg.astype(BF16), wgate_ref[...]) + bgate_ref[...]
    la_ref[...] = _log_sigmoid(z) * (1.0 / GLA_TAU)


def _pre_call(x2d, g1, wgu, wdn, gm, win, wzg, wgate, bgate, tm):
    n = x2d.shape[0]
    row = lambda w: pl.BlockSpec((tm, w), lambda i: (i, 0))
    out_shapes = [jax.ShapeDtypeStruct((n, w), F32) for w in (D_MODEL, 512, 512, 256, 512, 512)]
    return pl.pallas_call(
        _pre_kernel,
        grid=(n // tm,),
        in_specs=[row(D_MODEL), _const_spec(g1.shape), _const_spec(wgu.shape), _const_spec(wdn.shape),
                  _const_spec(gm.shape), _const_spec(win.shape), _const_spec(wzg.shape),
                  _const_spec(wgate.shape), _const_spec(bgate.shape)],
        out_specs=[row(D_MODEL), row(512), row(512), row(256), row(512), row(512)],
        out_shape=out_shapes,
        compiler_params=_cparams(1),
        name="pre_ffn_proj",
    )(x2d, g1, wgu, wdn, gm, win, wzg, wgate, bgate)


def _memkv_kernel(m_ref, g_ref, w_ref, k_ref, v_ref, k4_ref, v4_ref):
    mn = _rms(m_ref[...], g_ref[...]).astype(BF16)
    kv = _dot(mn, w_ref[...])
    k_ref[...] = kv[:, :D_MODEL]
    v_ref[...] = kv[:, D_MODEL:]
    for hd in range(CROSS_HEADS):
        k4_ref[:, hd, :] = kv[:, hd * CROSS_DH:(hd + 1) * CROSS_DH]
        v4_ref[:, hd, :] = kv[:, D_MODEL + hd * CROSS_DH:D_MODEL + (hd + 1) * CROSS_DH]


def _memkv_call(mem2d, g, w, tm):
    n = mem2d.shape[0]
    row = pl.BlockSpec((tm, D_MODEL), lambda i: (i, 0))
    row4 = pl.BlockSpec((tm, CROSS_HEADS, CROSS_DH), lambda i: (i, 0, 0))
    return pl.pallas_call(
        _memkv_kernel,
        grid=(n // tm,),
        in_specs=[row, _const_spec(g.shape), _const_spec(w.shape)],
        out_specs=[row, row, row4, row4],
        out_shape=[jax.ShapeDtypeStruct((n, D_MODEL), F32)] * 2
        + [jax.ShapeDtypeStruct((n, CROSS_HEADS, CROSS_DH), F32)] * 2,
        compiler_params=_cparams(1),
        name="memory_kv",
    )(mem2d, g, w)


def _mid_kernel(h_ref, og_ref, os_ref, wo_ref, gc_ref, wq_ref, h2_ref, q_ref):
    h2 = (h_ref[...] + _dot(og_ref[...].astype(BF16), wo_ref[0:GLA_WIDTH, :])
          + _dot(os_ref[...].astype(BF16), wo_ref[GLA_WIDTH:, :]))
    h2_ref[...] = h2
    hn = _rms(h2, gc_ref[...]).astype(BF16)
    q_ref[...] = _dot(hn, wq_ref[...])


def _mid_call(h, og, os_, wo, gc, wq, tm):
    n = h.shape[0]
    row = lambda w: pl.BlockSpec((tm, w), lambda i: (i, 0))
    return pl.pallas_call(
        _mid_kernel,
        grid=(n // tm,),
        in_specs=[row(D_MODEL), row(512), row(512), _const_spec(wo.shape), _const_spec(gc.shape),
                  _const_spec(wq.shape)],
        out_specs=[row(D_MODEL), row(D_MODEL)],
        out_shape=[jax.ShapeDtypeStruct((n, D_MODEL), F32)] * 2,
        compiler_params=_cparams(1),
        name="mix_out_cross_q",
    )(h, og, os_, wo, gc, wq)


def _post_kernel(h2_ref, o_ref, wco_ref, g2_ref, wgu_ref, wdn_ref, gf_ref, y_ref):
    h3 = h2_ref[...] + _dot(o_ref[...].astype(BF16), wco_ref[...])
    hn = _rms(h3, g2_ref[...]).astype(BF16)
    h4 = h3 + 0.5 * _swiglu(hn, wgu_ref, wdn_ref)
    y_ref[...] = _rms(h4, gf_ref[...])


def _post_call(h2, o, wco, g2, wgu, wdn, gf, tm):
    n = h2.shape[0]
    row = pl.BlockSpec((tm, D_MODEL), lambda i: (i, 0))
    return pl.pallas_call(
        _post_kernel,
        grid=(n // tm,),
        in_specs=[row, row, _const_spec(wco.shape), _const_spec(g2.shape), _const_spec(wgu.shape),
                  _const_spec(wdn.shape), _const_spec(gf.shape)],
        out_specs=row,
        out_shape=jax.ShapeDtypeStruct((n, D_MODEL), F32),
        compiler_params=_cparams(1),
        name="post_ffn_final",
    )(h2, o, wco, g2, wgu, wdn, gf)


def _tail_kernel(h_ref, og_ref, os_ref, k_ref, v_ref, wo_ref, gc_ref, wq_ref, wco_ref, g2_ref, wgu_ref,
                 wdn_ref, gf_ref, y_ref):
    h2 = (h_ref[0] + _dot(og_ref[0].astype(BF16), wo_ref[0:GLA_WIDTH, :])
          + _dot(os_ref[0].astype(BF16), wo_ref[GLA_WIDTH:, :]))
    q = _dot(_rms(h2, gc_ref[...]).astype(BF16), wq_ref[...]).astype(BF16)
    outs = []
    for hd in range(CROSS_HEADS):
        sl = slice(hd * CROSS_DH, (hd + 1) * CROSS_DH)
        s = _dot_nt(q[:, sl], k_ref[0, :, sl].astype(BF16)) * (CROSS_DH ** -0.5)
        outs.append(_dot(_softmax(s).astype(BF16), v_ref[0, :, sl].astype(BF16)).astype(BF16))
    h3 = h2 + _dot(jnp.concatenate(outs, axis=-1), wco_ref[...])
    hn = _rms(h3, g2_ref[...]).astype(BF16)
    h4 = h3 + 0.5 * _swiglu(hn, wgu_ref, wdn_ref)
    y_ref[0] = _rms(h4, gf_ref[...])


def _tail_call(h3d, og3, os3, k3, v3, w, tm):
    s, t, _ = h3d.shape
    row = lambda wd: pl.BlockSpec((1, tm, wd), lambda i, j: (i, j, 0))
    kvspec = pl.BlockSpec((1, MEM_LEN, D_MODEL), lambda i, j: (i, 0, 0))
    consts = [w['w_out'], w['g_cross'], w['w_cq'], w['w_co'], w['g_ffn2'], w['w_ffn2_gu'], w['w_ffn2_down'],
              w['g_final']]
    return pl.pallas_call(
        _tail_kernel,
        grid=(s, t // tm),
        in_specs=[row(D_MODEL), row(GLA_WIDTH), row(S5_WIDTH), kvspec, kvspec] + [_const_spec(a.shape) for a in consts],
        out_specs=row(D_MODEL),
        out_shape=jax.ShapeDtypeStruct((s, t, D_MODEL), F32),
        compiler_params=_cparams(2),
        name="tail_attn_ffn",
    )(h3d, og3, os3, k3, v3, *consts)


def _softmax(s):
    e = jnp.exp(s - jnp.max(s, axis=-1, keepdims=True))
    return e / jnp.sum(e, axis=-1, keepdims=True)


def _load_mem(ref, j):
    if len(ref.shape) == 3:
        return ref[j].astype(BF16)
    return jnp.concatenate([ref[j, :, hd, :] for hd in range(CROSS_HEADS)], axis=-1).astype(BF16)


def _attn_packed_kernel(q_ref, k_ref, v_ref, o_ref, *, n_seq):
    t = q_ref.shape[1]
    shape = (CROSS_HEADS * t, D_MODEL)
    q_mask = (_iota2(shape, 0) // t) == (_iota2(shape, 1) // CROSS_DH)
    o_lane_head = _iota2((t, D_MODEL), 1) // CROSS_DH
    scores = []
    for j in range(n_seq):
        q_stack = jnp.where(q_mask, jnp.concatenate([q_ref[j]] * CROSS_HEADS, axis=0), 0.0).astype(BF16)
        scores.append(_dot_nt(q_stack, _load_mem(k_ref, j)) * (CROSS_DH ** -0.5))
    probs = [_softmax(s).astype(BF16) for s in scores]
    for j in range(n_seq):
        full = _dot(probs[j], _load_mem(v_ref, j))
        o = jnp.zeros((t, D_MODEL), F32)
        for hd in range(CROSS_HEADS):
            o = jnp.where(o_lane_head == hd, full[hd * t:(hd + 1) * t, :], o)
        o_ref[j] = o.astype(o_ref.dtype)


def _attn_kernel(q_ref, k_ref, v_ref, o_ref, *, n_seq):
    cast = lambda a: a.astype(BF16)
    for j in range(n_seq):
        q = q_ref[j]
        outs = []
        for hd in range(CROSS_HEADS):
            sl = slice(hd * CROSS_DH, (hd + 1) * CROSS_DH)
            s = _dot_nt(cast(q[:, sl]), cast(k_ref[j, :, sl])) * (CROSS_DH ** -0.5)
            p = _softmax(s)
            outs.append(_dot(cast(p), cast(v_ref[j, :, sl])))
        o_ref[j] = jnp.concatenate(outs, axis=-1).astype(o_ref.dtype)


def _attn_call(q3, k3, v3, n_seq, tq):
    s, t, _ = q3.shape
    qspec = pl.BlockSpec((n_seq, tq, D_MODEL), lambda i, j: (i, j, 0))
    if k3.ndim == 3:
        kvspec = pl.BlockSpec((n_seq, MEM_LEN, D_MODEL), lambda i, j: (i, 0, 0))
    else:
        kvspec = pl.BlockSpec((n_seq, MEM_LEN, CROSS_HEADS, CROSS_DH), lambda i, j: (i, 0, 0, 0))
    return pl.pallas_call(
        functools.partial(_attn_packed_kernel if tq * CROSS_HEADS <= MXU_DIM else _attn_kernel, n_seq=n_seq),
        grid=(s // n_seq, t // tq),
        in_specs=[qspec, kvspec, kvspec],
        out_specs=qspec,
        out_shape=jax.ShapeDtypeStruct(q3.shape, BF16 if tq % 16 == 0 else F32),
        compiler_params=_cparams(2),
        name="cross_attention",
    )(q3, k3, v3)


def _split3(x):
    hi = x.astype(BF16)
    r1 = x - hi.astype(F32)
    mid = r1.astype(BF16)
    lo = (r1 - mid.astype(F32)).astype(BF16)
    return hi, mid, lo


def _dot01(m01, x):
    hi, mid, lo = _split3(x)
    return _dot(m01, hi) + _dot(m01, mid) + _dot(m01, lo)


def _iota2(shape, dim):
    return lax.broadcasted_iota(jnp.int32, shape, dim)


def _gla_kernel(*refs, seq_len, n_chunks, n_state, has_s0):
    qk_ref, v_ref, la_ref, r_ref, g_ref, tri_ref, lvl_ref, khm_ref, vhm_ref, shm_ref = refs[:10]
    if has_s0:
        s0_ref, o_ref, st_ref, s_scr = refs[10:]
    else:
        o_ref, st_ref, s_scr = refs[10:]
        s0_ref = None
    c_rows = GLA_CHUNK_ROWS
    n_sub = c_rows // seq_len
    t_idx = pl.program_id(1)

    @pl.when(t_idx == 0)
    def _():
        if has_s0:
            s_scr[...] = s0_ref[...]
        else:
            s_scr[...] = jnp.zeros_like(s_scr)

    log2 = lambda n: int(math.log2(n))
    tri = tri_ref[...]
    lvl = lvl_ref[...]
    rowq = _iota2((c_rows, GLA_QK_WIDTH), 0)
    u_head = [(_iota2((GLA_DV, GLA_QK_WIDTH), 1) >> log2(GLA_DK)) == hd for hd in range(GLA_HEADS)]

    def head_blocks(x, mask_ref):
        return jnp.concatenate([x.astype(BF16)] * GLA_HEADS, axis=0) * mask_ref[...]

    g_out = g_ref[...]
    chunks = range(n_chunks)
    rows_of = lambda c: slice(c * c_rows, (c + 1) * c_rows)
    sub_of = lambda j: slice(j * seq_len, (j + 1) * seq_len)
    q = [qk_ref[0, rows_of(c), 0:GLA_QK_WIDTH] * (GLA_DK ** -0.5) for c in chunks]
    k = [qk_ref[0, rows_of(c), GLA_QK_WIDTH:2 * GLA_QK_WIDTH] for c in chunks]
    v = [v_ref[0, rows_of(c), :] for c in chunks]
    b = [_dot01(tri, la_ref[0, rows_of(c), :]) for c in chunks]

    a_mat = [jnp.where(lvl == 0, _dot_nt(q[c].astype(BF16), head_blocks(k[c], khm_ref)), 0.0) for c in chunks]
    last = list(b)
    m = 1
    while m < seq_len:
        even = ((rowq >> log2(m)) & 1) == 0
        for c in chunks:
            prev_last = pltpu.roll(last[c], m, axis=0)
            e = jnp.exp(jnp.minimum(jnp.where(even, last[c] - b[c], b[c] - prev_last), 0.0))
            a_lvl = _dot_nt((q[c] * e).astype(BF16), head_blocks(k[c] * e, khm_ref))
            a_mat[c] = jnp.where(lvl == log2(m) + 1, a_lvl, a_mat[c])
            last[c] = jnp.where(even, pltpu.roll(last[c], c_rows - m, axis=0), last[c])
        m *= 2
    o_intra = [_dot(a_mat[c].astype(BF16), head_blocks(v[c], vhm_ref)) for c in chunks]

    narrow = (lambda a: a.astype(BF16)) if seq_len % 16 == 0 else (lambda a: a)
    q_dec = [narrow(q[c] * jnp.exp(b[c])) for c in chunks]
    k_dec = [narrow(k[c] * jnp.exp(jnp.minimum(last[c] - b[c], 0.0))) for c in chunks]
    u_t = {}
    for c in chunks:
        vc = narrow(v[c])
        for j in range(n_sub):
            u_full = _dot_tn(vc[sub_of(j)], k_dec[c][sub_of(j)])
            acc = jnp.zeros((GLA_DV, GLA_QK_WIDTH), F32)
            for hd in range(GLA_HEADS):
                acc = jnp.where(u_head[hd], u_full[hd * GLA_DV:(hd + 1) * GLA_DV, :], acc)
            u_t[c, j] = acc

    states = [s_scr[j] for j in range(n_sub)]
    for c in chunks:
        o_parts = []
        for j in range(n_sub):
            o_parts.append(_dot_nt(q_dec[c][sub_of(j)], head_blocks(states[j], shm_ref)))
            decay = jnp.exp(last[c][j * seq_len:j * seq_len + 1, :])
            states[j] = states[j] * decay + u_t[c, j]
        o_inter = o_parts[0] if n_sub == 1 else jnp.concatenate(o_parts, axis=0)
        o = o_intra[c] + o_inter
        r = r_ref[0, rows_of(c), :]
        outs = []
        for hd in range(GLA_HEADS):
            sl = slice(hd * GLA_DV, (hd + 1) * GLA_DV)
            outs.append(_rms(o[:, sl], g_out) * jax.nn.silu(r[:, sl]))
        o_ref[0, rows_of(c), :] = jnp.concatenate(outs, axis=-1).astype(o_ref.dtype)
    for j in range(n_sub):
        s_scr[j] = states[j]

    @pl.when(t_idx == pl.num_programs(1) - 1)
    def _():
        st_ref[...] = s_scr[...]


def _gla_index_constants(seq_len):
    c = GLA_CHUNK_ROWS
    i = np.arange(c)[:, None]
    s = np.arange(c)[None, :]
    same_seq = (i // seq_len) == (s // seq_len)
    tri = (same_seq & (s <= i)).astype(np.float32)
    top_bit = np.floor(np.log2(np.maximum(i ^ s, 1))).astype(np.int32)
    lvl = np.where(same_seq & (s < i), top_bit + 1, -1)
    lvl = np.where(i == s, 0, lvl).astype(np.int32)
    head_mask = lambda rows_per_head, lanes_per_head: (
        (np.arange(GLA_HEADS * rows_per_head)[:, None] // rows_per_head)
        == (np.arange(GLA_HEADS * lanes_per_head)[None, :] // lanes_per_head)).astype(np.float32)
    return [jnp.asarray(tri, BF16), jnp.asarray(np.tile(lvl, (1, GLA_HEADS))),
            jnp.asarray(head_mask(c, GLA_DK), BF16), jnp.asarray(head_mask(c, GLA_DV), BF16),
            jnp.asarray(head_mask(GLA_DV, GLA_DK), BF16)]


def _gla_call(qk3, v3, la3, r3, g_out, s0_t, seq_len, tile_rows, n_state):
    n_outer, rows, _ = qk3.shape
    n_chunks = tile_rows // GLA_CHUNK_ROWS
    spec = lambda w: pl.BlockSpec((1, tile_rows, w), lambda i, t: (i, t, 0))
    st_spec = pl.BlockSpec((n_state, GLA_DV, GLA_QK_WIDTH), lambda i, t: (i, 0, 0))
    has_s0 = s0_t is not None
    consts = [g_out] + _gla_index_constants(seq_len)
    in_specs = [spec(512), spec(512), spec(256), spec(512)] + [_const_spec(a.shape) for a in consts]
    args = [qk3, v3, la3, r3] + consts
    if has_s0:
        in_specs.append(st_spec)
        args.append(s0_t)
    return pl.pallas_call(
        functools.partial(_gla_kernel, seq_len=seq_len, n_chunks=n_chunks, n_state=n_state, has_s0=has_s0),
        grid=(n_outer, rows // tile_rows),
        in_specs=in_specs,
        out_specs=[spec(512), st_spec],
        out_shape=[jax.ShapeDtypeStruct((n_outer, rows, GLA_WIDTH), BF16),
                   jax.ShapeDtypeStruct((n_outer * n_state, GLA_DV, GLA_QK_WIDTH), F32)],
        scratch_shapes=[pltpu.VMEM((n_state, GLA_DV, GLA_QK_WIDTH), F32)],
        compiler_params=_cparams(2),
        name="gla_mixer",
    )(*args)


def _s5_disc_kernel(lre_ref, lim_ref, ldt_ref, bre_ref, bim_ref, are_ref, aim_ref, bbre_ref, bbim_ref):
    lr = lre_ref[...]
    li = lim_ref[...]
    dt = jnp.exp(ldt_ref[...])
    mag = jnp.exp(dt * lr)
    ab_re = mag * jnp.cos(dt * li)
    ab_im = mag * jnp.sin(dt * li)
    den = lr * lr + li * li
    coef_re = ((ab_re - 1.0) * lr + ab_im * li) / den
    coef_im = (ab_im * lr - (ab_re - 1.0) * li) / den
    are_ref[...] = ab_re
    aim_ref[...] = ab_im
    br = bre_ref[...]
    bi = bim_ref[...]
    cr = coef_re[:, None, :]
    cim = coef_im[:, None, :]
    bbre_ref[...] = cr * br - cim * bi
    bbim_ref[...] = cr * bi + cim * br


def _s5_disc_call(lre, lim, ldt, bre_t, bim_t):
    gn = jax.ShapeDtypeStruct((S5_GROUPS, S5_N), F32)
    gpn = jax.ShapeDtypeStruct((S5_GROUPS, S5_P, S5_N), F32)
    return pl.pallas_call(
        _s5_disc_kernel,
        out_shape=[gn, gn, gpn, gpn],
        name="s5_discretise",
    )(lre, lim, ldt, bre_t, bim_t)


def _to_time_major(u_ref, perm_scr, n_seq, t_tile):
    if n_seq == SUBLANES:
        n_slab = S5_WIDTH // LANES
        for j in range(n_seq):
            for sb in range(n_slab):
                perm_scr[sb, pl.ds(j, t_tile, stride=n_seq), :] = u_ref[j, :, sb * LANES:(sb + 1) * LANES]
        return jnp.concatenate([perm_scr[sb] for sb in range(n_slab)], axis=-1)
    return jnp.concatenate([u_ref[:, t, :] for t in range(t_tile)], axis=0)


def _from_time_major(y, o_ref, perm_scr, n_seq, t_tile):
    if n_seq == SUBLANES:
        n_slab = S5_WIDTH // LANES
        for sb in range(n_slab):
            perm_scr[sb] = y[:, sb * LANES:(sb + 1) * LANES]
        for j in range(n_seq):
            for sb in range(n_slab):
                o_ref[j, :, sb * LANES:(sb + 1) * LANES] = perm_scr[sb, pl.ds(j, t_tile, stride=n_seq), :]
    else:
        for t in range(t_tile):
            o_ref[:, t, :] = y[t * n_seq:(t + 1) * n_seq, :]


def _s5_kernel(u_ref, x0re_ref, x0im_ref, are_ref, aim_ref, bre_ref, bim_ref, c_ref, d_ref, wglu_ref,
               bglu_ref, o_ref, xre_ref, xim_ref, sre_scr, sim_scr, xre_scr, xim_scr, perm_scr, *, n_seq, t_tile):
    half_w = S5_WIDTH // 2
    half_s = S5_STATE // 2
    rows = t_tile * n_seq

    @pl.when(pl.program_id(0) == 0)
    def _():
        xre_scr[...] = x0re_ref[...]
        xim_scr[...] = x0im_ref[...]

    u = _to_time_major(u_ref, perm_scr, n_seq, t_tile)
    ub = u.astype(BF16)
    for hf in range(2):
        uh = ub[:, hf * half_w:(hf + 1) * half_w]
        sre_scr[:, :, hf * half_s:(hf + 1) * half_s] = _dot(uh, bre_ref[hf]).reshape(t_tile, n_seq, half_s)
        sim_scr[:, :, hf * half_s:(hf + 1) * half_s] = _dot(uh, bim_ref[hf]).reshape(t_tile, n_seq, half_s)

    for lc in range(S5_STATE // S5_LANE_CHUNK):
        lanes = slice(lc * S5_LANE_CHUNK, (lc + 1) * S5_LANE_CHUNK)
        ar = jnp.broadcast_to(are_ref[:, lanes], (SUBLANES, S5_LANE_CHUNK))
        ai = jnp.broadcast_to(aim_ref[:, lanes], (SUBLANES, S5_LANE_CHUNK))
        for sc in range(n_seq // SUBLANES):
            srows = slice(sc * SUBLANES, (sc + 1) * SUBLANES)
            xr = xre_scr[srows, lanes]
            xi = xim_scr[srows, lanes]
            for t in range(t_tile):
                nxr = ar * xr - ai * xi + sre_scr[t, srows, lanes]
                nxi = ar * xi + ai * xr + sim_scr[t, srows, lanes]
                sre_scr[t, srows, lanes] = nxr
                sim_scr[t, srows, lanes] = nxi
                xr, xi = nxr, nxi
            xre_scr[srows, lanes] = xr
            xim_scr[srows, lanes] = xi

    xr_all = sre_scr[...].reshape(rows, S5_STATE).astype(BF16)
    xi_all = sim_scr[...].reshape(rows, S5_STATE).astype(BF16)
    ys = []
    for hf in range(2):
        st = slice(hf * half_s, (hf + 1) * half_s)
        ys.append(_dot(xr_all[:, st], c_ref[hf, 0:half_s, :]) + _dot(xi_all[:, st], c_ref[hf, half_s:, :]))
    y = jnp.concatenate(ys, axis=-1) + d_ref[...] * u
    z = jax.nn.gelu(y)
    gate = jax.nn.sigmoid(_dot(z.astype(BF16), wglu_ref[...]) + bglu_ref[...])
    _from_time_major(z * gate, o_ref, perm_scr, n_seq, t_tile)

    @pl.when(pl.program_id(0) == pl.num_programs(0) - 1)
    def _():
        xre_ref[...] = xre_scr[...]
        xim_ref[...] = xim_scr[...]


def _s5_call(u3, x0re, x0im, are, aim, bre, bim, cmat, dskip, wglu, bglu, t_tile):
    n_seq, t, _ = u3.shape
    uspec = pl.BlockSpec((n_seq, t_tile, S5_WIDTH), lambda i: (0, i, 0))
    perm_rows = t_tile * n_seq if n_seq == SUBLANES else SUBLANES
    xspec = pl.BlockSpec((n_seq, S5_STATE), lambda i: (0, 0))
    consts = [are, aim, bre, bim, cmat, dskip, wglu, bglu]
    return pl.pallas_call(
        functools.partial(_s5_kernel, n_seq=n_seq, t_tile=t_tile),
        grid=(t // t_tile,),
        in_specs=[uspec, xspec, xspec] + [_const_spec(a.shape) for a in consts],
        out_specs=[uspec, xspec, xspec],
        out_shape=[jax.ShapeDtypeStruct((n_seq, t, S5_WIDTH), F32),
                   jax.ShapeDtypeStruct((n_seq, S5_STATE), F32),
                   jax.ShapeDtypeStruct((n_seq, S5_STATE), F32)],
        scratch_shapes=[pltpu.VMEM((t_tile, n_seq, S5_STATE), F32), pltpu.VMEM((t_tile, n_seq, S5_STATE), F32),
                        pltpu.VMEM((n_seq, S5_STATE), F32), pltpu.VMEM((n_seq, S5_STATE), F32),
                        pltpu.VMEM((S5_WIDTH // LANES, perm_rows, LANES), F32)],
        compiler_params=_cparams(1),
        name="s5_mixer",
    )(u3, x0re, x0im, *consts)


def _block_diag(blocks):
    g, a, b = blocks.shape
    eye = jnp.eye(g, dtype=blocks.dtype)
    return (eye[:, None, :, None] * blocks[:, :, None, :]).reshape(g * a, g * b)


def _halves_block_diag(blocks):
    g = blocks.shape[0]
    return jnp.stack([_block_diag(blocks[:g // 2]), _block_diag(blocks[g // 2:])])


def _trunk_group(x3, s_gla0, x0re, x0im, mem_k3, mem_v3, w, *, tm, gla_tile, gla_states, s5_t_tile,
                 attn_seqs, attn_tq):
    s, t, _ = x3.shape
    n = s * t
    h, qk, v, la, r, us5 = _pre_call(x3.reshape(n, D_MODEL), w['g_ffn1'], w['w_ffn1_gu'], w['w_ffn1_down'],
                                     w['g_mix'], w['w_in_main'], w['w_zg'], w['w_gate'], w['b_gate'], tm)
    if gla_states == 1:
        shp = lambda a: a.reshape(s, t, a.shape[-1])
    else:
        rows = gla_states * t
        shp = lambda a: a.reshape(n // rows, rows, a.shape[-1])
    o_gla, st_new = _gla_call(shp(qk), shp(v), shp(la), shp(r), w['g_gla_out'], s_gla0, t if gla_states > 1 else GLA_CHUNK_ROWS,
                              gla_tile, gla_states)
    o_s5, xre, xim = _s5_call(us5.reshape(s, t, S5_WIDTH), x0re, x0im, w['s5_are'], w['s5_aim'], w['s5_bre'],
                              w['s5_bim'], w['s5_c'], w['s5_d'], w['s5_wglu'], w['s5_bglu'], s5_t_tile)
    if t >= tm:
        y = _tail_call(h.reshape(s, t, D_MODEL), o_gla.reshape(s, t, GLA_WIDTH), o_s5, mem_k3, mem_v3, w, tm)
        return y, st_new, xre, xim
    o_s5 = o_s5.reshape(n, S5_WIDTH)
    h2, q = _mid_call(h, o_gla.reshape(n, GLA_WIDTH), o_s5, w['w_out'], w['g_cross'], w['w_cq'], tm)
    o_att = _attn_call(q.reshape(s, t, D_MODEL), mem_k3, mem_v3, attn_seqs, attn_tq)
    y = _post_call(h2, o_att.reshape(n, D_MODEL), w['w_co'], w['g_ffn2'], w['w_ffn2_gu'], w['w_ffn2_down'],
                   w['g_final'], tm)
    return y.reshape(s, t, D_MODEL), st_new, xre, xim


def kernel(x_prompt, x_sample, state_gla, state_s5_re, state_s5_im, cache_mem_k, cache_mem_v, mem_prompt, g_ffn1, w_ffn1_gu, w_ffn1_down, g_mix, w_in, w_gla_gate, b_gla_gate, g_gla_out, s5_lambda_re, s5_lambda_im, s5_log_dt, s5_b_re, s5_b_im, s5_c_re, s5_c_im, s5_d, s5_w_glu, s5_b_glu, w_out, g_cross, g_mem, w_cq, w_ckv, w_co, g_ffn2, w_ffn2_gu, w_ffn2_down, g_final):
    bp, tp, _ = x_prompt.shape
    bs, ts, _ = x_sample.shape
    l = 0
    row = lambda a: a.reshape(1, -1).astype(F32)
    bf = lambda a: a.astype(BF16)

    are, aim, bbre, bbim = _s5_disc_call(
        s5_lambda_re[l], s5_lambda_im[l], s5_log_dt[l].reshape(S5_GROUPS, 1),
        jnp.transpose(s5_b_re[l], (0, 2, 1)), jnp.transpose(s5_b_im[l], (0, 2, 1)))
    c_re_t = jnp.transpose(s5_c_re[l], (0, 2, 1))
    c_im_t = jnp.transpose(s5_c_im[l], (0, 2, 1))
    cre_h = _halves_block_diag(c_re_t)
    cim_h = _halves_block_diag(-c_im_t)
    w_in_l = w_in[l]
    zg_lo = 2 * GLA_QK_WIDTH + GLA_WIDTH
    zg_hi = zg_lo + GLA_GATE_RANK
    w = {
        'g_ffn1': row(g_ffn1[l]), 'w_ffn1_gu': bf(w_ffn1_gu[l]), 'w_ffn1_down': bf(w_ffn1_down[l]),
        'g_mix': row(g_mix[l]),
        'w_in_main': bf(jnp.concatenate([w_in_l[:, :zg_lo], w_in_l[:, zg_hi:]], axis=1)),
        'w_zg': bf(jnp.pad(w_in_l[:, zg_lo:zg_hi], ((0, 0), (0, LANES - GLA_GATE_RANK)))),
        'w_gate': bf(jnp.pad(w_gla_gate[l], ((0, LANES - GLA_GATE_RANK), (0, 0)))),
        'b_gate': row(b_gla_gate[l]),
        'g_gla_out': row(g_gla_out[l]),
        's5_are': are.reshape(1, S5_STATE), 's5_aim': aim.reshape(1, S5_STATE),
        's5_bre': bf(_halves_block_diag(bbre)), 's5_bim': bf(_halves_block_diag(bbim)),
        's5_c': bf(jnp.concatenate([cre_h, cim_h], axis=1)),
        's5_d': row(s5_d[l]), 's5_wglu': bf(_block_diag(s5_w_glu[l])), 's5_bglu': row(s5_b_glu[l]),
        'w_out': bf(w_out[l]), 'g_cross': row(g_cross[l]), 'w_cq': bf(w_cq[l]), 'w_co': bf(w_co[l]),
        'g_ffn2': row(g_ffn2[l]), 'w_ffn2_gu': bf(w_ffn2_gu[l]), 'w_ffn2_down': bf(w_ffn2_down[l]),
        'g_final': row(g_final),
    }

    mk, mv, mk4, mv4 = _memkv_call(mem_prompt.reshape(bp * MEM_LEN, D_MODEL), row(g_mem[l]), bf(w_ckv[l]), 512)
    zeros_state = jnp.zeros((bp, S5_STATE), F32)
    y_p, st_p, re_p, im_p = _trunk_group(
        x_prompt, None, zeros_state, zeros_state, mk.reshape(bp, MEM_LEN, D_MODEL), mv.reshape(bp, MEM_LEN, D_MODEL),
        w, tm=512, gla_tile=256, gla_states=1, s5_t_tile=128, attn_seqs=1, attn_tq=512)

    seqs_per_chunk = GLA_CHUNK_ROWS // ts
    s0_t = jnp.transpose(state_gla[l], (0, 3, 1, 2)).reshape(bs, GLA_DV, GLA_QK_WIDTH)
    y_s, st_s, re_s, im_s = _trunk_group(
        x_sample, s0_t, state_s5_re[l].reshape(bs, S5_STATE), state_s5_im[l].reshape(bs, S5_STATE),
        cache_mem_k.reshape(DEPTH * bs, MEM_LEN, CROSS_HEADS, CROSS_DH)[l * bs:(l + 1) * bs],
        cache_mem_v.reshape(DEPTH * bs, MEM_LEN, CROSS_HEADS, CROSS_DH)[l * bs:(l + 1) * bs],
        w, tm=512, gla_tile=GLA_CHUNK_ROWS, gla_states=seqs_per_chunk, s5_t_tile=ts, attn_seqs=4, attn_tq=ts)

    def gla_state_out(st, nb):
        return jnp.transpose(st.reshape(nb, GLA_DV, GLA_HEADS, GLA_DK), (0, 2, 3, 1))[None]

    s5_out = lambda a, nb: a.reshape(1, nb, S5_GROUPS, S5_N)
    kv_out = lambda a: a.reshape(1, bp, MEM_LEN, CROSS_HEADS, CROSS_DH)
    return (y_p, y_s,
            gla_state_out(st_p, bp), s5_out(re_p, bp), s5_out(im_p, bp), kv_out(mk4), kv_out(mv4),
            gla_state_out(st_s, bs), s5_out(re_s, bs), s5_out(im_s, bs))
```

```python
import functools
import math

import numpy as np
import jax
import jax.numpy as jnp
from jax import lax
from jax.experimental import pallas as pl
from jax.experimental.pallas import tpu as pltpu

F32 = jnp.float32
BF16 = jnp.bfloat16

D_MODEL = 1024
DEPTH = 1
EPS = 1e-6
D_FF = 2816
GLA_WIDTH = 512
GLA_HEADS = 4
GLA_DV = 128
GLA_DK = 64
GLA_QK_WIDTH = 256
GLA_GATE_RANK = 16
GLA_TAU = 16.0
S5_WIDTH = 512
S5_P = 16
S5_GROUPS = 32
S5_N = 64
S5_STATE = S5_GROUPS * S5_N
CROSS_HEADS = 4
CROSS_DH = 256
MEM_LEN = 256

LANES = 128
SUBLANES = 8
MXU_DIM = 256
VMEM_LIMIT_BYTES = 56 * 1024 * 1024

FF_CHUNK = MXU_DIM
GLA_CHUNK_ROWS = 64
S5_LANE_CHUNK = 512


def _cparams(n_grid_axes):
    return pltpu.CompilerParams(
        dimension_semantics=("arbitrary",) * n_grid_axes,
        vmem_limit_bytes=VMEM_LIMIT_BYTES,
    )


def _const_spec(shape):
    zeros = (0,) * len(shape)
    return pl.BlockSpec(shape, lambda *_: zeros, pipeline_mode=pl.Buffered(1))


def _dot(a, b):
    return jnp.dot(a, b, preferred_element_type=F32)


def _dot_nt(a, b):
    return lax.dot_general(a, b, (((1,), (1,)), ((), ())), preferred_element_type=F32)


def _dot_tn(a, b):
    return lax.dot_general(a, b, (((0,), (0,)), ((), ())), preferred_element_type=F32)


def _rms(x, g):
    return x * lax.rsqrt(jnp.mean(x * x, axis=-1, keepdims=True) + EPS) * g


def _swiglu(xn_bf, wgu_ref, wdn_ref, between=None):
    acc = None
    for c in range(D_FF // FF_CHUNK):
        lo, hi = c * FF_CHUNK, (c + 1) * FF_CHUNK
        g = _dot(xn_bf, wgu_ref[:, lo:hi])
        u = _dot(xn_bf, wgu_ref[:, D_FF + lo:D_FF + hi])
        a = (jax.nn.silu(g) * u).astype(BF16)
        d = _dot(a, wdn_ref[lo:hi, :])
        acc = d if acc is None else acc + d
        if between is not None:
            between()
    return acc


def _log_sigmoid(z):
    return jnp.minimum(z, 0.0) - jnp.log1p(jnp.exp(-jnp.abs(z)))


def _pre_kernel(x_ref, g1_ref, wgu_ref, wdn_ref, gm_ref, win_ref, wzg_ref, wgate_ref, bgate_ref,
                h_ref, qk_ref, v_ref, la_ref, r_ref, us5_ref):
    x = x_ref[...]
    xn = _rms(x, g1_ref[...]).astype(BF16)
    h = x + 0.5 * _swiglu(xn, wgu_ref, wdn_ref)
    h_ref[...] = h
    un = _rms(h, gm_ref[...]).astype(BF16)
    p = _dot(un, win_ref[...])
    qk_ref[...] = p[:, 0:512]
    v_ref[...] = p[:, 512:1024]
    r_ref[...] = p[:, 1024:1536]
    us5_ref[...] = p[:, 1536:2048]
    zg = _dot(un, wzg_ref[...])
    z = _dot(zg.astype(BF16), wgate_ref[...]) + bgate_ref[...]
    la_ref[...] = _log_sigmoid(z) * (1.0 / GLA_TAU)


def _pre_call(x2d, g1, wgu, wdn, gm, win, wzg, wgate, bgate, tm):
    n = x2d.shape[0]
    row = lambda w: pl.BlockSpec((tm, w), lambda i: (i, 0))
    out_shapes = [jax.ShapeDtypeStruct((n, w), F32) for w in (D_MODEL, 512, 512, 256, 512, 512)]
    return pl.pallas_call(
        _pre_kernel,
        grid=(n // tm,),
        in_specs=[row(D_MODEL), _const_spec(g1.shape), _const_spec(wgu.shape), _const_spec(wdn.shape),
                  _const_spec(gm.shape), _const_spec(win.shape), _const_spec(wzg.shape),
                  _const_spec(wgate.shape), _const_spec(bgate.shape)],
        out_specs=[row(D_MODEL), row(512), row(512), row(256), row(512), row(512)],
        out_shape=out_shapes,
        compiler_params=_cparams(1),
        name="pre_ffn_proj",
    )(x2d, g1, wgu, wdn, gm, win, wzg, wgate, bgate)


def _memkv_kernel(m_ref, g_ref, w_ref, k_ref, v_ref, k4_ref, v4_ref):
    mn = _rms(m_ref[...], g_ref[...]).astype(BF16)
    kv = _dot(mn, w_ref[...])
    k_ref[...] = kv[:, :D_MODEL]
    v_ref[...] = kv[:, D_MODEL:]
    for hd in range(CROSS_HEADS):
        k4_ref[:, hd, :] = kv[:, hd * CROSS_DH:(hd + 1) * CROSS_DH]
        v4_ref[:, hd, :] = kv[:, D_MODEL + hd * CROSS_DH:D_MODEL + (hd + 1) * CROSS_DH]


def _memkv_call(mem2d, g, w, tm):
    n = mem2d.shape[0]
    row = pl.BlockSpec((tm, D_MODEL), lambda i: (i, 0))
    row4 = pl.BlockSpec((tm, CROSS_HEADS, CROSS_DH), lambda i: (i, 0, 0))
    return pl.pallas_call(
        _memkv_kernel,
        grid=(n // tm,),
        in_specs=[row, _const_spec(g.shape), _const_spec(w.shape)],
        out_specs=[row, row, row4, row4],
        out_shape=[jax.ShapeDtypeStruct((n, D_MODEL), F32)] * 2
        + [jax.ShapeDtypeStruct((n, CROSS_HEADS, CROSS_DH), F32)] * 2,
        compiler_params=_cparams(1),
        name="memory_kv",
    )(mem2d, g, w)


def _mid_kernel(h_ref, og_ref, os_ref, wo_ref, gc_ref, wq_ref, h2_ref, q_ref):
    h2 = (h_ref[...] + _dot(og_ref[...].astype(BF16), wo_ref[0:GLA_WIDTH, :])
          + _dot(os_ref[...].astype(BF16), wo_ref[GLA_WIDTH:, :]))
    h2_ref[...] = h2
    hn = _rms(h2, gc_ref[...]).astype(BF16)
    q_ref[...] = _dot(hn, wq_ref[...])


def _mid_call(h, og, os_, wo, gc, wq, tm):
    n = h.shape[0]
    row = lambda w: pl.BlockSpec((tm, w), lambda i: (i, 0))
    return pl.pallas_call(
        _mid_kernel,
        grid=(n // tm,),
        in_specs=[row(D_MODEL), row(512), row(512), _const_spec(wo.shape), _const_spec(gc.shape),
                  _const_spec(wq.shape)],
        out_specs=[row(D_MODEL), row(D_MODEL)],
        out_shape=[jax.ShapeDtypeStruct((n, D_MODEL), F32)] * 2,
        compiler_params=_cparams(1),
        name="mix_out_cross_q",
    )(h, og, os_, wo, gc, wq)


def _post_kernel(h2_ref, o_ref, wco_ref, g2_ref, wgu_ref, wdn_ref, gf_ref, y_ref):
    h3 = h2_ref[...] + _dot(o_ref[...].astype(BF16), wco_ref[...])
    hn = _rms(h3, g2_ref[...]).astype(BF16)
    h4 = h3 + 0.5 * _swiglu(hn, wgu_ref, wdn_ref)
    y_ref[...] = _rms(h4, gf_ref[...])


def _post_call(h2, o, wco, g2, wgu, wdn, gf, tm):
    n = h2.shape[0]
    row = pl.BlockSpec((tm, D_MODEL), lambda i: (i, 0))
    return pl.pallas_call(
        _post_kernel,
        grid=(n // tm,),
        in_specs=[row, row, _const_spec(wco.shape), _const_spec(g2.shape), _const_spec(wgu.shape),
                  _const_spec(wdn.shape), _const_spec(gf.shape)],
        out_specs=row,
        out_shape=jax.ShapeDtypeStruct((n, D_MODEL), F32),
        compiler_params=_cparams(1),
        name="post_ffn_final",
    )(h2, o, wco, g2, wgu, wdn, gf)


def _tail_kernel(h_ref, og_ref, os_ref, k_ref, v_ref, wo_ref, gc_ref, wq_ref, wco_ref, g2_ref, wgu_ref,
                 wdn_ref, gf_ref, y_ref):
    h2 = (h_ref[0] + _dot(og_ref[0].astype(BF16), wo_ref[0:GLA_WIDTH, :])
          + _dot(os_ref[0].astype(BF16), wo_ref[GLA_WIDTH:, :]))
    q = _dot(_rms(h2, gc_ref[...]).astype(BF16), wq_ref[...]).astype(BF16)
    outs = []
    for hd in range(CROSS_HEADS):
        sl = slice(hd * CROSS_DH, (hd + 1) * CROSS_DH)
        s = _dot_nt(q[:, sl], k_ref[0, :, sl].astype(BF16)) * (CROSS_DH ** -0.5)
        outs.append(_dot(_softmax(s).astype(BF16), v_ref[0, :, sl].astype(BF16)).astype(BF16))
    h3 = h2 + _dot(jnp.concatenate(outs, axis=-1), wco_ref[...])
    hn = _rms(h3, g2_ref[...]).astype(BF16)
    h4 = h3 + 0.5 * _swiglu(hn, wgu_ref, wdn_ref)
    y_ref[0] = _rms(h4, gf_ref[...])


def _tail_call(h3d, og3, os3, k3, v3, w, tm):
    s, t, _ = h3d.shape
    row = lambda wd: pl.BlockSpec((1, tm, wd), lambda i, j: (i, j, 0))
    kvspec = pl.BlockSpec((1, MEM_LEN, D_MODEL), lambda i, j: (i, 0, 0))
    consts = [w['w_out'], w['g_cross'], w['w_cq'], w['w_co'], w['g_ffn2'], w['w_ffn2_gu'], w['w_ffn2_down'],
              w['g_final']]
    return pl.pallas_call(
        _tail_kernel,
        grid=(s, t // tm),
        in_specs=[row(D_MODEL), row(GLA_WIDTH), row(S5_WIDTH), kvspec, kvspec] + [_const_spec(a.shape) for a in consts],
        out_specs=row(D_MODEL),
        out_shape=jax.ShapeDtypeStruct((s, t, D_MODEL), F32),
        compiler_params=_cparams(2),
        name="tail_attn_ffn",
    )(h3d, og3, os3, k3, v3, *consts)


def _softmax(s):
    e = jnp.exp(s - jnp.max(s, axis=-1, keepdims=True))
    return e / jnp.sum(e, axis=-1, keepdims=True)


def _load_mem(ref, j):
    if len(ref.shape) == 3:
        return ref[j].astype(BF16)
    return jnp.concatenate([ref[j, :, hd, :] for hd in range(CROSS_HEADS)], axis=-1).astype(BF16)


def _attn_packed_kernel(q_ref, k_ref, v_ref, o_ref, *, n_seq):
    t = q_ref.shape[1]
    shape = (CROSS_HEADS * t, D_MODEL)
    q_mask = (_iota2(shape, 0) // t) == (_iota2(shape, 1) // CROSS_DH)
    o_lane_head = _iota2((t, D_MODEL), 1) // CROSS_DH
    scores = []
    for j in range(n_seq):
        q_stack = jnp.where(q_mask, jnp.concatenate([q_ref[j]] * CROSS_HEADS, axis=0), 0.0).astype(BF16)
        scores.append(_dot_nt(q_stack, _load_mem(k_ref, j)) * (CROSS_DH ** -0.5))
    probs = [_softmax(s).astype(BF16) for s in scores]
    for j in range(n_seq):
        full = _dot(probs[j], _load_mem(v_ref, j))
        o = jnp.zeros((t, D_MODEL), F32)
        for hd in range(CROSS_HEADS):
            o = jnp.where(o_lane_head == hd, full[hd * t:(hd + 1) * t, :], o)
        o_ref[j] = o.astype(o_ref.dtype)


def _attn_kernel(q_ref, k_ref, v_ref, o_ref, *, n_seq):
    cast = lambda a: a.astype(BF16)
    for j in range(n_seq):
        q = q_ref[j]
        outs = []
        for hd in range(CROSS_HEADS):
            sl = slice(hd * CROSS_DH, (hd + 1) * CROSS_DH)
            s = _dot_nt(cast(q[:, sl]), cast(k_ref[j, :, sl])) * (CROSS_DH ** -0.5)
            p = _softmax(s)
            outs.append(_dot(cast(p), cast(v_ref[j, :, sl])))
        o_ref[j] = jnp.concatenate(outs, axis=-1).astype(o_ref.dtype)


def _attn_call(q3, k3, v3, n_seq, tq):
    s, t, _ = q3.shape
    qspec = pl.BlockSpec((n_seq, tq, D_MODEL), lambda i, j: (i, j, 0))
    if k3.ndim == 3:
        kvspec = pl.BlockSpec((n_seq, MEM_LEN, D_MODEL), lambda i, j: (i, 0, 0))
    else:
        kvspec = pl.BlockSpec((n_seq, MEM_LEN, CROSS_HEADS, CROSS_DH), lambda i, j: (i, 0, 0, 0))
    return pl.pallas_call(
        functools.partial(_attn_packed_kernel if tq * CROSS_HEADS <= MXU_DIM else _attn_kernel, n_seq=n_seq),
        grid=(s // n_seq, t // tq),
        in_specs=[qspec, kvspec, kvspec],
        out_specs=qspec,
        out_shape=jax.ShapeDtypeStruct(q3.shape, BF16 if tq % 16 == 0 else F32),
        compiler_params=_cparams(2),
        name="cross_attention",
    )(q3, k3, v3)


def _split3(x):
    hi = x.astype(BF16)
    r1 = x - hi.astype(F32)
    mid = r1.astype(BF16)
    lo = (r1 - mid.astype(F32)).astype(BF16)
    return hi, mid, lo


def _dot01(m01, x):
    hi, mid, lo = _split3(x)
    return _dot(m01, hi) + _dot(m01, mid) + _dot(m01, lo)


def _iota2(shape, dim):
    return lax.broadcasted_iota(jnp.int32, shape, dim)


def _gla_stages(qk_ref, v_ref, la_ref, r_ref, o_ref, g_ref, tri_ref, lvl_ref, khm_ref, vhm_ref, shm_ref,
                states, put_states, *, seq_len, n_chunks):
    c_rows = GLA_CHUNK_ROWS
    n_sub = c_rows // seq_len
    log2 = lambda n: int(math.log2(n))
    tri = tri_ref[...]
    lvl = lvl_ref[...]
    rowq = _iota2((c_rows, GLA_QK_WIDTH), 0)
    u_head = [(_iota2((GLA_DV, GLA_QK_WIDTH), 1) >> log2(GLA_DK)) == hd for hd in range(GLA_HEADS)]

    def head_blocks(x, mask_ref):
        return jnp.concatenate([x.astype(BF16)] * GLA_HEADS, axis=0) * mask_ref[...]

    g_out = g_ref[...]
    chunks = range(n_chunks)
    rows_of = lambda c: slice(c * c_rows, (c + 1) * c_rows)
    sub_of = lambda j: slice(j * seq_len, (j + 1) * seq_len)
    q = [qk_ref[rows_of(c), 0:GLA_QK_WIDTH] * (GLA_DK ** -0.5) for c in chunks]
    k = [qk_ref[rows_of(c), GLA_QK_WIDTH:2 * GLA_QK_WIDTH] for c in chunks]
    v = [v_ref[rows_of(c), :] for c in chunks]
    b = [_dot01(tri, la_ref[rows_of(c), :]) for c in chunks]
    yield

    a_mat = [jnp.where(lvl == 0, _dot_nt(q[c].astype(BF16), head_blocks(k[c], khm_ref)), 0.0) for c in chunks]
    last = list(b)
    m = 1
    while m < seq_len:
        yield
        even = ((rowq >> log2(m)) & 1) == 0
        for c in chunks:
            prev_last = pltpu.roll(last[c], m, axis=0)
            e = jnp.exp(jnp.minimum(jnp.where(even, last[c] - b[c], b[c] - prev_last), 0.0))
            a_lvl = _dot_nt((q[c] * e).astype(BF16), head_blocks(k[c] * e, khm_ref))
            a_mat[c] = jnp.where(lvl == log2(m) + 1, a_lvl, a_mat[c])
            last[c] = jnp.where(even, pltpu.roll(last[c], c_rows - m, axis=0), last[c])
        m *= 2
    yield
    o_intra = [_dot(a_mat[c].astype(BF16), head_blocks(v[c], vhm_ref)) for c in chunks]
    yield

    narrow = (lambda a: a.astype(BF16)) if seq_len % 16 == 0 else (lambda a: a)
    q_dec = [narrow(q[c] * jnp.exp(b[c])) for c in chunks]
    k_dec = [narrow(k[c] * jnp.exp(jnp.minimum(last[c] - b[c], 0.0))) for c in chunks]
    u_t = {}
    for c in chunks:
        vc = narrow(v[c])
        for j in range(n_sub):
            u_full = _dot_tn(vc[sub_of(j)], k_dec[c][sub_of(j)])
            acc = jnp.zeros((GLA_DV, GLA_QK_WIDTH), F32)
            for hd in range(GLA_HEADS):
                acc = jnp.where(u_head[hd], u_full[hd * GLA_DV:(hd + 1) * GLA_DV, :], acc)
            u_t[c, j] = acc

    states = list(states)
    for c in chunks:
        if c % 2 == 0:
            yield
        o_parts = []
        for j in range(n_sub):
            o_parts.append(_dot_nt(q_dec[c][sub_of(j)], head_blocks(states[j], shm_ref)))
            decay = jnp.exp(last[c][j * seq_len:j * seq_len + 1, :])
            states[j] = states[j] * decay + u_t[c, j]
        o_inter = o_parts[0] if n_sub == 1 else jnp.concatenate(o_parts, axis=0)
        o = o_intra[c] + o_inter
        r = r_ref[rows_of(c), :]
        outs = []
        for hd in range(GLA_HEADS):
            sl = slice(hd * GLA_DV, (hd + 1) * GLA_DV)
            outs.append(_rms(o[:, sl], g_out) * jax.nn.silu(r[:, sl]))
        o_ref[rows_of(c), :] = jnp.concatenate(outs, axis=-1).astype(o_ref.dtype)
    put_states(states)


def _gla_kernel(*refs, seq_len, n_chunks, n_state, has_s0):
    ins, rest = refs[:10], refs[10:]
    qk_ref, v_ref, la_ref, r_ref, g_ref = ins[:5]
    if has_s0:
        s0_ref, o_ref, st_ref, s_scr = rest
    else:
        o_ref, st_ref, s_scr = rest
        s0_ref = None
    t_idx = pl.program_id(1)

    @pl.when(t_idx == 0)
    def _():
        if has_s0:
            s_scr[...] = s0_ref[...]
        else:
            s_scr[...] = jnp.zeros_like(s_scr)

    def put_states(states):
        for j, s_t in enumerate(states):
            s_scr[j] = s_t

    for _ in _gla_stages(qk_ref.at[0], v_ref.at[0], la_ref.at[0], r_ref.at[0], o_ref.at[0], g_ref, *ins[5:],
                         [s_scr[j] for j in range(n_state)], put_states, seq_len=seq_len, n_chunks=n_chunks):
        pass

    @pl.when(t_idx == pl.num_programs(1) - 1)
    def _():
        st_ref[...] = s_scr[...]


def _gla_index_constants(seq_len):
    c = GLA_CHUNK_ROWS
    i = np.arange(c)[:, None]
    s = np.arange(c)[None, :]
    same_seq = (i // seq_len) == (s // seq_len)
    tri = (same_seq & (s <= i)).astype(np.float32)
    top_bit = np.floor(np.log2(np.maximum(i ^ s, 1))).astype(np.int32)
    lvl = np.where(same_seq & (s < i), top_bit + 1, -1)
    lvl = np.where(i == s, 0, lvl).astype(np.int32)
    head_mask = lambda rows_per_head, lanes_per_head: (
        (np.arange(GLA_HEADS * rows_per_head)[:, None] // rows_per_head)
        == (np.arange(GLA_HEADS * lanes_per_head)[None, :] // lanes_per_head)).astype(np.float32)
    return [jnp.asarray(tri, BF16), jnp.asarray(np.tile(lvl, (1, GLA_HEADS))),
            jnp.asarray(head_mask(c, GLA_DK), BF16), jnp.asarray(head_mask(c, GLA_DV), BF16),
            jnp.asarray(head_mask(GLA_DV, GLA_DK), BF16)]


def _gla_call(qk3, v3, la3, r3, g_out, s0_t, seq_len, tile_rows, n_state):
    n_outer, rows, _ = qk3.shape
    n_chunks = tile_rows // GLA_CHUNK_ROWS
    spec = lambda w: pl.BlockSpec((1, tile_rows, w), lambda i, t: (i, t, 0))
    st_spec = pl.BlockSpec((n_state, GLA_DV, GLA_QK_WIDTH), lambda i, t: (i, 0, 0))
    has_s0 = s0_t is not None
    consts = [g_out] + _gla_index_constants(seq_len)
    in_specs = [spec(512), spec(512), spec(256), spec(512)] + [_const_spec(a.shape) for a in consts]
    args = [qk3, v3, la3, r3] + consts
    if has_s0:
        in_specs.append(st_spec)
        args.append(s0_t)
    return pl.pallas_call(
        functools.partial(_gla_kernel, seq_len=seq_len, n_chunks=n_chunks, n_state=n_state, has_s0=has_s0),
        grid=(n_outer, rows // tile_rows),
        in_specs=in_specs,
        out_specs=[spec(512), st_spec],
        out_shape=[jax.ShapeDtypeStruct((n_outer, rows, GLA_WIDTH), BF16),
                   jax.ShapeDtypeStruct((n_outer * n_state, GLA_DV, GLA_QK_WIDTH), F32)],
        scratch_shapes=[pltpu.VMEM((n_state, GLA_DV, GLA_QK_WIDTH), F32)],
        compiler_params=_cparams(2),
        name="gla_mixer",
    )(*args)


def _pre_gla_kernel(x_ref, g1_ref, wgu_ref, wdn_ref, gm_ref, win_ref, wzg_ref, wgate_ref, bgate_ref, gout_ref,
                    tri_ref, lvl_ref, khm_ref, vhm_ref, shm_ref,
                    h_ref, us5_ref, og_ref, st_ref,
                    qk_scr, v_scr, la_scr, r_scr, s_scr, *, tiles_per_seq):
    i = pl.program_id(0)
    last_step = pl.num_programs(0) - 1
    n_chunks = x_ref.shape[0] // GLA_CHUNK_ROWS

    @pl.when(i == 0)
    def _():
        for scr in (qk_scr, v_scr, la_scr, r_scr, s_scr):
            scr[...] = jnp.zeros_like(scr)

    def gla_stages():
        first_of_seq = lax.rem(i + (tiles_per_seq - 1), tiles_per_seq) == 0
        state = jnp.where(first_of_seq, 0.0, s_scr[0])

        def put_states(states):
            s_scr[0] = states[0]
            st_ref[0] = states[0]

        return _gla_stages(qk_scr, v_scr, la_scr, r_scr, og_ref, gout_ref, tri_ref, lvl_ref, khm_ref, vhm_ref,
                           shm_ref, [state], put_states, seq_len=GLA_CHUNK_ROWS, n_chunks=n_chunks)

    @pl.when(i < last_step)
    def _():
        stages = gla_stages()
        next(stages)
        x = x_ref[...]
        xn = _rms(x, g1_ref[...]).astype(BF16)
        h = x + 0.5 * _swiglu(xn, wgu_ref, wdn_ref, between=lambda: next(stages, None))
        h_ref[...] = h
        un = _rms(h, gm_ref[...]).astype(BF16)
        p = []
        for grp in range(4):
            p.append(_dot(un, win_ref[:, grp * 512:(grp + 1) * 512]))
            next(stages, None)
        zg = _dot(un, wzg_ref[...])
        z = _dot(zg.astype(BF16), wgate_ref[...]) + bgate_ref[...]
        for _ in stages:
            pass
        qk_scr[...] = p[0]
        v_scr[...] = p[1]
        r_scr[...] = p[2]
        us5_ref[...] = p[3]
        la_scr[...] = _log_sigmoid(z) * (1.0 / GLA_TAU)

    @pl.when(i == last_step)
    def _():
        for _ in gla_stages():
            pass


def _pre_gla_call(x2d, w, tm, seq_rows):
    n = x2d.shape[0]
    n_tiles = n // tm
    tiles_per_seq = seq_rows // tm
    cur = lambda wd: pl.BlockSpec((tm, wd), lambda i: (jnp.minimum(i, n_tiles - 1), 0))
    prev = lambda wd: pl.BlockSpec((tm, wd), lambda i: (jnp.maximum(i - 1, 0), 0))
    st_spec = pl.BlockSpec((1, GLA_DV, GLA_QK_WIDTH), lambda i: (jnp.maximum(i - 1, 0) // tiles_per_seq, 0, 0))
    consts = [w['g_ffn1'], w['w_ffn1_gu'], w['w_ffn1_down'], w['g_mix'], w['w_in_main'], w['w_zg'], w['w_gate'],
              w['b_gate'], w['g_gla_out']] + _gla_index_constants(GLA_CHUNK_ROWS)
    return pl.pallas_call(
        functools.partial(_pre_gla_kernel, tiles_per_seq=tiles_per_seq),
        grid=(n_tiles + 1,),
        in_specs=[cur(D_MODEL)] + [_const_spec(a.shape) for a in consts],
        out_specs=[cur(D_MODEL), cur(S5_WIDTH), prev(GLA_WIDTH), st_spec],
        out_shape=[jax.ShapeDtypeStruct((n, D_MODEL), F32), jax.ShapeDtypeStruct((n, S5_WIDTH), F32),
                   jax.ShapeDtypeStruct((n, GLA_WIDTH), BF16),
                   jax.ShapeDtypeStruct((n // seq_rows, GLA_DV, GLA_QK_WIDTH), F32)],
        scratch_shapes=[pltpu.VMEM((tm, 2 * GLA_QK_WIDTH), F32), pltpu.VMEM((tm, GLA_WIDTH), F32),
                        pltpu.VMEM((tm, GLA_QK_WIDTH), F32), pltpu.VMEM((tm, GLA_WIDTH), F32),
                        pltpu.VMEM((1, GLA_DV, GLA_QK_WIDTH), F32)],
        compiler_params=_cparams(1),
        name="pre_ffn_gla",
    )(x2d, *consts)


def _s5_disc_kernel(lre_ref, lim_ref, ldt_ref, bre_ref, bim_ref, are_ref, aim_ref, bbre_ref, bbim_ref):
    lr = lre_ref[...]
    li = lim_ref[...]
    dt = jnp.exp(ldt_ref[...])
    mag = jnp.exp(dt * lr)
    ab_re = mag * jnp.cos(dt * li)
    ab_im = mag * jnp.sin(dt * li)
    den = lr * lr + li * li
    coef_re = ((ab_re - 1.0) * lr + ab_im * li) / den
    coef_im = (ab_im * lr - (ab_re - 1.0) * li) / den
    are_ref[...] = ab_re
    aim_ref[...] = ab_im
    br = bre_ref[...]
    bi = bim_ref[...]
    cr = coef_re[:, None, :]
    cim = coef_im[:, None, :]
    bbre_ref[...] = cr * br - cim * bi
    bbim_ref[...] = cr * bi + cim * br


def _s5_disc_call(lre, lim, ldt, bre_t, bim_t):
    gn = jax.ShapeDtypeStruct((S5_GROUPS, S5_N), F32)
    gpn = jax.ShapeDtypeStruct((S5_GROUPS, S5_P, S5_N), F32)
    return pl.pallas_call(
        _s5_disc_kernel,
        out_shape=[gn, gn, gpn, gpn],
        name="s5_discretise",
    )(lre, lim, ldt, bre_t, bim_t)


def _to_time_major(u_ref, perm_scr, n_seq, t_tile):
    if n_seq == SUBLANES:
        n_slab = S5_WIDTH // LANES
        for j in range(n_seq):
            for sb in range(n_slab):
                perm_scr[sb, pl.ds(j, t_tile, stride=n_seq), :] = u_ref[j, :, sb * LANES:(sb + 1) * LANES]
        return jnp.concatenate([perm_scr[sb] for sb in range(n_slab)], axis=-1)
    return jnp.concatenate([u_ref[:, t, :] for t in range(t_tile)], axis=0)


def _from_time_major(y, o_ref, perm_scr, n_seq, t_tile):
    if n_seq == SUBLANES:
        n_slab = S5_WIDTH // LANES
        for sb in range(n_slab):
            perm_scr[sb] = y[:, sb * LANES:(sb + 1) * LANES]
        for j in range(n_seq):
            for sb in range(n_slab):
                o_ref[j, :, sb * LANES:(sb + 1) * LANES] = perm_scr[sb, pl.ds(j, t_tile, stride=n_seq), :]
    else:
        for t in range(t_tile):
            o_ref[:, t, :] = y[t * n_seq:(t + 1) * n_seq, :]


def _s5_kernel(u_ref, x0re_ref, x0im_ref, are_ref, aim_ref, bre_ref, bim_ref, c_ref, d_ref, wglu_ref,
               bglu_ref, o_ref, xre_ref, xim_ref, sre_scr, sim_scr, xre_scr, xim_scr, perm_scr, *, n_seq, t_tile):
    half_w = S5_WIDTH // 2
    half_s = S5_STATE // 2
    rows = t_tile * n_seq

    @pl.when(pl.program_id(0) == 0)
    def _():
        xre_scr[...] = x0re_ref[...]
        xim_scr[...] = x0im_ref[...]

    u = _to_time_major(u_ref, perm_scr, n_seq, t_tile)
    ub = u.astype(BF16)
    for hf in range(2):
        uh = ub[:, hf * half_w:(hf + 1) * half_w]
        sre_scr[:, :, hf * half_s:(hf + 1) * half_s] = _dot(uh, bre_ref[hf]).reshape(t_tile, n_seq, half_s)
        sim_scr[:, :, hf * half_s:(hf + 1) * half_s] = _dot(uh, bim_ref[hf]).reshape(t_tile, n_seq, half_s)

    for lc in range(S5_STATE // S5_LANE_CHUNK):
        lanes = slice(lc * S5_LANE_CHUNK, (lc + 1) * S5_LANE_CHUNK)
        ar = jnp.broadcast_to(are_ref[:, lanes], (SUBLANES, S5_LANE_CHUNK))
        ai = jnp.broadcast_to(aim_ref[:, lanes], (SUBLANES, S5_LANE_CHUNK))
        for sc in range(n_seq // SUBLANES):
            srows = slice(sc * SUBLANES, (sc + 1) * SUBLANES)
            xr = xre_scr[srows, lanes]
            xi = xim_scr[srows, lanes]
            for t in range(t_tile):
                nxr = ar * xr - ai * xi + sre_scr[t, srows, lanes]
                nxi = ar * xi + ai * xr + sim_scr[t, srows, lanes]
                sre_scr[t, srows, lanes] = nxr
                sim_scr[t, srows, lanes] = nxi
                xr, xi = nxr, nxi
            xre_scr[srows, lanes] = xr
            xim_scr[srows, lanes] = xi

    xr_all = sre_scr[...].reshape(rows, S5_STATE).astype(BF16)
    xi_all = sim_scr[...].reshape(rows, S5_STATE).astype(BF16)
    ys = []
    for hf in range(2):
        st = slice(hf * half_s, (hf + 1) * half_s)
        ys.append(_dot(xr_all[:, st], c_ref[hf, 0:half_s, :]) + _dot(xi_all[:, st], c_ref[hf, half_s:, :]))
    y = jnp.concatenate(ys, axis=-1) + d_ref[...] * u
    z = jax.nn.gelu(y)
    gate = jax.nn.sigmoid(_dot(z.astype(BF16), wglu_ref[...]) + bglu_ref[...])
    _from_time_major(z * gate, o_ref, perm_scr, n_seq, t_tile)

    @pl.when(pl.program_id(0) == pl.num_programs(0) - 1)
    def _():
        xre_ref[...] = xre_scr[...]
        xim_ref[...] = xim_scr[...]


def _s5_call(u3, x0re, x0im, are, aim, bre, bim, cmat, dskip, wglu, bglu, t_tile):
    n_seq, t, _ = u3.shape
    uspec = pl.BlockSpec((n_seq, t_tile, S5_WIDTH), lambda i: (0, i, 0))
    perm_rows = t_tile * n_seq if n_seq == SUBLANES else SUBLANES
    xspec = pl.BlockSpec((n_seq, S5_STATE), lambda i: (0, 0))
    consts = [are, aim, bre, bim, cmat, dskip, wglu, bglu]
    return pl.pallas_call(
        functools.partial(_s5_kernel, n_seq=n_seq, t_tile=t_tile),
        grid=(t // t_tile,),
        in_specs=[uspec, xspec, xspec] + [_const_spec(a.shape) for a in consts],
        out_specs=[uspec, xspec, xspec],
        out_shape=[jax.ShapeDtypeStruct((n_seq, t, S5_WIDTH), F32),
                   jax.ShapeDtypeStruct((n_seq, S5_STATE), F32),
                   jax.ShapeDtypeStruct((n_seq, S5_STATE), F32)],
        scratch_shapes=[pltpu.VMEM((t_tile, n_seq, S5_STATE), F32), pltpu.VMEM((t_tile, n_seq, S5_STATE), F32),
                        pltpu.VMEM((n_seq, S5_STATE), F32), pltpu.VMEM((n_seq, S5_STATE), F32),
                        pltpu.VMEM((S5_WIDTH // LANES, perm_rows, LANES), F32)],
        compiler_params=_cparams(1),
        name="s5_mixer",
    )(u3, x0re, x0im, *consts)


def _block_diag(blocks):
    g, a, b = blocks.shape
    eye = jnp.eye(g, dtype=blocks.dtype)
    return (eye[:, None, :, None] * blocks[:, :, None, :]).reshape(g * a, g * b)


def _halves_block_diag(blocks):
    g = blocks.shape[0]
    return jnp.stack([_block_diag(blocks[:g // 2]), _block_diag(blocks[g // 2:])])


def _trunk_group(x3, s_gla0, x0re, x0im, mem_k3, mem_v3, w, *, tm, gla_tile, gla_states, s5_t_tile,
                 attn_seqs, attn_tq):
    s, t, _ = x3.shape
    n = s * t
    if t >= tm and s_gla0 is None:
        h, us5, o_gla, st_new = _pre_gla_call(x3.reshape(n, D_MODEL), w, tm, t)
    else:
        h, qk, v, la, r, us5 = _pre_call(x3.reshape(n, D_MODEL), w['g_ffn1'], w['w_ffn1_gu'], w['w_ffn1_down'],
                                         w['g_mix'], w['w_in_main'], w['w_zg'], w['w_gate'], w['b_gate'], tm)
        if gla_states == 1:
            shp = lambda a: a.reshape(s, t, a.shape[-1])
        else:
            rows = gla_states * t
            shp = lambda a: a.reshape(n // rows, rows, a.shape[-1])
        o_gla, st_new = _gla_call(shp(qk), shp(v), shp(la), shp(r), w['g_gla_out'], s_gla0,
                                  t if gla_states > 1 else GLA_CHUNK_ROWS, gla_tile, gla_states)
    o_s5, xre, xim = _s5_call(us5.reshape(s, t, S5_WIDTH), x0re, x0im, w['s5_are'], w['s5_aim'], w['s5_bre'],
                              w['s5_bim'], w['s5_c'], w['s5_d'], w['s5_wglu'], w['s5_bglu'], s5_t_tile)
    if t >= tm:
        y = _tail_call(h.reshape(s, t, D_MODEL), o_gla.reshape(s, t, GLA_WIDTH), o_s5, mem_k3, mem_v3, w, tm)
        return y, st_new, xre, xim
    o_s5 = o_s5.reshape(n, S5_WIDTH)
    h2, q = _mid_call(h, o_gla.reshape(n, GLA_WIDTH), o_s5, w['w_out'], w['g_cross'], w['w_cq'], tm)
    o_att = _attn_call(q.reshape(s, t, D_MODEL), mem_k3, mem_v3, attn_seqs, attn_tq)
    y = _post_call(h2, o_att.reshape(n, D_MODEL), w['w_co'], w['g_ffn2'], w['w_ffn2_gu'], w['w_ffn2_down'],
                   w['g_final'], tm)
    return y.reshape(s, t, D_MODEL), st_new, xre, xim


def kernel(x_prompt, x_sample, state_gla, state_s5_re, state_s5_im, cache_mem_k, cache_mem_v, mem_prompt, g_ffn1, w_ffn1_gu, w_ffn1_down, g_mix, w_in, w_gla_gate, b_gla_gate, g_gla_out, s5_lambda_re, s5_lambda_im, s5_log_dt, s5_b_re, s5_b_im, s5_c_re, s5_c_im, s5_d, s5_w_glu, s5_b_glu, w_out, g_cross, g_mem, w_cq, w_ckv, w_co, g_ffn2, w_ffn2_gu, w_ffn2_down, g_final):
    bp, tp, _ = x_prompt.shape
    bs, ts, _ = x_sample.shape
    l = 0
    row = lambda a: a.reshape(1, -1).astype(F32)
    bf = lambda a: a.astype(BF16)

    are, aim, bbre, bbim = _s5_disc_call(
        s5_lambda_re[l], s5_lambda_im[l], s5_log_dt[l].reshape(S5_GROUPS, 1),
        jnp.transpose(s5_b_re[l], (0, 2, 1)), jnp.transpose(s5_b_im[l], (0, 2, 1)))
    c_re_t = jnp.transpose(s5_c_re[l], (0, 2, 1))
    c_im_t = jnp.transpose(s5_c_im[l], (0, 2, 1))
    cre_h = _halves_block_diag(c_re_t)
    cim_h = _halves_block_diag(-c_im_t)
    w_in_l = w_in[l]
    zg_lo = 2 * GLA_QK_WIDTH + GLA_WIDTH
    zg_hi = zg_lo + GLA_GATE_RANK
    w = {
        'g_ffn1': row(g_ffn1[l]), 'w_ffn1_gu': bf(w_ffn1_gu[l]), 'w_ffn1_down': bf(w_ffn1_down[l]),
        'g_mix': row(g_mix[l]),
        'w_in_main': bf(jnp.concatenate([w_in_l[:, :zg_lo], w_in_l[:, zg_hi:]], axis=1)),
        'w_zg': bf(jnp.pad(w_in_l[:, zg_lo:zg_hi], ((0, 0), (0, LANES - GLA_GATE_RANK)))),
        'w_gate': bf(jnp.pad(w_gla_gate[l], ((0, LANES - GLA_GATE_RANK), (0, 0)))),
        'b_gate': row(b_gla_gate[l]),
        'g_gla_out': row(g_gla_out[l]),
        's5_are': are.reshape(1, S5_STATE), 's5_aim': aim.reshape(1, S5_STATE),
        's5_bre': bf(_halves_block_diag(bbre)), 's5_bim': bf(_halves_block_diag(bbim)),
        's5_c': bf(jnp.concatenate([cre_h, cim_h], axis=1)),
        's5_d': row(s5_d[l]), 's5_wglu': bf(_block_diag(s5_w_glu[l])), 's5_bglu': row(s5_b_glu[l]),
        'w_out': bf(w_out[l]), 'g_cross': row(g_cross[l]), 'w_cq': bf(w_cq[l]), 'w_co': bf(w_co[l]),
        'g_ffn2': row(g_ffn2[l]), 'w_ffn2_gu': bf(w_ffn2_gu[l]), 'w_ffn2_down': bf(w_ffn2_down[l]),
        'g_final': row(g_final),
    }

    mk, mv, mk4, mv4 = _memkv_call(mem_prompt.reshape(bp * MEM_LEN, D_MODEL), row(g_mem[l]), bf(w_ckv[l]), 512)
    zeros_state = jnp.zeros((bp, S5_STATE), F32)
    y_p, st_p, re_p, im_p = _trunk_group(
        x_prompt, None, zeros_state, zeros_state, mk.reshape(bp, MEM_LEN, D_MODEL), mv.reshape(bp, MEM_LEN, D_MODEL),
        w, tm=512, gla_tile=256, gla_states=1, s5_t_tile=128, attn_seqs=1, attn_tq=512)

    seqs_per_chunk = GLA_CHUNK_ROWS // ts
    s0_t = jnp.transpose(state_gla[l], (0, 3, 1, 2)).reshape(bs, GLA_DV, GLA_QK_WIDTH)
    y_s, st_s, re_s, im_s = _trunk_group(
        x_sample, s0_t, state_s5_re[l].reshape(bs, S5_STATE), state_s5_im[l].reshape(bs, S5_STATE),
        cache_mem_k.reshape(DEPTH * bs, MEM_LEN, CROSS_HEADS, CROSS_DH)[l * bs:(l + 1) * bs],
        cache_mem_v.reshape(DEPTH * bs, MEM_LEN, CROSS_HEADS, CROSS_DH)[l * bs:(l + 1) * bs],
        w, tm=512, gla_tile=GLA_CHUNK_ROWS, gla_states=seqs_per_chunk, s5_t_tile=ts, attn_seqs=4, attn_tq=ts)

    def gla_state_out(st, nb):
        return jnp.transpose(st.reshape(nb, GLA_DV, GLA_HEADS, GLA_DK), (0, 2, 3, 1))[None]

    s5_out = lambda a, nb: a.reshape(1, nb, S5_GROUPS, S5_N)
    kv_out = lambda a: a.reshape(1, bp, MEM_LEN, CROSS_HEADS, CROSS_DH)
    return (y_p, y_s,
            gla_state_out(st_p, bp), s5_out(re_p, bp), s5_out(im_p, bp), kv_out(mk4), kv_out(mv4),
            gla_state_out(st_s, bs), s5_out(re_s, bs), s5_out(im_s, bs))
```

```python
import functools
import math

import numpy as np
import jax
import jax.numpy as jnp
from jax import lax
from jax.experimental import pallas as pl
from jax.experimental.pallas import tpu as pltpu

F32 = jnp.float32
BF16 = jnp.bfloat16

D_MODEL = 1024
DEPTH = 1
EPS = 1e-6
D_FF = 2816
GLA_WIDTH = 512
GLA_HEADS = 4
GLA_DV = 128
GLA_DK = 64
GLA_QK_WIDTH = 256
GLA_GATE_RANK = 16
GLA_TAU = 16.0
S5_WIDTH = 512
S5_P = 16
S5_GROUPS = 32
S5_N = 64
S5_STATE = S5_GROUPS * S5_N
CROSS_HEADS = 4
CROSS_DH = 256
MEM_LEN = 256

LANES = 128
SUBLANES = 8
MXU_DIM = 256
VMEM_LIMIT_BYTES = 56 * 1024 * 1024

FF_CHUNK = MXU_DIM
GLA_CHUNK_ROWS = 64
S5_LANE_CHUNK = 512


def _cparams(n_grid_axes):
    return pltpu.CompilerParams(
        dimension_semantics=("arbitrary",) * n_grid_axes,
        vmem_limit_bytes=VMEM_LIMIT_BYTES,
    )


def _const_spec(shape):
    zeros = (0,) * len(shape)
    return pl.BlockSpec(shape, lambda *_: zeros, pipeline_mode=pl.Buffered(1))


def _dot(a, b):
    return jnp.dot(a, b, preferred_element_type=F32)


def _dot_nt(a, b):
    return lax.dot_general(a, b, (((1,), (1,)), ((), ())), preferred_element_type=F32)


def _dot_tn(a, b):
    return lax.dot_general(a, b, (((0,), (0,)), ((), ())), preferred_element_type=F32)


def _rms(x, g):
    return x * lax.rsqrt(jnp.mean(x * x, axis=-1, keepdims=True) + EPS) * g


def _swiglu(xn_bf, wgu_ref, wdn_ref, between=None):
    n_chunks = D_FF // FF_CHUNK

    def gate_up(c):
        lo, hi = c * FF_CHUNK, (c + 1) * FF_CHUNK
        return _dot(xn_bf, wgu_ref[:, lo:hi]), _dot(xn_bf, wgu_ref[:, D_FF + lo:D_FF + hi])

    acc = None
    nxt = gate_up(0)
    for c in range(n_chunks):
        g, u = nxt
        if c + 1 < n_chunks:
            nxt = gate_up(c + 1)
        a = (jax.nn.silu(g) * u).astype(BF16)
        d = _dot(a, wdn_ref[c * FF_CHUNK:(c + 1) * FF_CHUNK, :])
        acc = d if acc is None else acc + d
        if between is not None:
            between()
    return acc


def _log_sigmoid(z):
    return jnp.minimum(z, 0.0) - jnp.log1p(jnp.exp(-jnp.abs(z)))


def _pre_kernel(x_ref, g1_ref, wgu_ref, wdn_ref, gm_ref, win_ref, wzg_ref, wgate_ref, bgate_ref,
                h_ref, qk_ref, v_ref, la_ref, r_ref, us5_ref):
    x = x_ref[...]
    xn = _rms(x, g1_ref[...]).astype(BF16)
    h = x + 0.5 * _swiglu(xn, wgu_ref, wdn_ref)
    h_ref[...] = h
    un = _rms(h, gm_ref[...]).astype(BF16)
    p = _dot(un, win_ref[...])
    qk_ref[...] = p[:, 0:512]
    v_ref[...] = p[:, 512:1024]
    r_ref[...] = p[:, 1024:1536]
    us5_ref[...] = p[:, 1536:2048]
    zg = _dot(un, wzg_ref[...])
    z = _dot(zg.astype(BF16), wgate_ref[...]) + bgate_ref[...]
    la_ref[...] = _log_sigmoid(z) * (1.0 / GLA_TAU)


def _pre_call(x2d, g1, wgu, wdn, gm, win, wzg, wgate, bgate, tm):
    n = x2d.shape[0]
    row = lambda w: pl.BlockSpec((tm, w), lambda i: (i, 0))
    out_shapes = [jax.ShapeDtypeStruct((n, w), F32) for w in (D_MODEL, 512, 512, 256, 512, 512)]
    return pl.pallas_call(
        _pre_kernel,
        grid=(n // tm,),
        in_specs=[row(D_MODEL), _const_spec(g1.shape), _const_spec(wgu.shape), _const_spec(wdn.shape),
                  _const_spec(gm.shape), _const_spec(win.shape), _const_spec(wzg.shape),
                  _const_spec(wgate.shape), _const_spec(bgate.shape)],
        out_specs=[row(D_MODEL), row(512), row(512), row(256), row(512), row(512)],
        out_shape=out_shapes,
        compiler_params=_cparams(1),
        name="pre_ffn_proj",
    )(x2d, g1, wgu, wdn, gm, win, wzg, wgate, bgate)


def _memkv_kernel(m_ref, g_ref, w_ref, k_ref, v_ref, k4_ref, v4_ref):
    mn = _rms(m_ref[...], g_ref[...]).astype(BF16)
    kv = _dot(mn, w_ref[...])
    k_ref[...] = kv[:, :D_MODEL]
    v_ref[...] = kv[:, D_MODEL:]
    for hd in range(CROSS_HEADS):
        k4_ref[:, hd, :] = kv[:, hd * CROSS_DH:(hd + 1) * CROSS_DH]
        v4_ref[:, hd, :] = kv[:, D_MODEL + hd * CROSS_DH:D_MODEL + (hd + 1) * CROSS_DH]


def _memkv_call(mem2d, g, w, tm):
    n = mem2d.shape[0]
    row = pl.BlockSpec((tm, D_MODEL), lambda i: (i, 0))
    row4 = pl.BlockSpec((tm, CROSS_HEADS, CROSS_DH), lambda i: (i, 0, 0))
    return pl.pallas_call(
        _memkv_kernel,
        grid=(n // tm,),
        in_specs=[row, _const_spec(g.shape), _const_spec(w.shape)],
        out_specs=[row, row, row4, row4],
        out_shape=[jax.ShapeDtypeStruct((n, D_MODEL), F32)] * 2
        + [jax.ShapeDtypeStruct((n, CROSS_HEADS, CROSS_DH), F32)] * 2,
        compiler_params=_cparams(1),
        name="memory_kv",
    )(mem2d, g, w)


def _mid_kernel(h_ref, og_ref, os_ref, wo_ref, gc_ref, wq_ref, h2_ref, q_ref):
    h2 = (h_ref[...] + _dot(og_ref[...].astype(BF16), wo_ref[0:GLA_WIDTH, :])
          + _dot(os_ref[...].astype(BF16), wo_ref[GLA_WIDTH:, :]))
    h2_ref[...] = h2
    hn = _rms(h2, gc_ref[...]).astype(BF16)
    q_ref[...] = _dot(hn, wq_ref[...])


def _mid_call(h, og, os_, wo, gc, wq, tm):
    n = h.shape[0]
    row = lambda w: pl.BlockSpec((tm, w), lambda i: (i, 0))
    return pl.pallas_call(
        _mid_kernel,
        grid=(n // tm,),
        in_specs=[row(D_MODEL), row(512), row(512), _const_spec(wo.shape), _const_spec(gc.shape),
                  _const_spec(wq.shape)],
        out_specs=[row(D_MODEL), row(D_MODEL)],
        out_shape=[jax.ShapeDtypeStruct((n, D_MODEL), F32)] * 2,
        compiler_params=_cparams(1),
        name="mix_out_cross_q",
    )(h, og, os_, wo, gc, wq)


def _post_kernel(h2_ref, o_ref, wco_ref, g2_ref, wgu_ref, wdn_ref, gf_ref, y_ref):
    h3 = h2_ref[...] + _dot(o_ref[...].astype(BF16), wco_ref[...])
    hn = _rms(h3, g2_ref[...]).astype(BF16)
    h4 = h3 + 0.5 * _swiglu(hn, wgu_ref, wdn_ref)
    y_ref[...] = _rms(h4, gf_ref[...])


def _post_call(h2, o, wco, g2, wgu, wdn, gf, tm):
    n = h2.shape[0]
    row = pl.BlockSpec((tm, D_MODEL), lambda i: (i, 0))
    return pl.pallas_call(
        _post_kernel,
        grid=(n // tm,),
        in_specs=[row, row, _const_spec(wco.shape), _const_spec(g2.shape), _const_spec(wgu.shape),
                  _const_spec(wdn.shape), _const_spec(gf.shape)],
        out_specs=row,
        out_shape=jax.ShapeDtypeStruct((n, D_MODEL), F32),
        compiler_params=_cparams(1),
        name="post_ffn_final",
    )(h2, o, wco, g2, wgu, wdn, gf)


def _tail_kernel(h_ref, og_ref, os_ref, k_ref, v_ref, wo_ref, gc_ref, wq_ref, wco_ref, g2_ref, wgu_ref,
                 wdn_ref, gf_ref, y_ref):
    h2 = (h_ref[0] + _dot(og_ref[0].astype(BF16), wo_ref[0:GLA_WIDTH, :])
          + _dot(os_ref[0].astype(BF16), wo_ref[GLA_WIDTH:, :]))
    q = _dot(_rms(h2, gc_ref[...]).astype(BF16), wq_ref[...]).astype(BF16)
    heads = [slice(hd * CROSS_DH, (hd + 1) * CROSS_DH) for hd in range(CROSS_HEADS)]
    scores = [_dot_nt(q[:, sl], k_ref[0, :, sl].astype(BF16)) * (CROSS_DH ** -0.5) for sl in heads]
    probs = [_softmax(s).astype(BF16) for s in scores]
    outs = [_dot(p, v_ref[0, :, sl].astype(BF16)).astype(BF16) for p, sl in zip(probs, heads)]
    h3 = h2 + _dot(jnp.concatenate(outs, axis=-1), wco_ref[...])
    hn = _rms(h3, g2_ref[...]).astype(BF16)
    h4 = h3 + 0.5 * _swiglu(hn, wgu_ref, wdn_ref)
    y_ref[0] = _rms(h4, gf_ref[...])


def _tail_call(h3d, og3, os3, k3, v3, w, tm):
    s, t, _ = h3d.shape
    row = lambda wd: pl.BlockSpec((1, tm, wd), lambda i, j: (i, j, 0))
    kvspec = pl.BlockSpec((1, MEM_LEN, D_MODEL), lambda i, j: (i, 0, 0))
    consts = [w['w_out'], w['g_cross'], w['w_cq'], w['w_co'], w['g_ffn2'], w['w_ffn2_gu'], w['w_ffn2_down'],
              w['g_final']]
    return pl.pallas_call(
        _tail_kernel,
        grid=(s, t // tm),
        in_specs=[row(D_MODEL), row(GLA_WIDTH), row(S5_WIDTH), kvspec, kvspec] + [_const_spec(a.shape) for a in consts],
        out_specs=row(D_MODEL),
        out_shape=jax.ShapeDtypeStruct((s, t, D_MODEL), F32),
        compiler_params=_cparams(2),
        name="tail_attn_ffn",
    )(h3d, og3, os3, k3, v3, *consts)


def _softmax(s):
    e = jnp.exp(s - jnp.max(s, axis=-1, keepdims=True))
    return e / jnp.sum(e, axis=-1, keepdims=True)


def _load_mem(ref, j):
    if len(ref.shape) == 3:
        return ref[j].astype(BF16)
    return jnp.concatenate([ref[j, :, hd, :] for hd in range(CROSS_HEADS)], axis=-1).astype(BF16)


def _attn_packed_kernel(q_ref, k_ref, v_ref, o_ref, *, n_seq):
    t = q_ref.shape[1]
    shape = (CROSS_HEADS * t, D_MODEL)
    q_mask = (_iota2(shape, 0) // t) == (_iota2(shape, 1) // CROSS_DH)
    o_lane_head = _iota2((t, D_MODEL), 1) // CROSS_DH
    scores = []
    for j in range(n_seq):
        q_stack = jnp.where(q_mask, jnp.concatenate([q_ref[j]] * CROSS_HEADS, axis=0), 0.0).astype(BF16)
        scores.append(_dot_nt(q_stack, _load_mem(k_ref, j)) * (CROSS_DH ** -0.5))
    probs = [_softmax(s).astype(BF16) for s in scores]
    for j in range(n_seq):
        full = _dot(probs[j], _load_mem(v_ref, j))
        o = jnp.zeros((t, D_MODEL), F32)
        for hd in range(CROSS_HEADS):
            o = jnp.where(o_lane_head == hd, full[hd * t:(hd + 1) * t, :], o)
        o_ref[j] = o.astype(o_ref.dtype)


def _attn_kernel(q_ref, k_ref, v_ref, o_ref, *, n_seq):
    cast = lambda a: a.astype(BF16)
    for j in range(n_seq):
        q = q_ref[j]
        outs = []
        for hd in range(CROSS_HEADS):
            sl = slice(hd * CROSS_DH, (hd + 1) * CROSS_DH)
            s = _dot_nt(cast(q[:, sl]), cast(k_ref[j, :, sl])) * (CROSS_DH ** -0.5)
            p = _softmax(s)
            outs.append(_dot(cast(p), cast(v_ref[j, :, sl])))
        o_ref[j] = jnp.concatenate(outs, axis=-1).astype(o_ref.dtype)


def _attn_call(q3, k3, v3, n_seq, tq):
    s, t, _ = q3.shape
    qspec = pl.BlockSpec((n_seq, tq, D_MODEL), lambda i, j: (i, j, 0))
    if k3.ndim == 3:
        kvspec = pl.BlockSpec((n_seq, MEM_LEN, D_MODEL), lambda i, j: (i, 0, 0))
    else:
        kvspec = pl.BlockSpec((n_seq, MEM_LEN, CROSS_HEADS, CROSS_DH), lambda i, j: (i, 0, 0, 0))
    return pl.pallas_call(
        functools.partial(_attn_packed_kernel if tq * CROSS_HEADS <= MXU_DIM else _attn_kernel, n_seq=n_seq),
        grid=(s // n_seq, t // tq),
        in_specs=[qspec, kvspec, kvspec],
        out_specs=qspec,
        out_shape=jax.ShapeDtypeStruct(q3.shape, BF16 if tq % 16 == 0 else F32),
        compiler_params=_cparams(2),
        name="cross_attention",
    )(q3, k3, v3)


def _split3(x):
    hi = x.astype(BF16)
    r1 = x - hi.astype(F32)
    mid = r1.astype(BF16)
    lo = (r1 - mid.astype(F32)).astype(BF16)
    return hi, mid, lo


def _dot01(m01, x):
    hi, mid, lo = _split3(x)
    return _dot(m01, hi) + _dot(m01, mid) + _dot(m01, lo)


def _iota2(shape, dim):
    return lax.broadcasted_iota(jnp.int32, shape, dim)


def _gla_stages(qk_ref, v_ref, la_ref, r_ref, o_ref, g_ref, tri_ref, lvl_ref, khm_ref, vhm_ref, shm_ref,
                states, put_states, *, seq_len, n_chunks):
    c_rows = GLA_CHUNK_ROWS
    n_sub = c_rows // seq_len
    log2 = lambda n: int(math.log2(n))
    tri = tri_ref[...]
    lvl = lvl_ref[...]
    rowq = _iota2((c_rows, GLA_QK_WIDTH), 0)
    u_head = [(_iota2((GLA_DV, GLA_QK_WIDTH), 1) >> log2(GLA_DK)) == hd for hd in range(GLA_HEADS)]

    def head_blocks(x, mask_ref):
        return jnp.concatenate([x.astype(BF16)] * GLA_HEADS, axis=0) * mask_ref[...]

    g_out = g_ref[...]
    chunks = range(n_chunks)
    rows_of = lambda c: slice(c * c_rows, (c + 1) * c_rows)
    sub_of = lambda j: slice(j * seq_len, (j + 1) * seq_len)
    q = [qk_ref[rows_of(c), 0:GLA_QK_WIDTH] * (GLA_DK ** -0.5) for c in chunks]
    k = [qk_ref[rows_of(c), GLA_QK_WIDTH:2 * GLA_QK_WIDTH] for c in chunks]
    v = [v_ref[rows_of(c), :] for c in chunks]
    b = [_dot01(tri, la_ref[rows_of(c), :]) for c in chunks]
    yield

    a_mat = [jnp.where(lvl == 0, _dot_nt(q[c].astype(BF16), head_blocks(k[c], khm_ref)), 0.0) for c in chunks]
    last = list(b)
    m = 1
    while m < seq_len:
        yield
        even = ((rowq >> log2(m)) & 1) == 0
        for c in chunks:
            prev_last = pltpu.roll(last[c], m, axis=0)
            e = jnp.exp(jnp.minimum(jnp.where(even, last[c] - b[c], b[c] - prev_last), 0.0))
            a_lvl = _dot_nt((q[c] * e).astype(BF16), head_blocks(k[c] * e, khm_ref))
            a_mat[c] = jnp.where(lvl == log2(m) + 1, a_lvl, a_mat[c])
            last[c] = jnp.where(even, pltpu.roll(last[c], c_rows - m, axis=0), last[c])
        m *= 2
    yield
    o_intra = [_dot(a_mat[c].astype(BF16), head_blocks(v[c], vhm_ref)) for c in chunks]
    yield

    narrow = (lambda a: a.astype(BF16)) if seq_len % 16 == 0 else (lambda a: a)
    q_dec = [narrow(q[c] * jnp.exp(b[c])) for c in chunks]
    k_dec = [narrow(k[c] * jnp.exp(jnp.minimum(last[c] - b[c], 0.0))) for c in chunks]
    u_t = {}
    for c in chunks:
        vc = narrow(v[c])
        for j in range(n_sub):
            u_full = _dot_tn(vc[sub_of(j)], k_dec[c][sub_of(j)])
            acc = jnp.zeros((GLA_DV, GLA_QK_WIDTH), F32)
            for hd in range(GLA_HEADS):
                acc = jnp.where(u_head[hd], u_full[hd * GLA_DV:(hd + 1) * GLA_DV, :], acc)
            u_t[c, j] = acc

    states = list(states)
    state_of = (lambda c, j: c * n_sub + j) if n_sub > 1 else (lambda c, j: 0)
    for c in chunks:
        if c % 2 == 0:
            yield
        o_parts = []
        for j in range(n_sub):
            sj = state_of(c, j)
            o_parts.append(_dot_nt(q_dec[c][sub_of(j)], head_blocks(states[sj], shm_ref)))
            decay = jnp.exp(last[c][j * seq_len:j * seq_len + 1, :])
            states[sj] = states[sj] * decay + u_t[c, j]
        o_inter = o_parts[0] if n_sub == 1 else jnp.concatenate(o_parts, axis=0)
        o = o_intra[c] + o_inter
        r = r_ref[rows_of(c), :]
        outs = []
        for hd in range(GLA_HEADS):
            sl = slice(hd * GLA_DV, (hd + 1) * GLA_DV)
            outs.append(_rms(o[:, sl], g_out) * jax.nn.silu(r[:, sl]))
        o_ref[rows_of(c), :] = jnp.concatenate(outs, axis=-1).astype(o_ref.dtype)
    put_states(states)


def _gla_kernel(*refs, seq_len, n_chunks, n_state, has_s0):
    ins, rest = refs[:10], refs[10:]
    qk_ref, v_ref, la_ref, r_ref, g_ref = ins[:5]
    if has_s0:
        s0_ref, o_ref, st_ref, s_scr = rest
    else:
        o_ref, st_ref, s_scr = rest
        s0_ref = None
    t_idx = pl.program_id(1)

    @pl.when(t_idx == 0)
    def _():
        if has_s0:
            s_scr[...] = s0_ref[...]
        else:
            s_scr[...] = jnp.zeros_like(s_scr)

    def put_states(states):
        for j, s_t in enumerate(states):
            s_scr[j] = s_t

    for _ in _gla_stages(qk_ref.at[0], v_ref.at[0], la_ref.at[0], r_ref.at[0], o_ref.at[0], g_ref, *ins[5:],
                         [s_scr[j] for j in range(n_state)], put_states, seq_len=seq_len, n_chunks=n_chunks):
        pass

    @pl.when(t_idx == pl.num_programs(1) - 1)
    def _():
        st_ref[...] = s_scr[...]


def _gla_index_constants(seq_len):
    c = GLA_CHUNK_ROWS
    i = np.arange(c)[:, None]
    s = np.arange(c)[None, :]
    same_seq = (i // seq_len) == (s // seq_len)
    tri = (same_seq & (s <= i)).astype(np.float32)
    top_bit = np.floor(np.log2(np.maximum(i ^ s, 1))).astype(np.int32)
    lvl = np.where(same_seq & (s < i), top_bit + 1, -1)
    lvl = np.where(i == s, 0, lvl).astype(np.int32)
    head_mask = lambda rows_per_head, lanes_per_head: (
        (np.arange(GLA_HEADS * rows_per_head)[:, None] // rows_per_head)
        == (np.arange(GLA_HEADS * lanes_per_head)[None, :] // lanes_per_head)).astype(np.float32)
    return [jnp.asarray(tri, BF16), jnp.asarray(np.tile(lvl, (1, GLA_HEADS))),
            jnp.asarray(head_mask(c, GLA_DK), BF16), jnp.asarray(head_mask(c, GLA_DV), BF16),
            jnp.asarray(head_mask(GLA_DV, GLA_DK), BF16)]


def _gla_call(qk3, v3, la3, r3, g_out, s0_t, seq_len, tile_rows, n_state):
    n_outer, rows, _ = qk3.shape
    n_chunks = tile_rows // GLA_CHUNK_ROWS
    spec = lambda w: pl.BlockSpec((1, tile_rows, w), lambda i, t: (i, t, 0))
    st_spec = pl.BlockSpec((n_state, GLA_DV, GLA_QK_WIDTH), lambda i, t: (i, 0, 0))
    has_s0 = s0_t is not None
    consts = [g_out] + _gla_index_constants(seq_len)
    in_specs = [spec(512), spec(512), spec(256), spec(512)] + [_const_spec(a.shape) for a in consts]
    args = [qk3, v3, la3, r3] + consts
    if has_s0:
        in_specs.append(st_spec)
        args.append(s0_t)
    return pl.pallas_call(
        functools.partial(_gla_kernel, seq_len=seq_len, n_chunks=n_chunks, n_state=n_state, has_s0=has_s0),
        grid=(n_outer, rows // tile_rows),
        in_specs=in_specs,
        out_specs=[spec(512), st_spec],
        out_shape=[jax.ShapeDtypeStruct((n_outer, rows, GLA_WIDTH), BF16),
                   jax.ShapeDtypeStruct((n_outer * n_state, GLA_DV, GLA_QK_WIDTH), F32)],
        scratch_shapes=[pltpu.VMEM((n_state, GLA_DV, GLA_QK_WIDTH), F32)],
        compiler_params=_cparams(2),
        name="gla_mixer",
    )(*args)


def _pre_gla_kernel(x_ref, g1_ref, wgu_ref, wdn_ref, gm_ref, win_ref, wzg_ref, wgate_ref, bgate_ref, gout_ref,
                    tri_ref, lvl_ref, khm_ref, vhm_ref, shm_ref,
                    h_ref, us5_ref, og_ref, st_ref,
                    qk_scr, v_scr, la_scr, r_scr, s_scr, *, tiles_per_seq):
    i = pl.program_id(0)
    last_step = pl.num_programs(0) - 1
    n_chunks = x_ref.shape[0] // GLA_CHUNK_ROWS

    @pl.when(i == 0)
    def _():
        for scr in (qk_scr, v_scr, la_scr, r_scr, s_scr):
            scr[...] = jnp.zeros_like(scr)

    def gla_stages():
        first_of_seq = lax.rem(i + (tiles_per_seq - 1), tiles_per_seq) == 0
        state = jnp.where(first_of_seq, 0.0, s_scr[0])

        def put_states(states):
            s_scr[0] = states[0]
            st_ref[0] = states[0]

        return _gla_stages(qk_scr, v_scr, la_scr, r_scr, og_ref, gout_ref, tri_ref, lvl_ref, khm_ref, vhm_ref,
                           shm_ref, [state], put_states, seq_len=GLA_CHUNK_ROWS, n_chunks=n_chunks)

    @pl.when(i < last_step)
    def _():
        stages = gla_stages()
        next(stages)
        x = x_ref[...]
        xn = _rms(x, g1_ref[...]).astype(BF16)
        h = x + 0.5 * _swiglu(xn, wgu_ref, wdn_ref, between=lambda: next(stages, None))
        h_ref[...] = h
        un = _rms(h, gm_ref[...]).astype(BF16)
        p = []
        for grp in range(4):
            p.append(_dot(un, win_ref[:, grp * 512:(grp + 1) * 512]))
            next(stages, None)
        zg = _dot(un, wzg_ref[...])
        z = _dot(zg.astype(BF16), wgate_ref[...]) + bgate_ref[...]
        for _ in stages:
            pass
        qk_scr[...] = p[0]
        v_scr[...] = p[1]
        r_scr[...] = p[2]
        us5_ref[...] = p[3]
        la_scr[...] = _log_sigmoid(z) * (1.0 / GLA_TAU)

    @pl.when(i == last_step)
    def _():
        for _ in gla_stages():
            pass


def _pre_gla_call(x2d, w, tm, seq_rows):
    n = x2d.shape[0]
    n_tiles = n // tm
    tiles_per_seq = seq_rows // tm
    cur = lambda wd: pl.BlockSpec((tm, wd), lambda i: (jnp.minimum(i, n_tiles - 1), 0))
    prev = lambda wd: pl.BlockSpec((tm, wd), lambda i: (jnp.maximum(i - 1, 0), 0))
    st_spec = pl.BlockSpec((1, GLA_DV, GLA_QK_WIDTH), lambda i: (jnp.maximum(i - 1, 0) // tiles_per_seq, 0, 0))
    consts = [w['g_ffn1'], w['w_ffn1_gu'], w['w_ffn1_down'], w['g_mix'], w['w_in_main'], w['w_zg'], w['w_gate'],
              w['b_gate'], w['g_gla_out']] + _gla_index_constants(GLA_CHUNK_ROWS)
    return pl.pallas_call(
        functools.partial(_pre_gla_kernel, tiles_per_seq=tiles_per_seq),
        grid=(n_tiles + 1,),
        in_specs=[cur(D_MODEL)] + [_const_spec(a.shape) for a in consts],
        out_specs=[cur(D_MODEL), cur(S5_WIDTH), prev(GLA_WIDTH), st_spec],
        out_shape=[jax.ShapeDtypeStruct((n, D_MODEL), F32), jax.ShapeDtypeStruct((n, S5_WIDTH), F32),
                   jax.ShapeDtypeStruct((n, GLA_WIDTH), BF16),
                   jax.ShapeDtypeStruct((n // seq_rows, GLA_DV, GLA_QK_WIDTH), F32)],
        scratch_shapes=[pltpu.VMEM((tm, 2 * GLA_QK_WIDTH), F32), pltpu.VMEM((tm, GLA_WIDTH), F32),
                        pltpu.VMEM((tm, GLA_QK_WIDTH), F32), pltpu.VMEM((tm, GLA_WIDTH), F32),
                        pltpu.VMEM((1, GLA_DV, GLA_QK_WIDTH), F32)],
        compiler_params=_cparams(1),
        name="pre_ffn_gla",
    )(x2d, *consts)


def _s5_disc_kernel(lre_ref, lim_ref, ldt_ref, bre_ref, bim_ref, are_ref, aim_ref, bbre_ref, bbim_ref):
    lr = lre_ref[...]
    li = lim_ref[...]
    dt = jnp.exp(ldt_ref[...])
    mag = jnp.exp(dt * lr)
    ab_re = mag * jnp.cos(dt * li)
    ab_im = mag * jnp.sin(dt * li)
    den = lr * lr + li * li
    coef_re = ((ab_re - 1.0) * lr + ab_im * li) / den
    coef_im = (ab_im * lr - (ab_re - 1.0) * li) / den
    are_ref[...] = ab_re
    aim_ref[...] = ab_im
    br = bre_ref[...]
    bi = bim_ref[...]
    cr = coef_re[:, None, :]
    cim = coef_im[:, None, :]
    bbre_ref[...] = cr * br - cim * bi
    bbim_ref[...] = cr * bi + cim * br


def _s5_disc_call(lre, lim, ldt, bre_t, bim_t):
    gn = jax.ShapeDtypeStruct((S5_GROUPS, S5_N), F32)
    gpn = jax.ShapeDtypeStruct((S5_GROUPS, S5_P, S5_N), F32)
    return pl.pallas_call(
        _s5_disc_kernel,
        out_shape=[gn, gn, gpn, gpn],
        name="s5_discretise",
    )(lre, lim, ldt, bre_t, bim_t)


def _to_time_major(u_ref, perm_scr, n_seq, t0, nt):
    if n_seq == SUBLANES:
        n_slab = S5_WIDTH // LANES
        for j in range(n_seq):
            for sb in range(n_slab):
                perm_scr[sb, pl.ds(t0 * n_seq + j, nt, stride=n_seq), :] = (
                    u_ref[j, t0:t0 + nt, sb * LANES:(sb + 1) * LANES])
        return jnp.concatenate([perm_scr[sb, t0 * n_seq:(t0 + nt) * n_seq, :] for sb in range(n_slab)], axis=-1)
    return jnp.concatenate([u_ref[:, t, :] for t in range(t0, t0 + nt)], axis=0)


def _from_time_major(y, o_ref, perm_scr, n_seq, t0, nt):
    if n_seq == SUBLANES:
        n_slab = S5_WIDTH // LANES
        for sb in range(n_slab):
            perm_scr[sb, t0 * n_seq:(t0 + nt) * n_seq, :] = y[:, sb * LANES:(sb + 1) * LANES]
        for j in range(n_seq):
            for sb in range(n_slab):
                o_ref[j, t0:t0 + nt, sb * LANES:(sb + 1) * LANES] = (
                    perm_scr[sb, pl.ds(t0 * n_seq + j, nt, stride=n_seq), :])
    else:
        for t in range(nt):
            o_ref[:, t0 + t, :] = y[t * n_seq:(t + 1) * n_seq, :]


def _s5_kernel(u_ref, x0re_ref, x0im_ref, are_ref, aim_ref, bre_ref, bim_ref, c_ref, d_ref, wglu_ref,
               bglu_ref, o_ref, xre_ref, xim_ref, sre_scr, sim_scr, xre_scr, xim_scr, perm_scr, *, n_seq, t_tile):
    half_w = S5_WIDTH // 2
    half_s = S5_STATE // 2

    @pl.when(pl.program_id(0) == 0)
    def _():
        xre_scr[...] = x0re_ref[...]
        xim_scr[...] = x0im_ref[...]

    nt = t_tile
    rows = nt * n_seq
    for t0 in (0,):
        u = _to_time_major(u_ref, perm_scr, n_seq, t0, nt)
        ub = u.astype(BF16)
        for hf in range(2):
            uh = ub[:, hf * half_w:(hf + 1) * half_w]
            st = slice(hf * half_s, (hf + 1) * half_s)
            sre_scr[t0:t0 + nt, :, st] = _dot(uh, bre_ref[hf]).reshape(nt, n_seq, half_s)
            sim_scr[t0:t0 + nt, :, st] = _dot(uh, bim_ref[hf]).reshape(nt, n_seq, half_s)

        for lc in range(S5_STATE // S5_LANE_CHUNK):
            lanes = slice(lc * S5_LANE_CHUNK, (lc + 1) * S5_LANE_CHUNK)
            ar = jnp.broadcast_to(are_ref[:, lanes], (SUBLANES, S5_LANE_CHUNK))
            ai = jnp.broadcast_to(aim_ref[:, lanes], (SUBLANES, S5_LANE_CHUNK))
            for sc in range(n_seq // SUBLANES):
                srows = slice(sc * SUBLANES, (sc + 1) * SUBLANES)
                xr = xre_scr[srows, lanes]
                xi = xim_scr[srows, lanes]
                for t in range(t0, t0 + nt):
                    nxr = ar * xr - ai * xi + sre_scr[t, srows, lanes]
                    nxi = ar * xi + ai * xr + sim_scr[t, srows, lanes]
                    sre_scr[t, srows, lanes] = nxr
                    sim_scr[t, srows, lanes] = nxi
                    xr, xi = nxr, nxi
                xre_scr[srows, lanes] = xr
                xim_scr[srows, lanes] = xi

        xr_all = sre_scr[t0:t0 + nt].reshape(rows, S5_STATE).astype(BF16)
        xi_all = sim_scr[t0:t0 + nt].reshape(rows, S5_STATE).astype(BF16)
        ys = []
        for hf in range(2):
            st = slice(hf * half_s, (hf + 1) * half_s)
            ys.append(_dot(xr_all[:, st], c_ref[hf, 0:half_s, :]) + _dot(xi_all[:, st], c_ref[hf, half_s:, :]))
        y = jnp.concatenate(ys, axis=-1) + d_ref[...] * u
        z = jax.nn.gelu(y)
        gate = jax.nn.sigmoid(_dot(z.astype(BF16), wglu_ref[...]) + bglu_ref[...])
        _from_time_major(z * gate, o_ref, perm_scr, n_seq, t0, nt)

    @pl.when(pl.program_id(0) == pl.num_programs(0) - 1)
    def _():
        xre_ref[...] = xre_scr[...]
        xim_ref[...] = xim_scr[...]


def _s5_call(u3, x0re, x0im, are, aim, bre, bim, cmat, dskip, wglu, bglu, t_tile):
    n_seq, t, _ = u3.shape
    uspec = pl.BlockSpec((n_seq, t_tile, S5_WIDTH), lambda i: (0, i, 0))
    perm_rows = t_tile * n_seq if n_seq == SUBLANES else SUBLANES
    xspec = pl.BlockSpec((n_seq, S5_STATE), lambda i: (0, 0))
    consts = [are, aim, bre, bim, cmat, dskip, wglu, bglu]
    return pl.pallas_call(
        functools.partial(_s5_kernel, n_seq=n_seq, t_tile=t_tile),
        grid=(t // t_tile,),
        in_specs=[uspec, xspec, xspec] + [_const_spec(a.shape) for a in consts],
        out_specs=[uspec, xspec, xspec],
        out_shape=[jax.ShapeDtypeStruct((n_seq, t, S5_WIDTH), F32),
                   jax.ShapeDtypeStruct((n_seq, S5_STATE), F32),
                   jax.ShapeDtypeStruct((n_seq, S5_STATE), F32)],
        scratch_shapes=[pltpu.VMEM((t_tile, n_seq, S5_STATE), F32), pltpu.VMEM((t_tile, n_seq, S5_STATE), F32),
                        pltpu.VMEM((n_seq, S5_STATE), F32), pltpu.VMEM((n_seq, S5_STATE), F32),
                        pltpu.VMEM((S5_WIDTH // LANES, perm_rows, LANES), F32)],
        compiler_params=_cparams(1),
        name="s5_mixer",
    )(u3, x0re, x0im, *consts)


def _block_diag(blocks):
    g, a, b = blocks.shape
    eye = jnp.eye(g, dtype=blocks.dtype)
    return (eye[:, None, :, None] * blocks[:, :, None, :]).reshape(g * a, g * b)


def _halves_block_diag(blocks):
    g = blocks.shape[0]
    return jnp.stack([_block_diag(blocks[:g // 2]), _block_diag(blocks[g // 2:])])


def _trunk_group(x3, s_gla0, x0re, x0im, mem_k3, mem_v3, w, *, tm, gla_tile, gla_states, s5_t_tile,
                 attn_seqs, attn_tq):
    s, t, _ = x3.shape
    n = s * t
    if t >= tm and s_gla0 is None:
        h, us5, o_gla, st_new = _pre_gla_call(x3.reshape(n, D_MODEL), w, tm, t)
    else:
        h, qk, v, la, r, us5 = _pre_call(x3.reshape(n, D_MODEL), w['g_ffn1'], w['w_ffn1_gu'], w['w_ffn1_down'],
                                         w['g_mix'], w['w_in_main'], w['w_zg'], w['w_gate'], w['b_gate'], tm)
        if gla_states == 1:
            shp = lambda a: a.reshape(s, t, a.shape[-1])
        else:
            rows = gla_states * t
            shp = lambda a: a.reshape(n // rows, rows, a.shape[-1])
        o_gla, st_new = _gla_call(shp(qk), shp(v), shp(la), shp(r), w['g_gla_out'], s_gla0,
                                  t if gla_states > 1 else GLA_CHUNK_ROWS, gla_tile, gla_states)
    o_s5, xre, xim = _s5_call(us5.reshape(s, t, S5_WIDTH), x0re, x0im, w['s5_are'], w['s5_aim'], w['s5_bre'],
                              w['s5_bim'], w['s5_c'], w['s5_d'], w['s5_wglu'], w['s5_bglu'], s5_t_tile)
    if t >= tm:
        y = _tail_call(h.reshape(s, t, D_MODEL), o_gla.reshape(s, t, GLA_WIDTH), o_s5, mem_k3, mem_v3, w, tm)
        return y, st_new, xre, xim
    o_s5 = o_s5.reshape(n, S5_WIDTH)
    h2, q = _mid_call(h, o_gla.reshape(n, GLA_WIDTH), o_s5, w['w_out'], w['g_cross'], w['w_cq'], tm)
    o_att = _attn_call(q.reshape(s, t, D_MODEL), mem_k3, mem_v3, attn_seqs, attn_tq)
    y = _post_call(h2, o_att.reshape(n, D_MODEL), w['w_co'], w['g_ffn2'], w['w_ffn2_gu'], w['w_ffn2_down'],
                   w['g_final'], tm)
    return y.reshape(s, t, D_MODEL), st_new, xre, xim


def kernel(x_prompt, x_sample, state_gla, state_s5_re, state_s5_im, cache_mem_k, cache_mem_v, mem_prompt, g_ffn1, w_ffn1_gu, w_ffn1_down, g_mix, w_in, w_gla_gate, b_gla_gate, g_gla_out, s5_lambda_re, s5_lambda_im, s5_log_dt, s5_b_re, s5_b_im, s5_c_re, s5_c_im, s5_d, s5_w_glu, s5_b_glu, w_out, g_cross, g_mem, w_cq, w_ckv, w_co, g_ffn2, w_ffn2_gu, w_ffn2_down, g_final):
    bp, tp, _ = x_prompt.shape
    bs, ts, _ = x_sample.shape
    l = 0
    row = lambda a: a.reshape(1, -1).astype(F32)
    bf = lambda a: a.astype(BF16)

    are, aim, bbre, bbim = _s5_disc_call(
        s5_lambda_re[l], s5_lambda_im[l], s5_log_dt[l].reshape(S5_GROUPS, 1),
        jnp.transpose(s5_b_re[l], (0, 2, 1)), jnp.transpose(s5_b_im[l], (0, 2, 1)))
    c_re_t = jnp.transpose(s5_c_re[l], (0, 2, 1))
    c_im_t = jnp.transpose(s5_c_im[l], (0, 2, 1))
    cre_h = _halves_block_diag(c_re_t)
    cim_h = _halves_block_diag(-c_im_t)
    w_in_l = w_in[l]
    zg_lo = 2 * GLA_QK_WIDTH + GLA_WIDTH
    zg_hi = zg_lo + GLA_GATE_RANK
    w = {
        'g_ffn1': row(g_ffn1[l]), 'w_ffn1_gu': bf(w_ffn1_gu[l]), 'w_ffn1_down': bf(w_ffn1_down[l]),
        'g_mix': row(g_mix[l]),
        'w_in_main': bf(jnp.concatenate([w_in_l[:, :zg_lo], w_in_l[:, zg_hi:]], axis=1)),
        'w_zg': bf(jnp.pad(w_in_l[:, zg_lo:zg_hi], ((0, 0), (0, LANES - GLA_GATE_RANK)))),
        'w_gate': bf(jnp.pad(w_gla_gate[l], ((0, LANES - GLA_GATE_RANK), (0, 0)))),
        'b_gate': row(b_gla_gate[l]),
        'g_gla_out': row(g_gla_out[l]),
        's5_are': are.reshape(1, S5_STATE), 's5_aim': aim.reshape(1, S5_STATE),
        's5_bre': bf(_halves_block_diag(bbre)), 's5_bim': bf(_halves_block_diag(bbim)),
        's5_c': bf(jnp.concatenate([cre_h, cim_h], axis=1)),
        's5_d': row(s5_d[l]), 's5_wglu': bf(_block_diag(s5_w_glu[l])), 's5_bglu': row(s5_b_glu[l]),
        'w_out': bf(w_out[l]), 'g_cross': row(g_cross[l]), 'w_cq': bf(w_cq[l]), 'w_co': bf(w_co[l]),
        'g_ffn2': row(g_ffn2[l]), 'w_ffn2_gu': bf(w_ffn2_gu[l]), 'w_ffn2_down': bf(w_ffn2_down[l]),
        'g_final': row(g_final),
    }

    mk, mv, mk4, mv4 = _memkv_call(mem_prompt.reshape(bp * MEM_LEN, D_MODEL), row(g_mem[l]), bf(w_ckv[l]), 512)
    zeros_state = jnp.zeros((bp, S5_STATE), F32)
    y_p, st_p, re_p, im_p = _trunk_group(
        x_prompt, None, zeros_state, zeros_state, mk.reshape(bp, MEM_LEN, D_MODEL), mv.reshape(bp, MEM_LEN, D_MODEL),
        w, tm=512, gla_tile=256, gla_states=1, s5_t_tile=128, attn_seqs=1, attn_tq=512)

    seqs_per_chunk = GLA_CHUNK_ROWS // ts
    s0_t = jnp.transpose(state_gla[l], (0, 3, 1, 2)).reshape(bs, GLA_DV, GLA_QK_WIDTH)
    y_s, st_s, re_s, im_s = _trunk_group(
        x_sample, s0_t, state_s5_re[l].reshape(bs, S5_STATE), state_s5_im[l].reshape(bs, S5_STATE),
        cache_mem_k.reshape(DEPTH * bs, MEM_LEN, CROSS_HEADS, CROSS_DH)[l * bs:(l + 1) * bs],
        cache_mem_v.reshape(DEPTH * bs, MEM_LEN, CROSS_HEADS, CROSS_DH)[l * bs:(l + 1) * bs],
        w, tm=512, gla_tile=4 * GLA_CHUNK_ROWS, gla_states=4 * seqs_per_chunk, s5_t_tile=ts, attn_seqs=4, attn_tq=ts)

    def gla_state_out(st, nb):
        return jnp.transpose(st.reshape(nb, GLA_DV, GLA_HEADS, GLA_DK), (0, 2, 3, 1))[None]

    s5_out = lambda a, nb: a.reshape(1, nb, S5_GROUPS, S5_N)
    kv_out = lambda a: a.reshape(1, bp, MEM_LEN, CROSS_HEADS, CROSS_DH)
    return (y_p, y_s,
            gla_state_out(st_p, bp), s5_out(re_p, bp), s5_out(im_p, bp), kv_out(mk4), kv_out(mv4),
            gla_state_out(st_s, bs), s5_out(re_s, bs), s5_out(im_s, bs))
```

```python
import functools
import math

import numpy as np
import jax
import jax.numpy as jnp
from jax import lax
from jax.experimental import pallas as pl
from jax.experimental.pallas import tpu as pltpu

F32 = jnp.float32
BF16 = jnp.bfloat16

D_MODEL = 1024
DEPTH = 1
EPS = 1e-6
D_FF = 2816
GLA_WIDTH = 512
GLA_HEADS = 4
GLA_DV = 128
GLA_DK = 64
GLA_QK_WIDTH = 256
GLA_GATE_RANK = 16
GLA_TAU = 16.0
S5_WIDTH = 512
S5_P = 16
S5_GROUPS = 32
S5_N = 64
S5_STATE = S5_GROUPS * S5_N
CROSS_HEADS = 4
CROSS_DH = 256
MEM_LEN = 256

LANES = 128
SUBLANES = 8
MXU_DIM = 256
VMEM_LIMIT_BYTES = 56 * 1024 * 1024

FF_CHUNK = MXU_DIM
GLA_CHUNK_ROWS = 64
S5_LANE_CHUNK = 512


def _cparams(n_grid_axes):
    return pltpu.CompilerParams(
        dimension_semantics=("arbitrary",) * n_grid_axes,
        vmem_limit_bytes=VMEM_LIMIT_BYTES,
    )


def _const_spec(shape):
    zeros = (0,) * len(shape)
    return pl.BlockSpec(shape, lambda *_: zeros, pipeline_mode=pl.Buffered(1))


def _dot(a, b):
    return jnp.dot(a, b, preferred_element_type=F32)


def _dot_nt(a, b):
    return lax.dot_general(a, b, (((1,), (1,)), ((), ())), preferred_element_type=F32)


def _dot_tn(a, b):
    return lax.dot_general(a, b, (((0,), (0,)), ((), ())), preferred_element_type=F32)


def _rms(x, g):
    return x * lax.rsqrt(jnp.mean(x * x, axis=-1, keepdims=True) + EPS) * g


def _swiglu(xn_bf, wgu_ref, wdn_ref, between=None):
    n_chunks = D_FF // FF_CHUNK

    def gate_up(c):
        lo, hi = c * FF_CHUNK, (c + 1) * FF_CHUNK
        return _dot(xn_bf, wgu_ref[:, lo:hi]), _dot(xn_bf, wgu_ref[:, D_FF + lo:D_FF + hi])

    acc = None
    nxt = gate_up(0)
    for c in range(n_chunks):
        g, u = nxt
        if c + 1 < n_chunks:
            nxt = gate_up(c + 1)
        a = (jax.nn.silu(g) * u).astype(BF16)
        d = _dot(a, wdn_ref[c * FF_CHUNK:(c + 1) * FF_CHUNK, :])
        acc = d if acc is None else acc + d
        if between is not None:
            between()
    return acc


def _log_sigmoid(z):
    return jnp.minimum(z, 0.0) - jnp.log1p(jnp.exp(-jnp.abs(z)))


def _pre_kernel(x_ref, g1_ref, wgu_ref, wdn_ref, gm_ref, win_ref, wzg_ref, wgate_ref, bgate_ref,
                h_ref, qk_ref, v_ref, la_ref, r_ref, us5_ref):
    x = x_ref[...]
    xn = _rms(x, g1_ref[...]).astype(BF16)
    h = x + 0.5 * _swiglu(xn, wgu_ref, wdn_ref)
    h_ref[...] = h
    un = _rms(h, gm_ref[...]).astype(BF16)
    p = _dot(un, win_ref[...])
    qk_ref[...] = p[:, 0:512]
    v_ref[...] = p[:, 512:1024]
    r_ref[...] = p[:, 1024:1536]
    us5_ref[...] = p[:, 1536:2048]
    zg = _dot(un, wzg_ref[...])
    z = _dot(zg.astype(BF16), wgate_ref[...]) + bgate_ref[...]
    la_ref[...] = _log_sigmoid(z) * (1.0 / GLA_TAU)


def _pre_call(x2d, g1, wgu, wdn, gm, win, wzg, wgate, bgate, tm):
    n = x2d.shape[0]
    row = lambda w: pl.BlockSpec((tm, w), lambda i: (i, 0))
    out_shapes = [jax.ShapeDtypeStruct((n, w), F32) for w in (D_MODEL, 512, 512, 256, 512, 512)]
    return pl.pallas_call(
        _pre_kernel,
        grid=(n // tm,),
        in_specs=[row(D_MODEL), _const_spec(g1.shape), _const_spec(wgu.shape), _const_spec(wdn.shape),
                  _const_spec(gm.shape), _const_spec(win.shape), _const_spec(wzg.shape),
                  _const_spec(wgate.shape), _const_spec(bgate.shape)],
        out_specs=[row(D_MODEL), row(512), row(512), row(256), row(512), row(512)],
        out_shape=out_shapes,
        compiler_params=_cparams(1),
        name="pre_ffn_proj",
    )(x2d, g1, wgu, wdn, gm, win, wzg, wgate, bgate)


def _memkv_kernel(m_ref, g_ref, w_ref, k_ref, v_ref, k4_ref, v4_ref):
    mn = _rms(m_ref[...], g_ref[...]).astype(BF16)
    kv = _dot(mn, w_ref[...])
    k_ref[...] = kv[:, :D_MODEL]
    v_ref[...] = kv[:, D_MODEL:]
    for hd in range(CROSS_HEADS):
        k4_ref[:, hd, :] = kv[:, hd * CROSS_DH:(hd + 1) * CROSS_DH]
        v4_ref[:, hd, :] = kv[:, D_MODEL + hd * CROSS_DH:D_MODEL + (hd + 1) * CROSS_DH]


def _memkv_call(mem2d, g, w, tm):
    n = mem2d.shape[0]
    row = pl.BlockSpec((tm, D_MODEL), lambda i: (i, 0))
    row4 = pl.BlockSpec((tm, CROSS_HEADS, CROSS_DH), lambda i: (i, 0, 0))
    return pl.pallas_call(
        _memkv_kernel,
        grid=(n // tm,),
        in_specs=[row, _const_spec(g.shape), _const_spec(w.shape)],
        out_specs=[row, row, row4, row4],
        out_shape=[jax.ShapeDtypeStruct((n, D_MODEL), F32)] * 2
        + [jax.ShapeDtypeStruct((n, CROSS_HEADS, CROSS_DH), F32)] * 2,
        compiler_params=_cparams(1),
        name="memory_kv",
    )(mem2d, g, w)


def _mid_kernel(h_ref, og_ref, os_ref, wo_ref, gc_ref, wq_ref, h2_ref, q_ref):
    h2 = (h_ref[...] + _dot(og_ref[...].astype(BF16), wo_ref[0:GLA_WIDTH, :])
          + _dot(os_ref[...].astype(BF16), wo_ref[GLA_WIDTH:, :]))
    h2_ref[...] = h2
    hn = _rms(h2, gc_ref[...]).astype(BF16)
    q_ref[...] = _dot(hn, wq_ref[...])


def _mid_call(h, og, os_, wo, gc, wq, tm):
    n = h.shape[0]
    row = lambda w: pl.BlockSpec((tm, w), lambda i: (i, 0))
    return pl.pallas_call(
        _mid_kernel,
        grid=(n // tm,),
        in_specs=[row(D_MODEL), row(512), row(512), _const_spec(wo.shape), _const_spec(gc.shape),
                  _const_spec(wq.shape)],
        out_specs=[row(D_MODEL), row(D_MODEL)],
        out_shape=[jax.ShapeDtypeStruct((n, D_MODEL), F32)] * 2,
        compiler_params=_cparams(1),
        name="mix_out_cross_q",
    )(h, og, os_, wo, gc, wq)


def _post_kernel(h2_ref, o_ref, wco_ref, g2_ref, wgu_ref, wdn_ref, gf_ref, y_ref):
    h3 = h2_ref[...] + _dot(o_ref[...].astype(BF16), wco_ref[...])
    hn = _rms(h3, g2_ref[...]).astype(BF16)
    h4 = h3 + 0.5 * _swiglu(hn, wgu_ref, wdn_ref)
    y_ref[...] = _rms(h4, gf_ref[...])


def _post_call(h2, o, wco, g2, wgu, wdn, gf, tm):
    n = h2.shape[0]
    row = pl.BlockSpec((tm, D_MODEL), lambda i: (i, 0))
    return pl.pallas_call(
        _post_kernel,
        grid=(n // tm,),
        in_specs=[row, row, _const_spec(wco.shape), _const_spec(g2.shape), _const_spec(wgu.shape),
                  _const_spec(wdn.shape), _const_spec(gf.shape)],
        out_specs=row,
        out_shape=jax.ShapeDtypeStruct((n, D_MODEL), F32),
        compiler_params=_cparams(1),
        name="post_ffn_final",
    )(h2, o, wco, g2, wgu, wdn, gf)


def _tail_kernel(h_ref, og_ref, os_ref, k_ref, v_ref, wo_ref, gc_ref, wq_ref, wco_ref, g2_ref, wgu_ref,
                 wdn_ref, gf_ref, y_ref):
    h2 = (h_ref[0] + _dot(og_ref[0].astype(BF16), wo_ref[0:GLA_WIDTH, :])
          + _dot(os_ref[0].astype(BF16), wo_ref[GLA_WIDTH:, :]))
    q = _dot(_rms(h2, gc_ref[...]).astype(BF16), wq_ref[...]).astype(BF16)
    heads = [slice(hd * CROSS_DH, (hd + 1) * CROSS_DH) for hd in range(CROSS_HEADS)]
    scores = [_dot_nt(q[:, sl], k_ref[0, :, sl].astype(BF16)) * (CROSS_DH ** -0.5) for sl in heads]
    probs = [_softmax(s).astype(BF16) for s in scores]
    outs = [_dot(p, v_ref[0, :, sl].astype(BF16)).astype(BF16) for p, sl in zip(probs, heads)]
    h3 = h2 + _dot(jnp.concatenate(outs, axis=-1), wco_ref[...])
    hn = _rms(h3, g2_ref[...]).astype(BF16)
    h4 = h3 + 0.5 * _swiglu(hn, wgu_ref, wdn_ref)
    y_ref[0] = _rms(h4, gf_ref[...])


def _tail_call(h3d, og3, os3, k3, v3, w, tm):
    s, t, _ = h3d.shape
    row = lambda wd: pl.BlockSpec((1, tm, wd), lambda i, j: (i, j, 0))
    kvspec = pl.BlockSpec((1, MEM_LEN, D_MODEL), lambda i, j: (i, 0, 0))
    consts = [w['w_out'], w['g_cross'], w['w_cq'], w['w_co'], w['g_ffn2'], w['w_ffn2_gu'], w['w_ffn2_down'],
              w['g_final']]
    return pl.pallas_call(
        _tail_kernel,
        grid=(s, t // tm),
        in_specs=[row(D_MODEL), row(GLA_WIDTH), row(S5_WIDTH), kvspec, kvspec] + [_const_spec(a.shape) for a in consts],
        out_specs=row(D_MODEL),
        out_shape=jax.ShapeDtypeStruct((s, t, D_MODEL), F32),
        compiler_params=_cparams(2),
        name="tail_attn_ffn",
    )(h3d, og3, os3, k3, v3, *consts)


def _softmax(s):
    e = jnp.exp(s - jnp.max(s, axis=-1, keepdims=True))
    return e / jnp.sum(e, axis=-1, keepdims=True)


def _load_mem(ref, j):
    if len(ref.shape) == 3:
        return ref[j].astype(BF16)
    return jnp.concatenate([ref[j, :, hd, :] for hd in range(CROSS_HEADS)], axis=-1).astype(BF16)


def _attn_packed_kernel(q_ref, k_ref, v_ref, o_ref, *, n_seq):
    t = q_ref.shape[1]
    shape = (CROSS_HEADS * t, D_MODEL)
    q_mask = (_iota2(shape, 0) // t) == (_iota2(shape, 1) // CROSS_DH)
    o_lane_head = _iota2((t, D_MODEL), 1) // CROSS_DH
    scores = []
    for j in range(n_seq):
        q_stack = jnp.where(q_mask, jnp.concatenate([q_ref[j]] * CROSS_HEADS, axis=0), 0.0).astype(BF16)
        scores.append(_dot_nt(q_stack, _load_mem(k_ref, j)) * (CROSS_DH ** -0.5))
    probs = [_softmax(s).astype(BF16) for s in scores]
    for j in range(n_seq):
        full = _dot(probs[j], _load_mem(v_ref, j))
        o = jnp.zeros((t, D_MODEL), F32)
        for hd in range(CROSS_HEADS):
            o = jnp.where(o_lane_head == hd, full[hd * t:(hd + 1) * t, :], o)
        o_ref[j] = o.astype(o_ref.dtype)


def _attn_kernel(q_ref, k_ref, v_ref, o_ref, *, n_seq):
    cast = lambda a: a.astype(BF16)
    for j in range(n_seq):
        q = q_ref[j]
        outs = []
        for hd in range(CROSS_HEADS):
            sl = slice(hd * CROSS_DH, (hd + 1) * CROSS_DH)
            s = _dot_nt(cast(q[:, sl]), cast(k_ref[j, :, sl])) * (CROSS_DH ** -0.5)
            p = _softmax(s)
            outs.append(_dot(cast(p), cast(v_ref[j, :, sl])))
        o_ref[j] = jnp.concatenate(outs, axis=-1).astype(o_ref.dtype)


def _attn_call(q3, k3, v3, n_seq, tq):
    s, t, _ = q3.shape
    qspec = pl.BlockSpec((n_seq, tq, D_MODEL), lambda i, j: (i, j, 0))
    if k3.ndim == 3:
        kvspec = pl.BlockSpec((n_seq, MEM_LEN, D_MODEL), lambda i, j: (i, 0, 0))
    else:
        kvspec = pl.BlockSpec((n_seq, MEM_LEN, CROSS_HEADS, CROSS_DH), lambda i, j: (i, 0, 0, 0))
    return pl.pallas_call(
        functools.partial(_attn_packed_kernel if tq * CROSS_HEADS <= MXU_DIM else _attn_kernel, n_seq=n_seq),
        grid=(s // n_seq, t // tq),
        in_specs=[qspec, kvspec, kvspec],
        out_specs=qspec,
        out_shape=jax.ShapeDtypeStruct(q3.shape, BF16 if tq % 16 == 0 else F32),
        compiler_params=_cparams(2),
        name="cross_attention",
    )(q3, k3, v3)


def _split3(x):
    hi = x.astype(BF16)
    r1 = x - hi.astype(F32)
    mid = r1.astype(BF16)
    lo = (r1 - mid.astype(F32)).astype(BF16)
    return hi, mid, lo


def _dot01(m01, x):
    hi, mid, lo = _split3(x)
    return _dot(m01, hi) + _dot(m01, mid) + _dot(m01, lo)


def _iota2(shape, dim):
    return lax.broadcasted_iota(jnp.int32, shape, dim)


def _gla_stages(qk_ref, v_ref, la_ref, r_ref, o_ref, g_ref, tri_ref, lvl_ref, khm_ref, vhm_ref, shm_ref,
                states, put_states, *, seq_len, n_chunks):
    c_rows = GLA_CHUNK_ROWS
    n_sub = c_rows // seq_len
    log2 = lambda n: int(math.log2(n))
    tri = tri_ref[...]
    lvl = lvl_ref[...]
    rowq = _iota2((c_rows, GLA_QK_WIDTH), 0)
    u_head = [(_iota2((GLA_DV, GLA_QK_WIDTH), 1) >> log2(GLA_DK)) == hd for hd in range(GLA_HEADS)]

    def head_blocks(x, mask_ref):
        return jnp.concatenate([x.astype(BF16)] * GLA_HEADS, axis=0) * mask_ref[...]

    g_out = g_ref[...]
    chunks = range(n_chunks)
    rows_of = lambda c: slice(c * c_rows, (c + 1) * c_rows)
    sub_of = lambda j: slice(j * seq_len, (j + 1) * seq_len)
    q = [qk_ref[rows_of(c), 0:GLA_QK_WIDTH] * (GLA_DK ** -0.5) for c in chunks]
    k = [qk_ref[rows_of(c), GLA_QK_WIDTH:2 * GLA_QK_WIDTH] for c in chunks]
    v = [v_ref[rows_of(c), :] for c in chunks]
    b = [_dot01(tri, la_ref[rows_of(c), :]) for c in chunks]
    yield

    a_mat = [jnp.where(lvl == 0, _dot_nt(q[c].astype(BF16), head_blocks(k[c], khm_ref)), 0.0) for c in chunks]
    last = list(b)
    m = 1
    while m < seq_len:
        yield
        even = ((rowq >> log2(m)) & 1) == 0
        for c in chunks:
            prev_last = pltpu.roll(last[c], m, axis=0)
            e = jnp.exp(jnp.minimum(jnp.where(even, last[c] - b[c], b[c] - prev_last), 0.0))
            a_lvl = _dot_nt((q[c] * e).astype(BF16), head_blocks(k[c] * e, khm_ref))
            a_mat[c] = jnp.where(lvl == log2(m) + 1, a_lvl, a_mat[c])
            last[c] = jnp.where(even, pltpu.roll(last[c], c_rows - m, axis=0), last[c])
        m *= 2
    yield
    o_intra = [_dot(a_mat[c].astype(BF16), head_blocks(v[c], vhm_ref)) for c in chunks]
    yield

    narrow = (lambda a: a.astype(BF16)) if seq_len % 16 == 0 else (lambda a: a)
    q_dec = [narrow(q[c] * jnp.exp(b[c])) for c in chunks]
    k_dec = [narrow(k[c] * jnp.exp(jnp.minimum(last[c] - b[c], 0.0))) for c in chunks]
    u_t = {}
    for c in chunks:
        vc = narrow(v[c])
        for j in range(n_sub):
            u_full = _dot_tn(vc[sub_of(j)], k_dec[c][sub_of(j)])
            acc = jnp.zeros((GLA_DV, GLA_QK_WIDTH), F32)
            for hd in range(GLA_HEADS):
                acc = jnp.where(u_head[hd], u_full[hd * GLA_DV:(hd + 1) * GLA_DV, :], acc)
            u_t[c, j] = acc

    states = list(states)
    state_of = (lambda c, j: c * n_sub + j) if n_sub > 1 else (lambda c, j: 0)
    for c in chunks:
        if c % 2 == 0:
            yield
        o_parts = []
        for j in range(n_sub):
            sj = state_of(c, j)
            o_parts.append(_dot_nt(q_dec[c][sub_of(j)], head_blocks(states[sj], shm_ref)))
            decay = jnp.exp(last[c][j * seq_len:j * seq_len + 1, :])
            states[sj] = states[sj] * decay + u_t[c, j]
        o_inter = o_parts[0] if n_sub == 1 else jnp.concatenate(o_parts, axis=0)
        o = o_intra[c] + o_inter
        r = r_ref[rows_of(c), :]
        outs = []
        for hd in range(GLA_HEADS):
            sl = slice(hd * GLA_DV, (hd + 1) * GLA_DV)
            outs.append(_rms(o[:, sl], g_out) * jax.nn.silu(r[:, sl]))
        o_ref[rows_of(c), :] = jnp.concatenate(outs, axis=-1).astype(o_ref.dtype)
    put_states(states)


def _gla_kernel(*refs, seq_len, n_chunks, n_state, has_s0):
    ins, rest = refs[:10], refs[10:]
    qk_ref, v_ref, la_ref, r_ref, g_ref = ins[:5]
    if has_s0:
        s0_ref, o_ref, st_ref, s_scr = rest
    else:
        o_ref, st_ref, s_scr = rest
        s0_ref = None
    t_idx = pl.program_id(1)

    @pl.when(t_idx == 0)
    def _():
        if has_s0:
            for j in range(n_state):
                s_scr[j] = s0_ref[j].T
        else:
            s_scr[...] = jnp.zeros_like(s_scr)

    def put_states(states):
        for j, s_t in enumerate(states):
            s_scr[j] = s_t

    for _ in _gla_stages(qk_ref.at[0], v_ref.at[0], la_ref.at[0], r_ref.at[0], o_ref.at[0], g_ref, *ins[5:],
                         [s_scr[j] for j in range(n_state)], put_states, seq_len=seq_len, n_chunks=n_chunks):
        pass

    @pl.when(t_idx == pl.num_programs(1) - 1)
    def _():
        for j in range(n_state):
            st_ref[j] = s_scr[j].T


def _gla_index_constants(seq_len):
    c = GLA_CHUNK_ROWS
    i = np.arange(c)[:, None]
    s = np.arange(c)[None, :]
    same_seq = (i // seq_len) == (s // seq_len)
    tri = (same_seq & (s <= i)).astype(np.float32)
    top_bit = np.floor(np.log2(np.maximum(i ^ s, 1))).astype(np.int32)
    lvl = np.where(same_seq & (s < i), top_bit + 1, -1)
    lvl = np.where(i == s, 0, lvl).astype(np.int32)
    head_mask = lambda rows_per_head, lanes_per_head: (
        (np.arange(GLA_HEADS * rows_per_head)[:, None] // rows_per_head)
        == (np.arange(GLA_HEADS * lanes_per_head)[None, :] // lanes_per_head)).astype(np.float32)
    return [jnp.asarray(tri, BF16), jnp.asarray(np.tile(lvl, (1, GLA_HEADS))),
            jnp.asarray(head_mask(c, GLA_DK), BF16), jnp.asarray(head_mask(c, GLA_DV), BF16),
            jnp.asarray(head_mask(GLA_DV, GLA_DK), BF16)]


def _gla_call(qk3, v3, la3, r3, g_out, s0, seq_len, tile_rows, n_state):
    n_outer, rows, _ = qk3.shape
    n_chunks = tile_rows // GLA_CHUNK_ROWS
    spec = lambda w: pl.BlockSpec((1, tile_rows, w), lambda i, t: (i, t, 0))
    st_spec = pl.BlockSpec((n_state, GLA_QK_WIDTH, GLA_DV), lambda i, t: (i, 0, 0))
    has_s0 = s0 is not None
    consts = [g_out] + _gla_index_constants(seq_len)
    in_specs = [spec(512), spec(512), spec(256), spec(512)] + [_const_spec(a.shape) for a in consts]
    args = [qk3, v3, la3, r3] + consts
    if has_s0:
        in_specs.append(st_spec)
        args.append(s0)
    return pl.pallas_call(
        functools.partial(_gla_kernel, seq_len=seq_len, n_chunks=n_chunks, n_state=n_state, has_s0=has_s0),
        grid=(n_outer, rows // tile_rows),
        in_specs=in_specs,
        out_specs=[spec(512), st_spec],
        out_shape=[jax.ShapeDtypeStruct((n_outer, rows, GLA_WIDTH), BF16),
                   jax.ShapeDtypeStruct((n_outer * n_state, GLA_QK_WIDTH, GLA_DV), F32)],
        scratch_shapes=[pltpu.VMEM((n_state, GLA_DV, GLA_QK_WIDTH), F32)],
        compiler_params=_cparams(2),
        name="gla_mixer",
    )(*args)


def _pre_gla_kernel(x_ref, g1_ref, wgu_ref, wdn_ref, gm_ref, win_ref, wzg_ref, wgate_ref, bgate_ref, gout_ref,
                    tri_ref, lvl_ref, khm_ref, vhm_ref, shm_ref,
                    h_ref, us5_ref, og_ref, st_ref,
                    qk_scr, v_scr, la_scr, r_scr, s_scr, *, tiles_per_seq):
    i = pl.program_id(0)
    last_step = pl.num_programs(0) - 1
    n_chunks = x_ref.shape[0] // GLA_CHUNK_ROWS

    @pl.when(i == 0)
    def _():
        for scr in (qk_scr, v_scr, la_scr, r_scr, s_scr):
            scr[...] = jnp.zeros_like(scr)

    def gla_stages():
        first_of_seq = lax.rem(i + (tiles_per_seq - 1), tiles_per_seq) == 0
        state = jnp.where(first_of_seq, 0.0, s_scr[0])

        def put_states(states):
            s_scr[0] = states[0]
            st_ref[0] = states[0].T

        return _gla_stages(qk_scr, v_scr, la_scr, r_scr, og_ref, gout_ref, tri_ref, lvl_ref, khm_ref, vhm_ref,
                           shm_ref, [state], put_states, seq_len=GLA_CHUNK_ROWS, n_chunks=n_chunks)

    @pl.when(i < last_step)
    def _():
        stages = gla_stages()
        next(stages)
        x = x_ref[...]
        xn = _rms(x, g1_ref[...]).astype(BF16)
        h = x + 0.5 * _swiglu(xn, wgu_ref, wdn_ref, between=lambda: next(stages, None))
        h_ref[...] = h
        un = _rms(h, gm_ref[...]).astype(BF16)
        p = []
        for grp in range(4):
            p.append(_dot(un, win_ref[:, grp * 512:(grp + 1) * 512]))
            next(stages, None)
        zg = _dot(un, wzg_ref[...])
        z = _dot(zg.astype(BF16), wgate_ref[...]) + bgate_ref[...]
        for _ in stages:
            pass
        qk_scr[...] = p[0]
        v_scr[...] = p[1]
        r_scr[...] = p[2]
        us5_ref[...] = p[3]
        la_scr[...] = _log_sigmoid(z) * (1.0 / GLA_TAU)

    @pl.when(i == last_step)
    def _():
        for _ in gla_stages():
            pass


def _pre_gla_call(x2d, w, tm, seq_rows):
    n = x2d.shape[0]
    n_tiles = n // tm
    tiles_per_seq = seq_rows // tm
    cur = lambda wd: pl.BlockSpec((tm, wd), lambda i: (jnp.minimum(i, n_tiles - 1), 0))
    prev = lambda wd: pl.BlockSpec((tm, wd), lambda i: (jnp.maximum(i - 1, 0), 0))
    st_spec = pl.BlockSpec((1, GLA_QK_WIDTH, GLA_DV), lambda i: (jnp.maximum(i - 1, 0) // tiles_per_seq, 0, 0))
    consts = [w['g_ffn1'], w['w_ffn1_gu'], w['w_ffn1_down'], w['g_mix'], w['w_in_main'], w['w_zg'], w['w_gate'],
              w['b_gate'], w['g_gla_out']] + _gla_index_constants(GLA_CHUNK_ROWS)
    return pl.pallas_call(
        functools.partial(_pre_gla_kernel, tiles_per_seq=tiles_per_seq),
        grid=(n_tiles + 1,),
        in_specs=[cur(D_MODEL)] + [_const_spec(a.shape) for a in consts],
        out_specs=[cur(D_MODEL), cur(S5_WIDTH), prev(GLA_WIDTH), st_spec],
        out_shape=[jax.ShapeDtypeStruct((n, D_MODEL), F32), jax.ShapeDtypeStruct((n, S5_WIDTH), F32),
                   jax.ShapeDtypeStruct((n, GLA_WIDTH), BF16),
                   jax.ShapeDtypeStruct((n // seq_rows, GLA_QK_WIDTH, GLA_DV), F32)],
        scratch_shapes=[pltpu.VMEM((tm, 2 * GLA_QK_WIDTH), F32), pltpu.VMEM((tm, GLA_WIDTH), F32),
                        pltpu.VMEM((tm, GLA_QK_WIDTH), F32), pltpu.VMEM((tm, GLA_WIDTH), F32),
                        pltpu.VMEM((1, GLA_DV, GLA_QK_WIDTH), F32)],
        compiler_params=_cparams(1),
        name="pre_ffn_gla",
    )(x2d, *consts)


def _s5_disc_kernel(lre_ref, lim_ref, ldt_ref, bre_ref, bim_ref, are_ref, aim_ref, bbre_ref, bbim_ref):
    lr = lre_ref[...]
    li = lim_ref[...]
    dt = jnp.exp(ldt_ref[...])
    mag = jnp.exp(dt * lr)
    ab_re = mag * jnp.cos(dt * li)
    ab_im = mag * jnp.sin(dt * li)
    den = lr * lr + li * li
    coef_re = ((ab_re - 1.0) * lr + ab_im * li) / den
    coef_im = (ab_im * lr - (ab_re - 1.0) * li) / den
    are_ref[...] = ab_re
    aim_ref[...] = ab_im
    br = bre_ref[...]
    bi = bim_ref[...]
    cr = coef_re[:, None, :]
    cim = coef_im[:, None, :]
    bbre_ref[...] = cr * br - cim * bi
    bbim_ref[...] = cr * bi + cim * br


def _s5_disc_call(lre, lim, ldt, bre_t, bim_t):
    gn = jax.ShapeDtypeStruct((S5_GROUPS, S5_N), F32)
    gpn = jax.ShapeDtypeStruct((S5_GROUPS, S5_P, S5_N), F32)
    return pl.pallas_call(
        _s5_disc_kernel,
        out_shape=[gn, gn, gpn, gpn],
        name="s5_discretise",
    )(lre, lim, ldt, bre_t, bim_t)


def _to_time_major(u_ref, perm_scr, n_seq, t0, nt):
    if n_seq == SUBLANES:
        n_slab = S5_WIDTH // LANES
        for j in range(n_seq):
            for sb in range(n_slab):
                perm_scr[sb, pl.ds(t0 * n_seq + j, nt, stride=n_seq), :] = (
                    u_ref[j, t0:t0 + nt, sb * LANES:(sb + 1) * LANES])
        return jnp.concatenate([perm_scr[sb, t0 * n_seq:(t0 + nt) * n_seq, :] for sb in range(n_slab)], axis=-1)
    return jnp.concatenate([u_ref[:, t, :] for t in range(t0, t0 + nt)], axis=0)


def _from_time_major(y, o_ref, perm_scr, n_seq, t0, nt):
    if n_seq == SUBLANES:
        n_slab = S5_WIDTH // LANES
        for sb in range(n_slab):
            perm_scr[sb, t0 * n_seq:(t0 + nt) * n_seq, :] = y[:, sb * LANES:(sb + 1) * LANES]
        for j in range(n_seq):
            for sb in range(n_slab):
                o_ref[j, t0:t0 + nt, sb * LANES:(sb + 1) * LANES] = (
                    perm_scr[sb, pl.ds(t0 * n_seq + j, nt, stride=n_seq), :])
    else:
        for t in range(nt):
            o_ref[:, t0 + t, :] = y[t * n_seq:(t + 1) * n_seq, :]


def _s5_kernel(u_ref, x0re_ref, x0im_ref, are_ref, aim_ref, bre_ref, bim_ref, c_ref, d_ref, wglu_ref,
               bglu_ref, o_ref, xre_ref, xim_ref, sre_scr, sim_scr, xre_scr, xim_scr, perm_scr, *, n_seq, t_tile):
    half_w = S5_WIDTH // 2
    half_s = S5_STATE // 2

    @pl.when(pl.program_id(0) == 0)
    def _():
        xre_scr[...] = x0re_ref[...]
        xim_scr[...] = x0im_ref[...]

    nt = t_tile
    rows = nt * n_seq
    for t0 in (0,):
        u = _to_time_major(u_ref, perm_scr, n_seq, t0, nt)
        ub = u.astype(BF16)
        for hf in range(2):
            uh = ub[:, hf * half_w:(hf + 1) * half_w]
            st = slice(hf * half_s, (hf + 1) * half_s)
            sre_scr[t0:t0 + nt, :, st] = _dot(uh, bre_ref[hf]).reshape(nt, n_seq, half_s)
            sim_scr[t0:t0 + nt, :, st] = _dot(uh, bim_ref[hf]).reshape(nt, n_seq, half_s)

        for lc in range(S5_STATE // S5_LANE_CHUNK):
            lanes = slice(lc * S5_LANE_CHUNK, (lc + 1) * S5_LANE_CHUNK)
            ar = jnp.broadcast_to(are_ref[:, lanes], (SUBLANES, S5_LANE_CHUNK))
            ai = jnp.broadcast_to(aim_ref[:, lanes], (SUBLANES, S5_LANE_CHUNK))
            for sc in range(n_seq // SUBLANES):
                srows = slice(sc * SUBLANES, (sc + 1) * SUBLANES)
                xr = xre_scr[srows, lanes]
                xi = xim_scr[srows, lanes]
                for t in range(t0, t0 + nt):
                    nxr = ar * xr - ai * xi + sre_scr[t, srows, lanes]
                    nxi = ar * xi + ai * xr + sim_scr[t, srows, lanes]
                    sre_scr[t, srows, lanes] = nxr
                    sim_scr[t, srows, lanes] = nxi
                    xr, xi = nxr, nxi
                xre_scr[srows, lanes] = xr
                xim_scr[srows, lanes] = xi

        xr_all = sre_scr[t0:t0 + nt].reshape(rows, S5_STATE).astype(BF16)
        xi_all = sim_scr[t0:t0 + nt].reshape(rows, S5_STATE).astype(BF16)
        ys = []
        for hf in range(2):
            st = slice(hf * half_s, (hf + 1) * half_s)
            ys.append(_dot(xr_all[:, st], c_ref[hf, 0:half_s, :]) + _dot(xi_all[:, st], c_ref[hf, half_s:, :]))
        y = jnp.concatenate(ys, axis=-1) + d_ref[...] * u
        z = jax.nn.gelu(y)
        gate = jax.nn.sigmoid(_dot(z.astype(BF16), wglu_ref[...]) + bglu_ref[...])
        _from_time_major(z * gate, o_ref, perm_scr, n_seq, t0, nt)

    @pl.when(pl.program_id(0) == pl.num_programs(0) - 1)
    def _():
        xre_ref[...] = xre_scr[...]
        xim_ref[...] = xim_scr[...]


def _s5_call(u3, x0re, x0im, are, aim, bre, bim, cmat, dskip, wglu, bglu, t_tile):
    n_seq, t, _ = u3.shape
    uspec = pl.BlockSpec((n_seq, t_tile, S5_WIDTH), lambda i: (0, i, 0))
    perm_rows = t_tile * n_seq if n_seq == SUBLANES else SUBLANES
    xspec = pl.BlockSpec((n_seq, S5_STATE), lambda i: (0, 0))
    consts = [are, aim, bre, bim, cmat, dskip, wglu, bglu]
    return pl.pallas_call(
        functools.partial(_s5_kernel, n_seq=n_seq, t_tile=t_tile),
        grid=(t // t_tile,),
        in_specs=[uspec, xspec, xspec] + [_const_spec(a.shape) for a in consts],
        out_specs=[uspec, xspec, xspec],
        out_shape=[jax.ShapeDtypeStruct((n_seq, t, S5_WIDTH), F32),
                   jax.ShapeDtypeStruct((n_seq, S5_STATE), F32),
                   jax.ShapeDtypeStruct((n_seq, S5_STATE), F32)],
        scratch_shapes=[pltpu.VMEM((t_tile, n_seq, S5_STATE), F32), pltpu.VMEM((t_tile, n_seq, S5_STATE), F32),
                        pltpu.VMEM((n_seq, S5_STATE), F32), pltpu.VMEM((n_seq, S5_STATE), F32),
                        pltpu.VMEM((S5_WIDTH // LANES, perm_rows, LANES), F32)],
        compiler_params=_cparams(1),
        name="s5_mixer",
    )(u3, x0re, x0im, *consts)


def _block_diag(blocks):
    g, a, b = blocks.shape
    eye = jnp.eye(g, dtype=blocks.dtype)
    return (eye[:, None, :, None] * blocks[:, :, None, :]).reshape(g * a, g * b)


def _halves_block_diag(blocks):
    g = blocks.shape[0]
    return jnp.stack([_block_diag(blocks[:g // 2]), _block_diag(blocks[g // 2:])])


def _trunk_group(x3, s_gla0, x0re, x0im, mem_k3, mem_v3, w, *, tm, gla_tile, gla_states, s5_t_tile,
                 attn_seqs, attn_tq):
    s, t, _ = x3.shape
    n = s * t
    if t >= tm and s_gla0 is None:
        h, us5, o_gla, st_new = _pre_gla_call(x3.reshape(n, D_MODEL), w, tm, t)
    else:
        h, qk, v, la, r, us5 = _pre_call(x3.reshape(n, D_MODEL), w['g_ffn1'], w['w_ffn1_gu'], w['w_ffn1_down'],
                                         w['g_mix'], w['w_in_main'], w['w_zg'], w['w_gate'], w['b_gate'], tm)
        if gla_states == 1:
            shp = lambda a: a.reshape(s, t, a.shape[-1])
        else:
            rows = gla_states * t
            shp = lambda a: a.reshape(n // rows, rows, a.shape[-1])
        o_gla, st_new = _gla_call(shp(qk), shp(v), shp(la), shp(r), w['g_gla_out'], s_gla0,
                                  t if gla_states > 1 else GLA_CHUNK_ROWS, gla_tile, gla_states)
    o_s5, xre, xim = _s5_call(us5.reshape(s, t, S5_WIDTH), x0re, x0im, w['s5_are'], w['s5_aim'], w['s5_bre'],
                              w['s5_bim'], w['s5_c'], w['s5_d'], w['s5_wglu'], w['s5_bglu'], s5_t_tile)
    if t >= tm:
        y = _tail_call(h.reshape(s, t, D_MODEL), o_gla.reshape(s, t, GLA_WIDTH), o_s5, mem_k3, mem_v3, w, tm)
        return y, st_new, xre, xim
    o_s5 = o_s5.reshape(n, S5_WIDTH)
    h2, q = _mid_call(h, o_gla.reshape(n, GLA_WIDTH), o_s5, w['w_out'], w['g_cross'], w['w_cq'], tm)
    o_att = _attn_call(q.reshape(s, t, D_MODEL), mem_k3, mem_v3, attn_seqs, attn_tq)
    y = _post_call(h2, o_att.reshape(n, D_MODEL), w['w_co'], w['g_ffn2'], w['w_ffn2_gu'], w['w_ffn2_down'],
                   w['g_final'], tm)
    return y.reshape(s, t, D_MODEL), st_new, xre, xim


def kernel(x_prompt, x_sample, state_gla, state_s5_re, state_s5_im, cache_mem_k, cache_mem_v, mem_prompt, g_ffn1, w_ffn1_gu, w_ffn1_down, g_mix, w_in, w_gla_gate, b_gla_gate, g_gla_out, s5_lambda_re, s5_lambda_im, s5_log_dt, s5_b_re, s5_b_im, s5_c_re, s5_c_im, s5_d, s5_w_glu, s5_b_glu, w_out, g_cross, g_mem, w_cq, w_ckv, w_co, g_ffn2, w_ffn2_gu, w_ffn2_down, g_final):
    bp, tp, _ = x_prompt.shape
    bs, ts, _ = x_sample.shape
    l = 0
    row = lambda a: a.reshape(1, -1).astype(F32)
    bf = lambda a: a.astype(BF16)

    are, aim, bbre, bbim = _s5_disc_call(
        s5_lambda_re[l], s5_lambda_im[l], s5_log_dt[l].reshape(S5_GROUPS, 1),
        jnp.transpose(s5_b_re[l], (0, 2, 1)), jnp.transpose(s5_b_im[l], (0, 2, 1)))
    c_re_t = jnp.transpose(s5_c_re[l], (0, 2, 1))
    c_im_t = jnp.transpose(s5_c_im[l], (0, 2, 1))
    cre_h = _halves_block_diag(c_re_t)
    cim_h = _halves_block_diag(-c_im_t)
    w_in_l = w_in[l]
    zg_lo = 2 * GLA_QK_WIDTH + GLA_WIDTH
    zg_hi = zg_lo + GLA_GATE_RANK
    w = {
        'g_ffn1': row(g_ffn1[l]), 'w_ffn1_gu': bf(w_ffn1_gu[l]), 'w_ffn1_down': bf(w_ffn1_down[l]),
        'g_mix': row(g_mix[l]),
        'w_in_main': bf(jnp.concatenate([w_in_l[:, :zg_lo], w_in_l[:, zg_hi:]], axis=1)),
        'w_zg': bf(jnp.pad(w_in_l[:, zg_lo:zg_hi], ((0, 0), (0, LANES - GLA_GATE_RANK)))),
        'w_gate': bf(jnp.pad(w_gla_gate[l], ((0, LANES - GLA_GATE_RANK), (0, 0)))),
        'b_gate': row(b_gla_gate[l]),
        'g_gla_out': row(g_gla_out[l]),
        's5_are': are.reshape(1, S5_STATE), 's5_aim': aim.reshape(1, S5_STATE),
        's5_bre': bf(_halves_block_diag(bbre)), 's5_bim': bf(_halves_block_diag(bbim)),
        's5_c': bf(jnp.concatenate([cre_h, cim_h], axis=1)),
        's5_d': row(s5_d[l]), 's5_wglu': bf(_block_diag(s5_w_glu[l])), 's5_bglu': row(s5_b_glu[l]),
        'w_out': bf(w_out[l]), 'g_cross': row(g_cross[l]), 'w_cq': bf(w_cq[l]), 'w_co': bf(w_co[l]),
        'g_ffn2': row(g_ffn2[l]), 'w_ffn2_gu': bf(w_ffn2_gu[l]), 'w_ffn2_down': bf(w_ffn2_down[l]),
        'g_final': row(g_final),
    }

    mk, mv, mk4, mv4 = _memkv_call(mem_prompt.reshape(bp * MEM_LEN, D_MODEL), row(g_mem[l]), bf(w_ckv[l]), 512)
    zeros_state = jnp.zeros((bp, S5_STATE), F32)
    y_p, st_p, re_p, im_p = _trunk_group(
        x_prompt, None, zeros_state, zeros_state, mk.reshape(bp, MEM_LEN, D_MODEL), mv.reshape(bp, MEM_LEN, D_MODEL),
        w, tm=512, gla_tile=256, gla_states=1, s5_t_tile=128, attn_seqs=1, attn_tq=512)

    seqs_per_chunk = GLA_CHUNK_ROWS // ts
    s0 = state_gla.reshape(DEPTH * bs, GLA_QK_WIDTH, GLA_DV)[l * bs:(l + 1) * bs]
    y_s, st_s, re_s, im_s = _trunk_group(
        x_sample, s0, state_s5_re[l].reshape(bs, S5_STATE), state_s5_im[l].reshape(bs, S5_STATE),
        cache_mem_k.reshape(DEPTH * bs, MEM_LEN, CROSS_HEADS, CROSS_DH)[l * bs:(l + 1) * bs],
        cache_mem_v.reshape(DEPTH * bs, MEM_LEN, CROSS_HEADS, CROSS_DH)[l * bs:(l + 1) * bs],
        w, tm=512, gla_tile=4 * GLA_CHUNK_ROWS, gla_states=4 * seqs_per_chunk, s5_t_tile=ts, attn_seqs=4, attn_tq=ts)

    def gla_state_out(st, nb):
        return st.reshape(1, nb, GLA_HEADS, GLA_DK, GLA_DV)

    s5_out = lambda a, nb: a.reshape(1, nb, S5_GROUPS, S5_N)
    kv_out = lambda a: a.reshape(1, bp, MEM_LEN, CROSS_HEADS, CROSS_DH)
    return (y_p, y_s,
            gla_state_out(st_p, bp), s5_out(re_p, bp), s5_out(im_p, bp), kv_out(mk4), kv_out(mv4),
            gla_state_out(st_s, bs), s5_out(re_s, bs), s5_out(im_s, bs))
```

```python
import functools
import math

import numpy as np
import jax
import jax.numpy as jnp
from jax import lax
from jax.experimental import pallas as pl
from jax.experimental.pallas import tpu as pltpu

F32 = jnp.float32
BF16 = jnp.bfloat16

D_MODEL = 1024
DEPTH = 1
EPS = 1e-6
D_FF = 2816
GLA_WIDTH = 512
GLA_HEADS = 4
GLA_DV = 128
GLA_DK = 64
GLA_QK_WIDTH = 256
GLA_GATE_RANK = 16
GLA_TAU = 16.0
S5_WIDTH = 512
S5_P = 16
S5_GROUPS = 32
S5_N = 64
S5_STATE = S5_GROUPS * S5_N
CROSS_HEADS = 4
CROSS_DH = 256
MEM_LEN = 256

LANES = 128
SUBLANES = 8
MXU_DIM = 256
VMEM_LIMIT_BYTES = 56 * 1024 * 1024

ROW_TILE = 512
S5_TILE_STEPS = 128
SHORT_GLA_CHUNKS = 4
SHORT_ATTN_SEQS = 4
FF_CHUNK = MXU_DIM
GLA_CHUNK_ROWS = 64
S5_LANE_CHUNK = 512


def _cparams(n_grid_axes):
    return pltpu.CompilerParams(
        dimension_semantics=("arbitrary",) * n_grid_axes,
        vmem_limit_bytes=VMEM_LIMIT_BYTES,
    )


def _const_spec(shape):
    zeros = (0,) * len(shape)
    return pl.BlockSpec(shape, lambda *_: zeros, pipeline_mode=pl.Buffered(1))


def _dot(a, b):
    return jnp.dot(a, b, preferred_element_type=F32)


def _dot_nt(a, b):
    return lax.dot_general(a, b, (((1,), (1,)), ((), ())), preferred_element_type=F32)


def _dot_tn(a, b):
    return lax.dot_general(a, b, (((0,), (0,)), ((), ())), preferred_element_type=F32)


def _rms(x, g):
    return x * lax.rsqrt(jnp.mean(x * x, axis=-1, keepdims=True) + EPS) * g


def _swiglu(xn_bf, wgu_ref, wdn_ref, between=None):
    n_chunks = D_FF // FF_CHUNK

    def gate_up(c):
        lo, hi = c * FF_CHUNK, (c + 1) * FF_CHUNK
        return _dot(xn_bf, wgu_ref[:, lo:hi]), _dot(xn_bf, wgu_ref[:, D_FF + lo:D_FF + hi])

    acc = None
    nxt = gate_up(0)
    for c in range(n_chunks):
        g, u = nxt
        if c + 1 < n_chunks:
            nxt = gate_up(c + 1)
        a = (jax.nn.silu(g) * u).astype(BF16)
        d = _dot(a, wdn_ref[c * FF_CHUNK:(c + 1) * FF_CHUNK, :])
        acc = d if acc is None else acc + d
        if between is not None:
            between()
    return acc


def _log_sigmoid(z):
    return jnp.minimum(z, 0.0) - jnp.log1p(jnp.exp(-jnp.abs(z)))


def _pre_kernel(x_ref, g1_ref, wgu_ref, wdn_ref, gm_ref, win_ref, wzg_ref, wgate_ref, bgate_ref,
                h_ref, qk_ref, v_ref, la_ref, r_ref, us5_ref):
    x = x_ref[...]
    xn = _rms(x, g1_ref[...]).astype(BF16)
    h = x + 0.5 * _swiglu(xn, wgu_ref, wdn_ref)
    h_ref[...] = h
    un = _rms(h, gm_ref[...]).astype(BF16)
    p = _dot(un, win_ref[...])
    qk_ref[...] = p[:, 0:512]
    v_ref[...] = p[:, 512:1024]
    r_ref[...] = p[:, 1024:1536]
    us5_ref[...] = p[:, 1536:2048]
    zg = _dot(un, wzg_ref[...])
    z = _dot(zg.astype(BF16), wgate_ref[...]) + bgate_ref[...]
    la_ref[...] = _log_sigmoid(z) * (1.0 / GLA_TAU)


def _pre_call(x2d, g1, wgu, wdn, gm, win, wzg, wgate, bgate, tm):
    n = x2d.shape[0]
    row = lambda w: pl.BlockSpec((tm, w), lambda i: (i, 0))
    out_shapes = [jax.ShapeDtypeStruct((n, w), F32) for w in (D_MODEL, 512, 512, 256, 512, 512)]
    return pl.pallas_call(
        _pre_kernel,
        grid=(n // tm,),
        in_specs=[row(D_MODEL), _const_spec(g1.shape), _const_spec(wgu.shape), _const_spec(wdn.shape),
                  _const_spec(gm.shape), _const_spec(win.shape), _const_spec(wzg.shape),
                  _const_spec(wgate.shape), _const_spec(bgate.shape)],
        out_specs=[row(D_MODEL), row(512), row(512), row(256), row(512), row(512)],
        out_shape=out_shapes,
        compiler_params=_cparams(1),
        name="pre_ffn_proj",
    )(x2d, g1, wgu, wdn, gm, win, wzg, wgate, bgate)


def _memkv_kernel(m_ref, g_ref, w_ref, k_ref, v_ref, k4_ref, v4_ref):
    mn = _rms(m_ref[...], g_ref[...]).astype(BF16)
    kv = _dot(mn, w_ref[...])
    k_ref[...] = kv[:, :D_MODEL]
    v_ref[...] = kv[:, D_MODEL:]
    for hd in range(CROSS_HEADS):
        k4_ref[:, hd, :] = kv[:, hd * CROSS_DH:(hd + 1) * CROSS_DH]
        v4_ref[:, hd, :] = kv[:, D_MODEL + hd * CROSS_DH:D_MODEL + (hd + 1) * CROSS_DH]


def _memkv_call(mem2d, g, w, tm):
    n = mem2d.shape[0]
    row = pl.BlockSpec((tm, D_MODEL), lambda i: (i, 0))
    row4 = pl.BlockSpec((tm, CROSS_HEADS, CROSS_DH), lambda i: (i, 0, 0))
    return pl.pallas_call(
        _memkv_kernel,
        grid=(n // tm,),
        in_specs=[row, _const_spec(g.shape), _const_spec(w.shape)],
        out_specs=[row, row, row4, row4],
        out_shape=[jax.ShapeDtypeStruct((n, D_MODEL), F32)] * 2
        + [jax.ShapeDtypeStruct((n, CROSS_HEADS, CROSS_DH), F32)] * 2,
        compiler_params=_cparams(1),
        name="memory_kv",
    )(mem2d, g, w)


def _mid_kernel(h_ref, og_ref, os_ref, wo_ref, gc_ref, wq_ref, h2_ref, q_ref):
    h2 = (h_ref[...] + _dot(og_ref[...].astype(BF16), wo_ref[0:GLA_WIDTH, :])
          + _dot(os_ref[...].astype(BF16), wo_ref[GLA_WIDTH:, :]))
    h2_ref[...] = h2
    hn = _rms(h2, gc_ref[...]).astype(BF16)
    q_ref[...] = _dot(hn, wq_ref[...])


def _mid_call(h, og, os_, wo, gc, wq, tm):
    n = h.shape[0]
    row = lambda w: pl.BlockSpec((tm, w), lambda i: (i, 0))
    return pl.pallas_call(
        _mid_kernel,
        grid=(n // tm,),
        in_specs=[row(D_MODEL), row(512), row(512), _const_spec(wo.shape), _const_spec(gc.shape),
                  _const_spec(wq.shape)],
        out_specs=[row(D_MODEL), row(D_MODEL)],
        out_shape=[jax.ShapeDtypeStruct((n, D_MODEL), F32)] * 2,
        compiler_params=_cparams(1),
        name="mix_out_cross_q",
    )(h, og, os_, wo, gc, wq)


def _post_kernel(h2_ref, o_ref, wco_ref, g2_ref, wgu_ref, wdn_ref, gf_ref, y_ref):
    h3 = h2_ref[...] + _dot(o_ref[...].astype(BF16), wco_ref[...])
    hn = _rms(h3, g2_ref[...]).astype(BF16)
    h4 = h3 + 0.5 * _swiglu(hn, wgu_ref, wdn_ref)
    y_ref[...] = _rms(h4, gf_ref[...])


def _post_call(h2, o, wco, g2, wgu, wdn, gf, tm):
    n = h2.shape[0]
    row = pl.BlockSpec((tm, D_MODEL), lambda i: (i, 0))
    return pl.pallas_call(
        _post_kernel,
        grid=(n // tm,),
        in_specs=[row, row, _const_spec(wco.shape), _const_spec(g2.shape), _const_spec(wgu.shape),
                  _const_spec(wdn.shape), _const_spec(gf.shape)],
        out_specs=row,
        out_shape=jax.ShapeDtypeStruct((n, D_MODEL), F32),
        compiler_params=_cparams(1),
        name="post_ffn_final",
    )(h2, o, wco, g2, wgu, wdn, gf)


def _tail_kernel(h_ref, og_ref, os_ref, k_ref, v_ref, wo_ref, gc_ref, wq_ref, wco_ref, g2_ref, wgu_ref,
                 wdn_ref, gf_ref, y_ref):
    h2 = (h_ref[0] + _dot(og_ref[0].astype(BF16), wo_ref[0:GLA_WIDTH, :])
          + _dot(os_ref[0].astype(BF16), wo_ref[GLA_WIDTH:, :]))
    q = _dot(_rms(h2, gc_ref[...]).astype(BF16), wq_ref[...]).astype(BF16)
    heads = [slice(hd * CROSS_DH, (hd + 1) * CROSS_DH) for hd in range(CROSS_HEADS)]
    scores = [_dot_nt(q[:, sl], k_ref[0, :, sl].astype(BF16)) * (CROSS_DH ** -0.5) for sl in heads]
    probs = [_softmax(s).astype(BF16) for s in scores]
    outs = [_dot(p, v_ref[0, :, sl].astype(BF16)).astype(BF16) for p, sl in zip(probs, heads)]
    h3 = h2 + _dot(jnp.concatenate(outs, axis=-1), wco_ref[...])
    hn = _rms(h3, g2_ref[...]).astype(BF16)
    h4 = h3 + 0.5 * _swiglu(hn, wgu_ref, wdn_ref)
    y_ref[0] = _rms(h4, gf_ref[...])


def _tail_call(h3d, og3, os3, k3, v3, w, tm):
    s, t, _ = h3d.shape
    row = lambda wd: pl.BlockSpec((1, tm, wd), lambda i, j: (i, j, 0))
    kvspec = pl.BlockSpec((1, MEM_LEN, D_MODEL), lambda i, j: (i, 0, 0))
    consts = [w['w_out'], w['g_cross'], w['w_cq'], w['w_co'], w['g_ffn2'], w['w_ffn2_gu'], w['w_ffn2_down'],
              w['g_final']]
    return pl.pallas_call(
        _tail_kernel,
        grid=(s, t // tm),
        in_specs=[row(D_MODEL), row(GLA_WIDTH), row(S5_WIDTH), kvspec, kvspec] + [_const_spec(a.shape) for a in consts],
        out_specs=row(D_MODEL),
        out_shape=jax.ShapeDtypeStruct((s, t, D_MODEL), F32),
        compiler_params=_cparams(2),
        name="tail_attn_ffn",
    )(h3d, og3, os3, k3, v3, *consts)


def _softmax(s):
    e = jnp.exp(s - jnp.max(s, axis=-1, keepdims=True))
    return e / jnp.sum(e, axis=-1, keepdims=True)


def _load_mem(ref, j):
    return jnp.concatenate([ref[j, :, hd, :] for hd in range(CROSS_HEADS)], axis=-1).astype(BF16)


def _attn_packed_kernel(q_ref, k_ref, v_ref, o_ref, *, n_seq):
    t = q_ref.shape[1]
    shape = (CROSS_HEADS * t, D_MODEL)
    q_mask = (_iota2(shape, 0) // t) == (_iota2(shape, 1) // CROSS_DH)
    o_lane_head = _iota2((t, D_MODEL), 1) // CROSS_DH
    scores = []
    for j in range(n_seq):
        q_stack = jnp.where(q_mask, jnp.concatenate([q_ref[j]] * CROSS_HEADS, axis=0), 0.0).astype(BF16)
        scores.append(_dot_nt(q_stack, _load_mem(k_ref, j)) * (CROSS_DH ** -0.5))
    probs = [_softmax(s).astype(BF16) for s in scores]
    for j in range(n_seq):
        full = _dot(probs[j], _load_mem(v_ref, j))
        o = jnp.zeros((t, D_MODEL), F32)
        for hd in range(CROSS_HEADS):
            o = jnp.where(o_lane_head == hd, full[hd * t:(hd + 1) * t, :], o)
        o_ref[j] = o.astype(o_ref.dtype)


def _attn_call(q3, k3, v3, n_seq):
    s, t, _ = q3.shape
    assert t * CROSS_HEADS <= MXU_DIM and s % n_seq == 0
    qspec = pl.BlockSpec((n_seq, t, D_MODEL), lambda i: (i, 0, 0))
    kvspec = pl.BlockSpec((n_seq, MEM_LEN, CROSS_HEADS, CROSS_DH), lambda i: (i, 0, 0, 0))
    return pl.pallas_call(
        functools.partial(_attn_packed_kernel, n_seq=n_seq),
        grid=(s // n_seq,),
        in_specs=[qspec, kvspec, kvspec],
        out_specs=qspec,
        out_shape=jax.ShapeDtypeStruct(q3.shape, F32),
        compiler_params=_cparams(1),
        name="cross_attention",
    )(q3, k3, v3)


def _split3(x):
    hi = x.astype(BF16)
    r1 = x - hi.astype(F32)
    mid = r1.astype(BF16)
    lo = (r1 - mid.astype(F32)).astype(BF16)
    return hi, mid, lo


def _dot01(m01, x):
    hi, mid, lo = _split3(x)
    return _dot(m01, hi) + _dot(m01, mid) + _dot(m01, lo)


def _iota2(shape, dim):
    return lax.broadcasted_iota(jnp.int32, shape, dim)


def _gla_stages(qk_ref, v_ref, la_ref, r_ref, o_ref, g_ref, tri_ref, lvl_ref, khm_ref, vhm_ref, shm_ref,
                states, put_states, *, seq_len, n_chunks):
    c_rows = GLA_CHUNK_ROWS
    n_sub = c_rows // seq_len
    log2 = lambda n: int(math.log2(n))
    tri = tri_ref[...]
    lvl = lvl_ref[...]
    rowq = _iota2((c_rows, GLA_QK_WIDTH), 0)
    u_head = [(_iota2((GLA_DV, GLA_QK_WIDTH), 1) >> log2(GLA_DK)) == hd for hd in range(GLA_HEADS)]

    def head_blocks(x, mask_ref):
        return jnp.concatenate([x.astype(BF16)] * GLA_HEADS, axis=0) * mask_ref[...]

    g_out = g_ref[...]
    chunks = range(n_chunks)
    rows_of = lambda c: slice(c * c_rows, (c + 1) * c_rows)
    sub_of = lambda j: slice(j * seq_len, (j + 1) * seq_len)
    q = [qk_ref[rows_of(c), 0:GLA_QK_WIDTH] * (GLA_DK ** -0.5) for c in chunks]
    k = [qk_ref[rows_of(c), GLA_QK_WIDTH:2 * GLA_QK_WIDTH] for c in chunks]
    v = [v_ref[rows_of(c), :] for c in chunks]
    b = [_dot01(tri, la_ref[rows_of(c), :]) for c in chunks]
    yield

    a_mat = [jnp.where(lvl == 0, _dot_nt(q[c].astype(BF16), head_blocks(k[c], khm_ref)), 0.0) for c in chunks]
    last = list(b)
    m = 1
    while m < seq_len:
        yield
        even = ((rowq >> log2(m)) & 1) == 0
        for c in chunks:
            prev_last = pltpu.roll(last[c], m, axis=0)
            e = jnp.exp(jnp.minimum(jnp.where(even, last[c] - b[c], b[c] - prev_last), 0.0))
            a_lvl = _dot_nt((q[c] * e).astype(BF16), head_blocks(k[c] * e, khm_ref))
            a_mat[c] = jnp.where(lvl == log2(m) + 1, a_lvl, a_mat[c])
            last[c] = jnp.where(even, pltpu.roll(last[c], c_rows - m, axis=0), last[c])
        m *= 2
    yield
    o_intra = [_dot(a_mat[c].astype(BF16), head_blocks(v[c], vhm_ref)) for c in chunks]
    yield

    narrow = (lambda a: a.astype(BF16)) if seq_len % 16 == 0 else (lambda a: a)
    q_dec = [narrow(q[c] * jnp.exp(b[c])) for c in chunks]
    k_dec = [narrow(k[c] * jnp.exp(jnp.minimum(last[c] - b[c], 0.0))) for c in chunks]
    u_t = {}
    for c in chunks:
        vc = narrow(v[c])
        for j in range(n_sub):
            u_full = _dot_tn(vc[sub_of(j)], k_dec[c][sub_of(j)])
            acc = jnp.zeros((GLA_DV, GLA_QK_WIDTH), F32)
            for hd in range(GLA_HEADS):
                acc = jnp.where(u_head[hd], u_full[hd * GLA_DV:(hd + 1) * GLA_DV, :], acc)
            u_t[c, j] = acc

    states = list(states)
    state_of = (lambda c, j: c * n_sub + j) if n_sub > 1 else (lambda c, j: 0)
    for c in chunks:
        if c % 2 == 0:
            yield
        o_parts = []
        for j in range(n_sub):
            sj = state_of(c, j)
            o_parts.append(_dot_nt(q_dec[c][sub_of(j)], head_blocks(states[sj], shm_ref)))
            decay = jnp.exp(last[c][j * seq_len:j * seq_len + 1, :])
            states[sj] = states[sj] * decay + u_t[c, j]
        o_inter = o_parts[0] if n_sub == 1 else jnp.concatenate(o_parts, axis=0)
        o = o_intra[c] + o_inter
        r = r_ref[rows_of(c), :]
        outs = []
        for hd in range(GLA_HEADS):
            sl = slice(hd * GLA_DV, (hd + 1) * GLA_DV)
            outs.append(_rms(o[:, sl], g_out) * jax.nn.silu(r[:, sl]))
        o_ref[rows_of(c), :] = jnp.concatenate(outs, axis=-1).astype(o_ref.dtype)
    put_states(states)


def _gla_kernel(*refs, seq_len, n_chunks, n_state, has_s0):
    ins, rest = refs[:10], refs[10:]
    qk_ref, v_ref, la_ref, r_ref, g_ref = ins[:5]
    if has_s0:
        s0_ref, o_ref, st_ref, s_scr = rest
    else:
        o_ref, st_ref, s_scr = rest
        s0_ref = None
    t_idx = pl.program_id(1)

    @pl.when(t_idx == 0)
    def _():
        if has_s0:
            for j in range(n_state):
                s_scr[j] = s0_ref[j].reshape(GLA_QK_WIDTH, GLA_DV).T
        else:
            s_scr[...] = jnp.zeros_like(s_scr)

    def put_states(states):
        for j, s_t in enumerate(states):
            s_scr[j] = s_t

    for _ in _gla_stages(qk_ref.at[0], v_ref.at[0], la_ref.at[0], r_ref.at[0], o_ref.at[0], g_ref, *ins[5:],
                         [s_scr[j] for j in range(n_state)], put_states, seq_len=seq_len, n_chunks=n_chunks):
        pass

    @pl.when(t_idx == pl.num_programs(1) - 1)
    def _():
        for j in range(n_state):
            st_ref[j] = s_scr[j].T.reshape(GLA_HEADS, GLA_DK, GLA_DV)


def _gla_index_constants(seq_len):
    c = GLA_CHUNK_ROWS
    i = np.arange(c)[:, None]
    s = np.arange(c)[None, :]
    same_seq = (i // seq_len) == (s // seq_len)
    tri = (same_seq & (s <= i)).astype(np.float32)
    top_bit = np.floor(np.log2(np.maximum(i ^ s, 1))).astype(np.int32)
    lvl = np.where(same_seq & (s < i), top_bit + 1, -1)
    lvl = np.where(i == s, 0, lvl).astype(np.int32)
    head_mask = lambda rows_per_head, lanes_per_head: (
        (np.arange(GLA_HEADS * rows_per_head)[:, None] // rows_per_head)
        == (np.arange(GLA_HEADS * lanes_per_head)[None, :] // lanes_per_head)).astype(np.float32)
    return [jnp.asarray(tri, BF16), jnp.asarray(np.tile(lvl, (1, GLA_HEADS))),
            jnp.asarray(head_mask(c, GLA_DK), BF16), jnp.asarray(head_mask(c, GLA_DV), BF16),
            jnp.asarray(head_mask(GLA_DV, GLA_DK), BF16)]


def _gla_call(qk3, v3, la3, r3, g_out, s0, seq_len, tile_rows, n_state):
    n_outer, rows, _ = qk3.shape
    n_chunks = tile_rows // GLA_CHUNK_ROWS
    spec = lambda w: pl.BlockSpec((1, tile_rows, w), lambda i, t: (i, t, 0))
    st_spec = pl.BlockSpec((n_state, GLA_HEADS, GLA_DK, GLA_DV), lambda i, t: (i, 0, 0, 0))
    has_s0 = s0 is not None
    consts = [g_out] + _gla_index_constants(seq_len)
    in_specs = [spec(512), spec(512), spec(256), spec(512)] + [_const_spec(a.shape) for a in consts]
    args = [qk3, v3, la3, r3] + consts
    if has_s0:
        in_specs.append(st_spec)
        args.append(s0)
    return pl.pallas_call(
        functools.partial(_gla_kernel, seq_len=seq_len, n_chunks=n_chunks, n_state=n_state, has_s0=has_s0),
        grid=(n_outer, rows // tile_rows),
        in_specs=in_specs,
        out_specs=[spec(512), st_spec],
        out_shape=[jax.ShapeDtypeStruct((n_outer, rows, GLA_WIDTH), BF16),
                   jax.ShapeDtypeStruct((n_outer * n_state, GLA_HEADS, GLA_DK, GLA_DV), F32)],
        scratch_shapes=[pltpu.VMEM((n_state, GLA_DV, GLA_QK_WIDTH), F32)],
        compiler_params=_cparams(2),
        name="gla_mixer",
    )(*args)


def _pre_gla_kernel(x_ref, g1_ref, wgu_ref, wdn_ref, gm_ref, win_ref, wzg_ref, wgate_ref, bgate_ref, gout_ref,
                    tri_ref, lvl_ref, khm_ref, vhm_ref, shm_ref,
                    h_ref, us5_ref, og_ref, st_ref,
                    qk_scr, v_scr, la_scr, r_scr, s_scr, *, tiles_per_seq):
    i = pl.program_id(0)
    last_step = pl.num_programs(0) - 1
    n_chunks = x_ref.shape[0] // GLA_CHUNK_ROWS

    @pl.when(i == 0)
    def _():
        for scr in (qk_scr, v_scr, la_scr, r_scr, s_scr):
            scr[...] = jnp.zeros_like(scr)

    def gla_stages():
        first_of_seq = lax.rem(i + (tiles_per_seq - 1), tiles_per_seq) == 0
        state = jnp.where(first_of_seq, 0.0, s_scr[0])

        def put_states(states):
            s_scr[0] = states[0]
            st_ref[0] = states[0].T.reshape(GLA_HEADS, GLA_DK, GLA_DV)

        return _gla_stages(qk_scr, v_scr, la_scr, r_scr, og_ref, gout_ref, tri_ref, lvl_ref, khm_ref, vhm_ref,
                           shm_ref, [state], put_states, seq_len=GLA_CHUNK_ROWS, n_chunks=n_chunks)

    @pl.when(i < last_step)
    def _():
        stages = gla_stages()
        next(stages)
        x = x_ref[...]
        xn = _rms(x, g1_ref[...]).astype(BF16)
        h = x + 0.5 * _swiglu(xn, wgu_ref, wdn_ref, between=lambda: next(stages, None))
        h_ref[...] = h
        un = _rms(h, gm_ref[...]).astype(BF16)
        p = []
        for grp in range(4):
            p.append(_dot(un, win_ref[:, grp * 512:(grp + 1) * 512]))
            next(stages, None)
        zg = _dot(un, wzg_ref[...])
        z = _dot(zg.astype(BF16), wgate_ref[...]) + bgate_ref[...]
        for _ in stages:
            pass
        qk_scr[...] = p[0]
        v_scr[...] = p[1]
        r_scr[...] = p[2]
        us5_ref[...] = p[3]
        la_scr[...] = _log_sigmoid(z) * (1.0 / GLA_TAU)

    @pl.when(i == last_step)
    def _():
        for _ in gla_stages():
            pass


def _pre_gla_call(x2d, w, tm, seq_rows):
    n = x2d.shape[0]
    n_tiles = n // tm
    tiles_per_seq = seq_rows // tm
    cur = lambda wd: pl.BlockSpec((tm, wd), lambda i: (jnp.minimum(i, n_tiles - 1), 0))
    prev = lambda wd: pl.BlockSpec((tm, wd), lambda i: (jnp.maximum(i - 1, 0), 0))
    st_spec = pl.BlockSpec((1, GLA_HEADS, GLA_DK, GLA_DV),
                           lambda i: (jnp.maximum(i - 1, 0) // tiles_per_seq, 0, 0, 0))
    consts = [w['g_ffn1'], w['w_ffn1_gu'], w['w_ffn1_down'], w['g_mix'], w['w_in_main'], w['w_zg'], w['w_gate'],
              w['b_gate'], w['g_gla_out']] + _gla_index_constants(GLA_CHUNK_ROWS)
    return pl.pallas_call(
        functools.partial(_pre_gla_kernel, tiles_per_seq=tiles_per_seq),
        grid=(n_tiles + 1,),
        in_specs=[cur(D_MODEL)] + [_const_spec(a.shape) for a in consts],
        out_specs=[cur(D_MODEL), cur(S5_WIDTH), prev(GLA_WIDTH), st_spec],
        out_shape=[jax.ShapeDtypeStruct((n, D_MODEL), F32), jax.ShapeDtypeStruct((n, S5_WIDTH), F32),
                   jax.ShapeDtypeStruct((n, GLA_WIDTH), BF16),
                   jax.ShapeDtypeStruct((n // seq_rows, GLA_HEADS, GLA_DK, GLA_DV), F32)],
        scratch_shapes=[pltpu.VMEM((tm, 2 * GLA_QK_WIDTH), F32), pltpu.VMEM((tm, GLA_WIDTH), F32),
                        pltpu.VMEM((tm, GLA_QK_WIDTH), F32), pltpu.VMEM((tm, GLA_WIDTH), F32),
                        pltpu.VMEM((1, GLA_DV, GLA_QK_WIDTH), F32)],
        compiler_params=_cparams(1),
        name="pre_ffn_gla",
    )(x2d, *consts)


def _s5_disc_kernel(lre_ref, lim_ref, ldt_ref, bre_ref, bim_ref, are_ref, aim_ref, bbre_ref, bbim_ref):
    lr = lre_ref[...]
    li = lim_ref[...]
    dt = jnp.exp(ldt_ref[...])
    mag = jnp.exp(dt * lr)
    ab_re = mag * jnp.cos(dt * li)
    ab_im = mag * jnp.sin(dt * li)
    den = lr * lr + li * li
    coef_re = ((ab_re - 1.0) * lr + ab_im * li) / den
    coef_im = (ab_im * lr - (ab_re - 1.0) * li) / den
    are_ref[...] = ab_re
    aim_ref[...] = ab_im
    br = bre_ref[...]
    bi = bim_ref[...]
    cr = coef_re[:, None, :]
    cim = coef_im[:, None, :]
    bbre_ref[...] = cr * br - cim * bi
    bbim_ref[...] = cr * bi + cim * br


def _s5_disc_call(lre, lim, ldt, bre_t, bim_t):
    gn = jax.ShapeDtypeStruct((S5_GROUPS, S5_N), F32)
    gpn = jax.ShapeDtypeStruct((S5_GROUPS, S5_P, S5_N), F32)
    return pl.pallas_call(
        _s5_disc_kernel,
        out_shape=[gn, gn, gpn, gpn],
        name="s5_discretise",
    )(lre, lim, ldt, bre_t, bim_t)


def _to_time_major(u_ref, perm_scr, n_seq, t_tile):
    if n_seq == SUBLANES:
        n_slab = S5_WIDTH // LANES
        for j in range(n_seq):
            for sb in range(n_slab):
                perm_scr[sb, pl.ds(j, t_tile, stride=n_seq), :] = u_ref[j, :, sb * LANES:(sb + 1) * LANES]
        return jnp.concatenate([perm_scr[sb] for sb in range(n_slab)], axis=-1)
    return jnp.concatenate([u_ref[:, t, :] for t in range(t_tile)], axis=0)


def _from_time_major(y, o_ref, perm_scr, n_seq, t_tile):
    if n_seq == SUBLANES:
        n_slab = S5_WIDTH // LANES
        for sb in range(n_slab):
            perm_scr[sb] = y[:, sb * LANES:(sb + 1) * LANES]
        for j in range(n_seq):
            for sb in range(n_slab):
                o_ref[j, :, sb * LANES:(sb + 1) * LANES] = perm_scr[sb, pl.ds(j, t_tile, stride=n_seq), :]
    else:
        for t in range(t_tile):
            o_ref[:, t, :] = y[t * n_seq:(t + 1) * n_seq, :]


def _s5_kernel(u_ref, x0re_ref, x0im_ref, are_ref, aim_ref, bre_ref, bim_ref, c_ref, d_ref, wglu_ref,
               bglu_ref, o_ref, xre_ref, xim_ref, sre_scr, sim_scr, xre_scr, xim_scr, perm_scr, *, n_seq, t_tile):
    half_w = S5_WIDTH // 2
    half_s = S5_STATE // 2

    @pl.when(pl.program_id(0) == 0)
    def _():
        xre_scr[...] = x0re_ref[...]
        xim_scr[...] = x0im_ref[...]

    rows = t_tile * n_seq
    u = _to_time_major(u_ref, perm_scr, n_seq, t_tile)
    ub = u.astype(BF16)
    for hf in range(2):
        uh = ub[:, hf * half_w:(hf + 1) * half_w]
        st = slice(hf * half_s, (hf + 1) * half_s)
        sre_scr[:, :, st] = _dot(uh, bre_ref[hf]).reshape(t_tile, n_seq, half_s)
        sim_scr[:, :, st] = _dot(uh, bim_ref[hf]).reshape(t_tile, n_seq, half_s)

    for lc in range(S5_STATE // S5_LANE_CHUNK):
        lanes = slice(lc * S5_LANE_CHUNK, (lc + 1) * S5_LANE_CHUNK)
        ar = jnp.broadcast_to(are_ref[:, lanes], (SUBLANES, S5_LANE_CHUNK))
        ai = jnp.broadcast_to(aim_ref[:, lanes], (SUBLANES, S5_LANE_CHUNK))
        for sc in range(n_seq // SUBLANES):
            srows = slice(sc * SUBLANES, (sc + 1) * SUBLANES)
            xr = xre_scr[srows, lanes]
            xi = xim_scr[srows, lanes]
            for t in range(t_tile):
                nxr = ar * xr - ai * xi + sre_scr[t, srows, lanes]
                nxi = ar * xi + ai * xr + sim_scr[t, srows, lanes]
                sre_scr[t, srows, lanes] = nxr
                sim_scr[t, srows, lanes] = nxi
                xr, xi = nxr, nxi
            xre_scr[srows, lanes] = xr
            xim_scr[srows, lanes] = xi

    xr_all = sre_scr[...].reshape(rows, S5_STATE).astype(BF16)
    xi_all = sim_scr[...].reshape(rows, S5_STATE).astype(BF16)
    ys = []
    for hf in range(2):
        st = slice(hf * half_s, (hf + 1) * half_s)
        ys.append(_dot(xr_all[:, st], c_ref[hf, 0:half_s, :]) + _dot(xi_all[:, st], c_ref[hf, half_s:, :]))
    y = jnp.concatenate(ys, axis=-1) + d_ref[...] * u
    z = jax.nn.gelu(y)
    gate = jax.nn.sigmoid(_dot(z.astype(BF16), wglu_ref[...]) + bglu_ref[...])
    _from_time_major(z * gate, o_ref, perm_scr, n_seq, t_tile)

    @pl.when(pl.program_id(0) == pl.num_programs(0) - 1)
    def _():
        xre_ref[...] = xre_scr[...]
        xim_ref[...] = xim_scr[...]


def _s5_call(u3, x0re, x0im, are, aim, bre, bim, cmat, dskip, wglu, bglu, t_tile):
    n_seq, t, _ = u3.shape
    uspec = pl.BlockSpec((n_seq, t_tile, S5_WIDTH), lambda i: (0, i, 0))
    perm_rows = t_tile * n_seq if n_seq == SUBLANES else SUBLANES
    xspec = pl.BlockSpec((n_seq, S5_STATE), lambda i: (0, 0))
    consts = [are, aim, bre, bim, cmat, dskip, wglu, bglu]
    return pl.pallas_call(
        functools.partial(_s5_kernel, n_seq=n_seq, t_tile=t_tile),
        grid=(t // t_tile,),
        in_specs=[uspec, xspec, xspec] + [_const_spec(a.shape) for a in consts],
        out_specs=[uspec, xspec, xspec],
        out_shape=[jax.ShapeDtypeStruct((n_seq, t, S5_WIDTH), F32),
                   jax.ShapeDtypeStruct((n_seq, S5_STATE), F32),
                   jax.ShapeDtypeStruct((n_seq, S5_STATE), F32)],
        scratch_shapes=[pltpu.VMEM((t_tile, n_seq, S5_STATE), F32), pltpu.VMEM((t_tile, n_seq, S5_STATE), F32),
                        pltpu.VMEM((n_seq, S5_STATE), F32), pltpu.VMEM((n_seq, S5_STATE), F32),
                        pltpu.VMEM((S5_WIDTH // LANES, perm_rows, LANES), F32)],
        compiler_params=_cparams(1),
        name="s5_mixer",
    )(u3, x0re, x0im, *consts)


def _block_diag(blocks):
    g, a, b = blocks.shape
    eye = jnp.eye(g, dtype=blocks.dtype)
    return (eye[:, None, :, None] * blocks[:, :, None, :]).reshape(g * a, g * b)


def _halves_block_diag(blocks):
    g = blocks.shape[0]
    return jnp.stack([_block_diag(blocks[:g // 2]), _block_diag(blocks[g // 2:])])


def _trunk_group(x3, s_gla0, x0re, x0im, mem_k3, mem_v3, w):
    s, t, _ = x3.shape
    n = s * t
    tm = ROW_TILE
    long_seq = t >= tm
    if long_seq:
        assert s_gla0 is None and t % tm == 0
        h, us5, o_gla, st_new = _pre_gla_call(x3.reshape(n, D_MODEL), w, tm, t)
    else:
        assert GLA_CHUNK_ROWS % t == 0 and n % (SHORT_GLA_CHUNKS * GLA_CHUNK_ROWS) == 0
        h, qk, v, la, r, us5 = _pre_call(x3.reshape(n, D_MODEL), w['g_ffn1'], w['w_ffn1_gu'], w['w_ffn1_down'],
                                         w['g_mix'], w['w_in_main'], w['w_zg'], w['w_gate'], w['b_gate'], tm)
        rows = SHORT_GLA_CHUNKS * GLA_CHUNK_ROWS
        shp = lambda a: a.reshape(n // rows, rows, a.shape[-1])
        o_gla, st_new = _gla_call(shp(qk), shp(v), shp(la), shp(r), w['g_gla_out'], s_gla0, t, rows, rows // t)
    o_s5, xre, xim = _s5_call(us5.reshape(s, t, S5_WIDTH), x0re, x0im, w['s5_are'], w['s5_aim'], w['s5_bre'],
                              w['s5_bim'], w['s5_c'], w['s5_d'], w['s5_wglu'], w['s5_bglu'], min(t, S5_TILE_STEPS))
    if long_seq:
        y = _tail_call(h.reshape(s, t, D_MODEL), o_gla.reshape(s, t, GLA_WIDTH), o_s5, mem_k3, mem_v3, w, tm)
        return y, st_new, xre, xim
    o_s5 = o_s5.reshape(n, S5_WIDTH)
    h2, q = _mid_call(h, o_gla.reshape(n, GLA_WIDTH), o_s5, w['w_out'], w['g_cross'], w['w_cq'], tm)
    o_att = _attn_call(q.reshape(s, t, D_MODEL), mem_k3, mem_v3, SHORT_ATTN_SEQS)
    y = _post_call(h2, o_att.reshape(n, D_MODEL), w['w_co'], w['g_ffn2'], w['w_ffn2_gu'], w['w_ffn2_down'],
                   w['g_final'], tm)
    return y.reshape(s, t, D_MODEL), st_new, xre, xim


def kernel(x_prompt, x_sample, state_gla, state_s5_re, state_s5_im, cache_mem_k, cache_mem_v, mem_prompt, g_ffn1, w_ffn1_gu, w_ffn1_down, g_mix, w_in, w_gla_gate, b_gla_gate, g_gla_out, s5_lambda_re, s5_lambda_im, s5_log_dt, s5_b_re, s5_b_im, s5_c_re, s5_c_im, s5_d, s5_w_glu, s5_b_glu, w_out, g_cross, g_mem, w_cq, w_ckv, w_co, g_ffn2, w_ffn2_gu, w_ffn2_down, g_final):
    bp, tp, _ = x_prompt.shape
    bs, ts, _ = x_sample.shape
    l = 0
    row = lambda a: a.reshape(1, -1).astype(F32)
    bf = lambda a: a.astype(BF16)

    are, aim, bbre, bbim = _s5_disc_call(
        s5_lambda_re[l], s5_lambda_im[l], s5_log_dt[l].reshape(S5_GROUPS, 1),
        jnp.transpose(s5_b_re[l], (0, 2, 1)), jnp.transpose(s5_b_im[l], (0, 2, 1)))
    c_re_t = jnp.transpose(s5_c_re[l], (0, 2, 1))
    c_im_t = jnp.transpose(s5_c_im[l], (0, 2, 1))
    cre_h = _halves_block_diag(c_re_t)
    cim_h = _halves_block_diag(-c_im_t)
    w_in_l = w_in[l]
    zg_lo = 2 * GLA_QK_WIDTH + GLA_WIDTH
    zg_hi = zg_lo + GLA_GATE_RANK
    w = {
        'g_ffn1': row(g_ffn1[l]), 'w_ffn1_gu': bf(w_ffn1_gu[l]), 'w_ffn1_down': bf(w_ffn1_down[l]),
        'g_mix': row(g_mix[l]),
        'w_in_main': bf(jnp.concatenate([w_in_l[:, :zg_lo], w_in_l[:, zg_hi:]], axis=1)),
        'w_zg': bf(jnp.pad(w_in_l[:, zg_lo:zg_hi], ((0, 0), (0, LANES - GLA_GATE_RANK)))),
        'w_gate': bf(jnp.pad(w_gla_gate[l], ((0, LANES - GLA_GATE_RANK), (0, 0)))),
        'b_gate': row(b_gla_gate[l]),
        'g_gla_out': row(g_gla_out[l]),
        's5_are': are.reshape(1, S5_STATE), 's5_aim': aim.reshape(1, S5_STATE),
        's5_bre': bf(_halves_block_diag(bbre)), 's5_bim': bf(_halves_block_diag(bbim)),
        's5_c': bf(jnp.concatenate([cre_h, cim_h], axis=1)),
        's5_d': row(s5_d[l]), 's5_wglu': bf(_block_diag(s5_w_glu[l])), 's5_bglu': row(s5_b_glu[l]),
        'w_out': bf(w_out[l]), 'g_cross': row(g_cross[l]), 'w_cq': bf(w_cq[l]), 'w_co': bf(w_co[l]),
        'g_ffn2': row(g_ffn2[l]), 'w_ffn2_gu': bf(w_ffn2_gu[l]), 'w_ffn2_down': bf(w_ffn2_down[l]),
        'g_final': row(g_final),
    }

    mk, mv, mk4, mv4 = _memkv_call(mem_prompt.reshape(bp * MEM_LEN, D_MODEL), row(g_mem[l]), bf(w_ckv[l]), ROW_TILE)
    zeros_state = jnp.zeros((bp, S5_STATE), F32)
    y_p, st_p, re_p, im_p = _trunk_group(
        x_prompt, None, zeros_state, zeros_state, mk.reshape(bp, MEM_LEN, D_MODEL), mv.reshape(bp, MEM_LEN, D_MODEL),
        w)

    s0 = state_gla.reshape(DEPTH * bs, GLA_HEADS, GLA_DK, GLA_DV)[l * bs:(l + 1) * bs]
    y_s, st_s, re_s, im_s = _trunk_group(
        x_sample, s0, state_s5_re[l].reshape(bs, S5_STATE), state_s5_im[l].reshape(bs, S5_STATE),
        cache_mem_k.reshape(DEPTH * bs, MEM_LEN, CROSS_HEADS, CROSS_DH)[l * bs:(l + 1) * bs],
        cache_mem_v.reshape(DEPTH * bs, MEM_LEN, CROSS_HEADS, CROSS_DH)[l * bs:(l + 1) * bs],
        w)

    def gla_state_out(st, nb):
        return st[None]

    s5_out = lambda a, nb: a.reshape(1, nb, S5_GROUPS, S5_N)
    kv_out = lambda a: a.reshape(1, bp, MEM_LEN, CROSS_HEADS, CROSS_DH)
    return (y_p, y_s,
            gla_state_out(st_p, bp), s5_out(re_p, bp), s5_out(im_p, bp), kv_out(mk4), kv_out(mv4),
            gla_state_out(st_s, bs), s5_out(re_s, bs), s5_out(im_s, bs))
```

```python
import functools
import math

import numpy as np
import jax
import jax.numpy as jnp
from jax import lax
from jax.experimental import pallas as pl
from jax.experimental.pallas import tpu as pltpu

F32 = jnp.float32
BF16 = jnp.bfloat16

D_MODEL = 1024
DEPTH = 1
EPS = 1e-6
D_FF = 2816
GLA_WIDTH = 512
GLA_HEADS = 4
GLA_DV = 128
GLA_DK = 64
GLA_QK_WIDTH = 256
GLA_GATE_RANK = 16
GLA_TAU = 16.0
S5_WIDTH = 512
S5_P = 16
S5_GROUPS = 32
S5_N = 64
S5_STATE = S5_GROUPS * S5_N
CROSS_HEADS = 4
CROSS_DH = 256
MEM_LEN = 256

LANES = 128
SUBLANES = 8
MXU_DIM = 256
VMEM_LIMIT_BYTES = 56 * 1024 * 1024

ROW_TILE = 512
S5_TILE_STEPS = 128
SHORT_GLA_CHUNKS = 4
SHORT_ATTN_SEQS = 4
FF_CHUNK = MXU_DIM
GLA_CHUNK_ROWS = 64
S5_LANE_CHUNK = 512


def _cparams(n_grid_axes):
    return pltpu.CompilerParams(
        dimension_semantics=("arbitrary",) * n_grid_axes,
        vmem_limit_bytes=VMEM_LIMIT_BYTES,
    )


def _const_spec(shape):
    zeros = (0,) * len(shape)
    return pl.BlockSpec(shape, lambda *_: zeros, pipeline_mode=pl.Buffered(1))


def _dot(a, b):
    return jnp.dot(a, b, preferred_element_type=F32)


def _dot_nt(a, b):
    return lax.dot_general(a, b, (((1,), (1,)), ((), ())), preferred_element_type=F32)


def _dot_tn(a, b):
    return lax.dot_general(a, b, (((0,), (0,)), ((), ())), preferred_element_type=F32)


def _rms(x, g):
    return x * lax.rsqrt(jnp.mean(x * x, axis=-1, keepdims=True) + EPS) * g


def _swiglu(xn_bf, wgu_ref, wdn_ref, between=None):
    n_chunks = D_FF // FF_CHUNK

    def gate_up(c):
        lo, hi = c * FF_CHUNK, (c + 1) * FF_CHUNK
        return _dot(xn_bf, wgu_ref[:, lo:hi]), _dot(xn_bf, wgu_ref[:, D_FF + lo:D_FF + hi])

    acc = None
    nxt = gate_up(0)
    for c in range(n_chunks):
        g, u = nxt
        if c + 1 < n_chunks:
            nxt = gate_up(c + 1)
        a = (jax.nn.silu(g) * u).astype(BF16)
        d = _dot(a, wdn_ref[c * FF_CHUNK:(c + 1) * FF_CHUNK, :])
        acc = d if acc is None else acc + d
        if between is not None:
            between()
    return acc


def _log_sigmoid(z):
    return jnp.minimum(z, 0.0) - jnp.log1p(jnp.exp(-jnp.abs(z)))


def _pre_kernel(x_ref, g1_ref, wgu_ref, wdn_ref, gm_ref, win_ref, wzg_ref, wgate_ref, bgate_ref,
                h_ref, qk_ref, v_ref, la_ref, r_ref, us5_ref):
    x = x_ref[...]
    xn = _rms(x, g1_ref[...]).astype(BF16)
    h = x + 0.5 * _swiglu(xn, wgu_ref, wdn_ref)
    h_ref[...] = h
    un = _rms(h, gm_ref[...]).astype(BF16)
    p = _dot(un, win_ref[...])
    qk_ref[...] = p[:, 0:512]
    v_ref[...] = p[:, 512:1024]
    r_ref[...] = p[:, 1024:1536]
    us5_ref[...] = p[:, 1536:2048]
    zg = _dot(un, wzg_ref[...])
    z = _dot(zg.astype(BF16), wgate_ref[...]) + bgate_ref[...]
    la_ref[...] = _log_sigmoid(z) * (1.0 / GLA_TAU)


def _pre_call(x2d, g1, wgu, wdn, gm, win, wzg, wgate, bgate, tm):
    n = x2d.shape[0]
    row = lambda w: pl.BlockSpec((tm, w), lambda i: (i, 0))
    out_shapes = [jax.ShapeDtypeStruct((n, w), F32) for w in (D_MODEL, 512, 512, 256, 512, 512)]
    return pl.pallas_call(
        _pre_kernel,
        grid=(n // tm,),
        in_specs=[row(D_MODEL), _const_spec(g1.shape), _const_spec(wgu.shape), _const_spec(wdn.shape),
                  _const_spec(gm.shape), _const_spec(win.shape), _const_spec(wzg.shape),
                  _const_spec(wgate.shape), _const_spec(bgate.shape)],
        out_specs=[row(D_MODEL), row(512), row(512), row(256), row(512), row(512)],
        out_shape=out_shapes,
        compiler_params=_cparams(1),
        name="pre_ffn_proj",
    )(x2d, g1, wgu, wdn, gm, win, wzg, wgate, bgate)


def _memkv_kernel(m_ref, g_ref, w_ref, k_ref, v_ref, k4_ref, v4_ref):
    mn = _rms(m_ref[...], g_ref[...]).astype(BF16)
    kv = _dot(mn, w_ref[...])
    k_ref[...] = kv[:, :D_MODEL]
    v_ref[...] = kv[:, D_MODEL:]
    for hd in range(CROSS_HEADS):
        k4_ref[:, hd, :] = kv[:, hd * CROSS_DH:(hd + 1) * CROSS_DH]
        v4_ref[:, hd, :] = kv[:, D_MODEL + hd * CROSS_DH:D_MODEL + (hd + 1) * CROSS_DH]


def _memkv_call(mem2d, g, w, tm):
    n = mem2d.shape[0]
    row = pl.BlockSpec((tm, D_MODEL), lambda i: (i, 0))
    row4 = pl.BlockSpec((tm, CROSS_HEADS, CROSS_DH), lambda i: (i, 0, 0))
    return pl.pallas_call(
        _memkv_kernel,
        grid=(n // tm,),
        in_specs=[row, _const_spec(g.shape), _const_spec(w.shape)],
        out_specs=[row, row, row4, row4],
        out_shape=[jax.ShapeDtypeStruct((n, D_MODEL), F32)] * 2
        + [jax.ShapeDtypeStruct((n, CROSS_HEADS, CROSS_DH), F32)] * 2,
        compiler_params=_cparams(1),
        name="memory_kv",
    )(mem2d, g, w)


def _mid_kernel(h_ref, og_ref, os_ref, wo_ref, gc_ref, wq_ref, h2_ref, q_ref):
    h2 = (h_ref[...] + _dot(og_ref[...].astype(BF16), wo_ref[0:GLA_WIDTH, :])
          + _dot(os_ref[...].astype(BF16), wo_ref[GLA_WIDTH:, :]))
    h2_ref[...] = h2
    hn = _rms(h2, gc_ref[...]).astype(BF16)
    q_ref[...] = _dot(hn, wq_ref[...])


def _mid_call(h, og, os_, wo, gc, wq, tm):
    n = h.shape[0]
    row = lambda w: pl.BlockSpec((tm, w), lambda i: (i, 0))
    return pl.pallas_call(
        _mid_kernel,
        grid=(n // tm,),
        in_specs=[row(D_MODEL), row(512), row(512), _const_spec(wo.shape), _const_spec(gc.shape),
                  _const_spec(wq.shape)],
        out_specs=[row(D_MODEL), row(D_MODEL)],
        out_shape=[jax.ShapeDtypeStruct((n, D_MODEL), F32)] * 2,
        compiler_params=_cparams(1),
        name="mix_out_cross_q",
    )(h, og, os_, wo, gc, wq)


def _post_kernel(h2_ref, o_ref, wco_ref, g2_ref, wgu_ref, wdn_ref, gf_ref, y_ref):
    h3 = h2_ref[...] + _dot(o_ref[...].astype(BF16), wco_ref[...])
    hn = _rms(h3, g2_ref[...]).astype(BF16)
    h4 = h3 + 0.5 * _swiglu(hn, wgu_ref, wdn_ref)
    y_ref[...] = _rms(h4, gf_ref[...])


def _post_call(h2, o, wco, g2, wgu, wdn, gf, tm):
    n = h2.shape[0]
    row = pl.BlockSpec((tm, D_MODEL), lambda i: (i, 0))
    return pl.pallas_call(
        _post_kernel,
        grid=(n // tm,),
        in_specs=[row, row, _const_spec(wco.shape), _const_spec(g2.shape), _const_spec(wgu.shape),
                  _const_spec(wdn.shape), _const_spec(gf.shape)],
        out_specs=row,
        out_shape=jax.ShapeDtypeStruct((n, D_MODEL), F32),
        compiler_params=_cparams(1),
        name="post_ffn_final",
    )(h2, o, wco, g2, wgu, wdn, gf)


def _tail_kernel(h_ref, og_ref, os_ref, k_ref, v_ref, wo_ref, gc_ref, wq_ref, wco_ref, g2_ref, wgu_ref,
                 wdn_ref, gf_ref, y_ref):
    h2 = (h_ref[0] + _dot(og_ref[0].astype(BF16), wo_ref[0:GLA_WIDTH, :])
          + _dot(os_ref[0].astype(BF16), wo_ref[GLA_WIDTH:, :]))
    q = _dot(_rms(h2, gc_ref[...]).astype(BF16), wq_ref[...]).astype(BF16)
    heads = [slice(hd * CROSS_DH, (hd + 1) * CROSS_DH) for hd in range(CROSS_HEADS)]
    scores = [_dot_nt(q[:, sl], k_ref[0, :, sl].astype(BF16)) * (CROSS_DH ** -0.5) for sl in heads]
    probs = [_softmax(s).astype(BF16) for s in scores]
    outs = [_dot(p, v_ref[0, :, sl].astype(BF16)).astype(BF16) for p, sl in zip(probs, heads)]
    h3 = h2 + _dot(jnp.concatenate(outs, axis=-1), wco_ref[...])
    hn = _rms(h3, g2_ref[...]).astype(BF16)
    h4 = h3 + 0.5 * _swiglu(hn, wgu_ref, wdn_ref)
    y_ref[0] = _rms(h4, gf_ref[...])


def _tail_call(h3d, og3, os3, k3, v3, w, tm):
    s, t, _ = h3d.shape
    row = lambda wd: pl.BlockSpec((1, tm, wd), lambda i, j: (i, j, 0))
    kvspec = pl.BlockSpec((1, MEM_LEN, D_MODEL), lambda i, j: (i, 0, 0))
    consts = [w['w_out'], w['g_cross'], w['w_cq'], w['w_co'], w['g_ffn2'], w['w_ffn2_gu'], w['w_ffn2_down'],
              w['g_final']]
    return pl.pallas_call(
        _tail_kernel,
        grid=(s, t // tm),
        in_specs=[row(D_MODEL), row(GLA_WIDTH), row(S5_WIDTH), kvspec, kvspec] + [_const_spec(a.shape) for a in consts],
        out_specs=row(D_MODEL),
        out_shape=jax.ShapeDtypeStruct((s, t, D_MODEL), F32),
        compiler_params=_cparams(2),
        name="tail_attn_ffn",
    )(h3d, og3, os3, k3, v3, *consts)


def _softmax(s):
    e = jnp.exp(s - jnp.max(s, axis=-1, keepdims=True))
    return e / jnp.sum(e, axis=-1, keepdims=True)


def _load_mem(ref, j):
    return jnp.concatenate([ref[j, :, hd, :] for hd in range(CROSS_HEADS)], axis=-1).astype(BF16)


def _attn_packed_kernel(q_ref, k_ref, v_ref, o_ref, *, n_seq):
    t = q_ref.shape[1]
    shape = (CROSS_HEADS * t, D_MODEL)
    q_mask = (_iota2(shape, 0) // t) == (_iota2(shape, 1) // CROSS_DH)
    o_lane_head = _iota2((t, D_MODEL), 1) // CROSS_DH
    scores = []
    for j in range(n_seq):
        q_stack = jnp.where(q_mask, jnp.concatenate([q_ref[j]] * CROSS_HEADS, axis=0), 0.0).astype(BF16)
        scores.append(_dot_nt(q_stack, _load_mem(k_ref, j)) * (CROSS_DH ** -0.5))
    probs = [_softmax(s).astype(BF16) for s in scores]
    for j in range(n_seq):
        full = _dot(probs[j], _load_mem(v_ref, j))
        o = jnp.zeros((t, D_MODEL), F32)
        for hd in range(CROSS_HEADS):
            o = jnp.where(o_lane_head == hd, full[hd * t:(hd + 1) * t, :], o)
        o_ref[j] = o.astype(o_ref.dtype)


def _attn_call(q3, k3, v3, n_seq):
    s, t, _ = q3.shape
    assert t * CROSS_HEADS <= MXU_DIM and s % n_seq == 0
    qspec = pl.BlockSpec((n_seq, t, D_MODEL), lambda i: (i, 0, 0))
    kvspec = pl.BlockSpec((n_seq, MEM_LEN, CROSS_HEADS, CROSS_DH), lambda i: (i, 0, 0, 0))
    return pl.pallas_call(
        functools.partial(_attn_packed_kernel, n_seq=n_seq),
        grid=(s // n_seq,),
        in_specs=[qspec, kvspec, kvspec],
        out_specs=qspec,
        out_shape=jax.ShapeDtypeStruct(q3.shape, F32),
        compiler_params=_cparams(1),
        name="cross_attention",
    )(q3, k3, v3)


def _split3(x):
    hi = x.astype(BF16)
    r1 = x - hi.astype(F32)
    mid = r1.astype(BF16)
    lo = (r1 - mid.astype(F32)).astype(BF16)
    return hi, mid, lo


def _dot01(m01, x):
    hi, mid, lo = _split3(x)
    return _dot(m01, hi) + _dot(m01, mid) + _dot(m01, lo)


def _iota2(shape, dim):
    return lax.broadcasted_iota(jnp.int32, shape, dim)


def _gla_stages(qk_ref, v_ref, la_ref, r_ref, o_ref, g_ref, tri_ref, lvl_ref, khm_ref, vhm_ref, shm_ref,
                states, put_states, *, seq_len, n_chunks):
    c_rows = GLA_CHUNK_ROWS
    n_sub = c_rows // seq_len
    log2 = lambda n: int(math.log2(n))
    tri = tri_ref[...]
    lvl = lvl_ref[...]
    rowq = _iota2((c_rows, GLA_QK_WIDTH), 0)
    u_head = [(_iota2((GLA_DV, GLA_QK_WIDTH), 1) >> log2(GLA_DK)) == hd for hd in range(GLA_HEADS)]

    def head_blocks(x, mask_ref):
        return jnp.concatenate([x.astype(BF16)] * GLA_HEADS, axis=0) * mask_ref[...]

    g_out = g_ref[...]
    chunks = range(n_chunks)
    rows_of = lambda c: slice(c * c_rows, (c + 1) * c_rows)
    sub_of = lambda j: slice(j * seq_len, (j + 1) * seq_len)
    q = [qk_ref[rows_of(c), 0:GLA_QK_WIDTH] * (GLA_DK ** -0.5) for c in chunks]
    k = [qk_ref[rows_of(c), GLA_QK_WIDTH:2 * GLA_QK_WIDTH] for c in chunks]
    v = [v_ref[rows_of(c), :] for c in chunks]
    b = [_dot01(tri, la_ref[rows_of(c), :]) for c in chunks]
    yield

    a_mat = [jnp.where(lvl == 0, _dot_nt(q[c].astype(BF16), head_blocks(k[c], khm_ref)), 0.0) for c in chunks]
    last = list(b)
    m = 1
    while m < seq_len:
        yield
        even = ((rowq >> log2(m)) & 1) == 0
        for c in chunks:
            prev_last = pltpu.roll(last[c], m, axis=0)
            e = jnp.exp(jnp.minimum(jnp.where(even, last[c] - b[c], b[c] - prev_last), 0.0))
            a_lvl = _dot_nt((q[c] * e).astype(BF16), head_blocks(k[c] * e, khm_ref))
            a_mat[c] = jnp.where(lvl == log2(m) + 1, a_lvl, a_mat[c])
            last[c] = jnp.where(even, pltpu.roll(last[c], c_rows - m, axis=0), last[c])
        m *= 2
    yield
    o_intra = [_dot(a_mat[c].astype(BF16), head_blocks(v[c], vhm_ref)) for c in chunks]
    yield

    narrow = (lambda a: a.astype(BF16)) if seq_len % 16 == 0 else (lambda a: a)
    q_dec = [narrow(q[c] * jnp.exp(b[c])) for c in chunks]
    k_dec = [narrow(k[c] * jnp.exp(jnp.minimum(last[c] - b[c], 0.0))) for c in chunks]
    u_t = {}
    for c in chunks:
        vc = narrow(v[c])
        for j in range(n_sub):
            u_full = _dot_tn(vc[sub_of(j)], k_dec[c][sub_of(j)])
            acc = jnp.zeros((GLA_DV, GLA_QK_WIDTH), F32)
            for hd in range(GLA_HEADS):
                acc = jnp.where(u_head[hd], u_full[hd * GLA_DV:(hd + 1) * GLA_DV, :], acc)
            u_t[c, j] = acc

    states = list(states)
    state_of = (lambda c, j: c * n_sub + j) if n_sub > 1 else (lambda c, j: 0)
    for c in chunks:
        if c % 2 == 0:
            yield
        o_parts = []
        for j in range(n_sub):
            sj = state_of(c, j)
            o_parts.append(_dot_nt(q_dec[c][sub_of(j)], head_blocks(states[sj], shm_ref)))
            decay = jnp.exp(last[c][j * seq_len:j * seq_len + 1, :])
            states[sj] = states[sj] * decay + u_t[c, j]
        o_inter = o_parts[0] if n_sub == 1 else jnp.concatenate(o_parts, axis=0)
        o = o_intra[c] + o_inter
        r = r_ref[rows_of(c), :]
        outs = []
        for hd in range(GLA_HEADS):
            sl = slice(hd * GLA_DV, (hd + 1) * GLA_DV)
            outs.append(_rms(o[:, sl], g_out) * jax.nn.silu(r[:, sl]))
        o_ref[rows_of(c), :] = jnp.concatenate(outs, axis=-1).astype(o_ref.dtype)
    put_states(states)


def _gla_kernel(*refs, seq_len, n_chunks, n_state, has_s0):
    ins, rest = refs[:10], refs[10:]
    qk_ref, v_ref, la_ref, r_ref, g_ref = ins[:5]
    if has_s0:
        s0_ref, o_ref, st_ref, s_scr = rest
    else:
        o_ref, st_ref, s_scr = rest
        s0_ref = None
    t_idx = pl.program_id(1)

    @pl.when(t_idx == 0)
    def _():
        if has_s0:
            for j in range(n_state):
                s_scr[j] = s0_ref[j].reshape(GLA_QK_WIDTH, GLA_DV).T
        else:
            s_scr[...] = jnp.zeros_like(s_scr)

    def put_states(states):
        for j, s_t in enumerate(states):
            s_scr[j] = s_t

    for _ in _gla_stages(qk_ref.at[0], v_ref.at[0], la_ref.at[0], r_ref.at[0], o_ref.at[0], g_ref, *ins[5:],
                         [s_scr[j] for j in range(n_state)], put_states, seq_len=seq_len, n_chunks=n_chunks):
        pass

    @pl.when(t_idx == pl.num_programs(1) - 1)
    def _():
        for j in range(n_state):
            st_ref[j] = s_scr[j].T.reshape(GLA_HEADS, GLA_DK, GLA_DV)


def _gla_index_constants(seq_len):
    c = GLA_CHUNK_ROWS
    i = np.arange(c)[:, None]
    s = np.arange(c)[None, :]
    same_seq = (i // seq_len) == (s // seq_len)
    tri = (same_seq & (s <= i)).astype(np.float32)
    top_bit = np.floor(np.log2(np.maximum(i ^ s, 1))).astype(np.int32)
    lvl = np.where(same_seq & (s < i), top_bit + 1, -1)
    lvl = np.where(i == s, 0, lvl).astype(np.int32)
    head_mask = lambda rows_per_head, lanes_per_head: (
        (np.arange(GLA_HEADS * rows_per_head)[:, None] // rows_per_head)
        == (np.arange(GLA_HEADS * lanes_per_head)[None, :] // lanes_per_head)).astype(np.float32)
    return [jnp.asarray(tri, BF16), jnp.asarray(np.tile(lvl, (1, GLA_HEADS))),
            jnp.asarray(head_mask(c, GLA_DK), BF16), jnp.asarray(head_mask(c, GLA_DV), BF16),
            jnp.asarray(head_mask(GLA_DV, GLA_DK), BF16)]


def _gla_call(qk3, v3, la3, r3, g_out, s0, seq_len, tile_rows, n_state):
    n_outer, rows, _ = qk3.shape
    n_chunks = tile_rows // GLA_CHUNK_ROWS
    spec = lambda w: pl.BlockSpec((1, tile_rows, w), lambda i, t: (i, t, 0))
    st_spec = pl.BlockSpec((n_state, GLA_HEADS, GLA_DK, GLA_DV), lambda i, t: (i, 0, 0, 0))
    has_s0 = s0 is not None
    consts = [g_out] + _gla_index_constants(seq_len)
    in_specs = [spec(512), spec(512), spec(256), spec(512)] + [_const_spec(a.shape) for a in consts]
    args = [qk3, v3, la3, r3] + consts
    if has_s0:
        in_specs.append(st_spec)
        args.append(s0)
    return pl.pallas_call(
        functools.partial(_gla_kernel, seq_len=seq_len, n_chunks=n_chunks, n_state=n_state, has_s0=has_s0),
        grid=(n_outer, rows // tile_rows),
        in_specs=in_specs,
        out_specs=[spec(512), st_spec],
        out_shape=[jax.ShapeDtypeStruct((n_outer, rows, GLA_WIDTH), BF16),
                   jax.ShapeDtypeStruct((n_outer * n_state, GLA_HEADS, GLA_DK, GLA_DV), F32)],
        scratch_shapes=[pltpu.VMEM((n_state, GLA_DV, GLA_QK_WIDTH), F32)],
        compiler_params=_cparams(2),
        name="gla_mixer",
    )(*args)


def _pre_gla_kernel(x_ref, g1_ref, wgu_ref, wdn_ref, gm_ref, win_ref, wzg_ref, wgate_ref, bgate_ref, gout_ref,
                    tri_ref, lvl_ref, khm_ref, vhm_ref, shm_ref,
                    h_ref, us5_ref, og_ref, st_ref,
                    qk_scr, v_scr, la_scr, r_scr, s_scr, *, tiles_per_seq):
    i = pl.program_id(0)
    last_step = pl.num_programs(0) - 1
    n_chunks = x_ref.shape[0] // GLA_CHUNK_ROWS

    @pl.when(i == 0)
    def _():
        for scr in (qk_scr, v_scr, la_scr, r_scr, s_scr):
            scr[...] = jnp.zeros_like(scr)

    def gla_stages():
        first_of_seq = lax.rem(i + (tiles_per_seq - 1), tiles_per_seq) == 0
        state = jnp.where(first_of_seq, 0.0, s_scr[0])

        def put_states(states):
            s_scr[0] = states[0]
            st_ref[0] = states[0].T.reshape(GLA_HEADS, GLA_DK, GLA_DV)

        return _gla_stages(qk_scr, v_scr, la_scr, r_scr, og_ref, gout_ref, tri_ref, lvl_ref, khm_ref, vhm_ref,
                           shm_ref, [state], put_states, seq_len=GLA_CHUNK_ROWS, n_chunks=n_chunks)

    @pl.when(i < last_step)
    def _():
        stages = gla_stages()
        next(stages)
        x = x_ref[...]
        xn = _rms(x, g1_ref[...]).astype(BF16)
        h = x + 0.5 * _swiglu(xn, wgu_ref, wdn_ref, between=lambda: next(stages, None))
        h_ref[...] = h
        un = _rms(h, gm_ref[...]).astype(BF16)
        p = []
        for grp in range(4):
            p.append(_dot(un, win_ref[:, grp * 512:(grp + 1) * 512]))
            next(stages, None)
        zg = _dot(un, wzg_ref[...])
        z = _dot(zg.astype(BF16), wgate_ref[...]) + bgate_ref[...]
        for _ in stages:
            pass
        qk_scr[...] = p[0]
        v_scr[...] = p[1]
        r_scr[...] = p[2]
        us5_ref[...] = p[3]
        la_scr[...] = _log_sigmoid(z) * (1.0 / GLA_TAU)

    @pl.when(i == last_step)
    def _():
        for _ in gla_stages():
            pass


def _pre_gla_call(x2d, w, tm, seq_rows):
    n = x2d.shape[0]
    n_tiles = n // tm
    tiles_per_seq = seq_rows // tm
    cur = lambda wd: pl.BlockSpec((tm, wd), lambda i: (jnp.minimum(i, n_tiles - 1), 0))
    prev = lambda wd: pl.BlockSpec((tm, wd), lambda i: (jnp.maximum(i - 1, 0), 0))
    st_spec = pl.BlockSpec((1, GLA_HEADS, GLA_DK, GLA_DV),
                           lambda i: (jnp.maximum(i - 1, 0) // tiles_per_seq, 0, 0, 0))
    consts = [w['g_ffn1'], w['w_ffn1_gu'], w['w_ffn1_down'], w['g_mix'], w['w_in_main'], w['w_zg'], w['w_gate'],
              w['b_gate'], w['g_gla_out']] + _gla_index_constants(GLA_CHUNK_ROWS)
    return pl.pallas_call(
        functools.partial(_pre_gla_kernel, tiles_per_seq=tiles_per_seq),
        grid=(n_tiles + 1,),
        in_specs=[cur(D_MODEL)] + [_const_spec(a.shape) for a in consts],
        out_specs=[cur(D_MODEL), cur(S5_WIDTH), prev(GLA_WIDTH), st_spec],
        out_shape=[jax.ShapeDtypeStruct((n, D_MODEL), F32), jax.ShapeDtypeStruct((n, S5_WIDTH), F32),
                   jax.ShapeDtypeStruct((n, GLA_WIDTH), BF16),
                   jax.ShapeDtypeStruct((n // seq_rows, GLA_HEADS, GLA_DK, GLA_DV), F32)],
        scratch_shapes=[pltpu.VMEM((tm, 2 * GLA_QK_WIDTH), F32), pltpu.VMEM((tm, GLA_WIDTH), F32),
                        pltpu.VMEM((tm, GLA_QK_WIDTH), F32), pltpu.VMEM((tm, GLA_WIDTH), F32),
                        pltpu.VMEM((1, GLA_DV, GLA_QK_WIDTH), F32)],
        compiler_params=_cparams(1),
        name="pre_ffn_gla",
    )(x2d, *consts)


def _s5_disc_kernel(lre_ref, lim_ref, ldt_ref, bre_ref, bim_ref, are_ref, aim_ref, bbre_ref, bbim_ref):
    lr = lre_ref[...]
    li = lim_ref[...]
    dt = jnp.exp(ldt_ref[...])
    mag = jnp.exp(dt * lr)
    ab_re = mag * jnp.cos(dt * li)
    ab_im = mag * jnp.sin(dt * li)
    den = lr * lr + li * li
    coef_re = ((ab_re - 1.0) * lr + ab_im * li) / den
    coef_im = (ab_im * lr - (ab_re - 1.0) * li) / den
    are_ref[...] = ab_re
    aim_ref[...] = ab_im
    br = bre_ref[...]
    bi = bim_ref[...]
    cr = coef_re[:, None, :]
    cim = coef_im[:, None, :]
    bbre_ref[...] = cr * br - cim * bi
    bbim_ref[...] = cr * bi + cim * br


def _s5_disc_call(lre, lim, ldt, bre_t, bim_t):
    gn = jax.ShapeDtypeStruct((S5_GROUPS, S5_N), F32)
    gpn = jax.ShapeDtypeStruct((S5_GROUPS, S5_P, S5_N), F32)
    return pl.pallas_call(
        _s5_disc_kernel,
        out_shape=[gn, gn, gpn, gpn],
        name="s5_discretise",
    )(lre, lim, ldt, bre_t, bim_t)


def _to_time_major(u_ref, perm_scr, n_seq, t_tile):
    if n_seq == SUBLANES:
        n_slab = S5_WIDTH // LANES
        for j in range(n_seq):
            for sb in range(n_slab):
                perm_scr[sb, pl.ds(j, t_tile, stride=n_seq), :] = u_ref[j, :, sb * LANES:(sb + 1) * LANES]
        return jnp.concatenate([perm_scr[sb] for sb in range(n_slab)], axis=-1)
    return jnp.concatenate([u_ref[:, t, :] for t in range(t_tile)], axis=0)


def _from_time_major(y, o_ref, perm_scr, n_seq, t_tile):
    if n_seq == SUBLANES:
        n_slab = S5_WIDTH // LANES
        for sb in range(n_slab):
            perm_scr[sb] = y[:, sb * LANES:(sb + 1) * LANES]
        for j in range(n_seq):
            for sb in range(n_slab):
                o_ref[j, :, sb * LANES:(sb + 1) * LANES] = perm_scr[sb, pl.ds(j, t_tile, stride=n_seq), :]
    else:
        for t in range(t_tile):
            o_ref[:, t, :] = y[t * n_seq:(t + 1) * n_seq, :]


def _s5_kernel(u_ref, x0re_ref, x0im_ref, are_ref, aim_ref, bre_ref, bim_ref, c_ref, d_ref, wglu_ref,
               bglu_ref, o_ref, xre_ref, xim_ref, sre_scr, sim_scr, xre_scr, xim_scr, perm_scr, *, n_seq, t_tile):
    half_w = S5_WIDTH // 2
    half_s = S5_STATE // 2

    @pl.when(pl.program_id(0) == 0)
    def _():
        xre_scr[...] = jnp.concatenate([x0re_ref[:, g, :] for g in range(S5_GROUPS)], axis=-1)
        xim_scr[...] = jnp.concatenate([x0im_ref[:, g, :] for g in range(S5_GROUPS)], axis=-1)

    rows = t_tile * n_seq
    u = _to_time_major(u_ref, perm_scr, n_seq, t_tile)
    ub = u.astype(BF16)
    for hf in range(2):
        uh = ub[:, hf * half_w:(hf + 1) * half_w]
        st = slice(hf * half_s, (hf + 1) * half_s)
        sre_scr[:, :, st] = _dot(uh, bre_ref[hf]).reshape(t_tile, n_seq, half_s)
        sim_scr[:, :, st] = _dot(uh, bim_ref[hf]).reshape(t_tile, n_seq, half_s)

    for lc in range(S5_STATE // S5_LANE_CHUNK):
        lanes = slice(lc * S5_LANE_CHUNK, (lc + 1) * S5_LANE_CHUNK)
        ar = jnp.broadcast_to(are_ref[:, lanes], (SUBLANES, S5_LANE_CHUNK))
        ai = jnp.broadcast_to(aim_ref[:, lanes], (SUBLANES, S5_LANE_CHUNK))
        for sc in range(n_seq // SUBLANES):
            srows = slice(sc * SUBLANES, (sc + 1) * SUBLANES)
            xr = xre_scr[srows, lanes]
            xi = xim_scr[srows, lanes]
            for t in range(t_tile):
                nxr = ar * xr - ai * xi + sre_scr[t, srows, lanes]
                nxi = ar * xi + ai * xr + sim_scr[t, srows, lanes]
                sre_scr[t, srows, lanes] = nxr
                sim_scr[t, srows, lanes] = nxi
                xr, xi = nxr, nxi
            xre_scr[srows, lanes] = xr
            xim_scr[srows, lanes] = xi

    xr_all = sre_scr[...].reshape(rows, S5_STATE).astype(BF16)
    xi_all = sim_scr[...].reshape(rows, S5_STATE).astype(BF16)
    ys = []
    for hf in range(2):
        st = slice(hf * half_s, (hf + 1) * half_s)
        ys.append(_dot(xr_all[:, st], c_ref[hf, 0:half_s, :]) + _dot(xi_all[:, st], c_ref[hf, half_s:, :]))
    y = jnp.concatenate(ys, axis=-1) + d_ref[...] * u
    z = jax.nn.gelu(y)
    gate = jax.nn.sigmoid(_dot(z.astype(BF16), wglu_ref[...]) + bglu_ref[...])
    _from_time_major(z * gate, o_ref, perm_scr, n_seq, t_tile)

    @pl.when(pl.program_id(0) == pl.num_programs(0) - 1)
    def _():
        for g in range(S5_GROUPS):
            xre_ref[:, g, :] = xre_scr[:, g * S5_N:(g + 1) * S5_N]
            xim_ref[:, g, :] = xim_scr[:, g * S5_N:(g + 1) * S5_N]


def _s5_call(u3, x0re, x0im, are, aim, bre, bim, cmat, dskip, wglu, bglu, t_tile):
    n_seq, t, _ = u3.shape
    uspec = pl.BlockSpec((n_seq, t_tile, S5_WIDTH), lambda i: (0, i, 0))
    perm_rows = t_tile * n_seq if n_seq == SUBLANES else SUBLANES
    xspec = pl.BlockSpec((n_seq, S5_GROUPS, S5_N), lambda i: (0, 0, 0))
    consts = [are, aim, bre, bim, cmat, dskip, wglu, bglu]
    return pl.pallas_call(
        functools.partial(_s5_kernel, n_seq=n_seq, t_tile=t_tile),
        grid=(t // t_tile,),
        in_specs=[uspec, xspec, xspec] + [_const_spec(a.shape) for a in consts],
        out_specs=[uspec, xspec, xspec],
        out_shape=[jax.ShapeDtypeStruct((n_seq, t, S5_WIDTH), F32),
                   jax.ShapeDtypeStruct((n_seq, S5_GROUPS, S5_N), F32),
                   jax.ShapeDtypeStruct((n_seq, S5_GROUPS, S5_N), F32)],
        scratch_shapes=[pltpu.VMEM((t_tile, n_seq, S5_STATE), F32), pltpu.VMEM((t_tile, n_seq, S5_STATE), F32),
                        pltpu.VMEM((n_seq, S5_STATE), F32), pltpu.VMEM((n_seq, S5_STATE), F32),
                        pltpu.VMEM((S5_WIDTH // LANES, perm_rows, LANES), F32)],
        compiler_params=_cparams(1),
        name="s5_mixer",
    )(u3, x0re, x0im, *consts)


def _block_diag(blocks):
    g, a, b = blocks.shape
    spread = np.tile(np.eye(b, dtype=np.float32), (1, g))
    keep = (np.arange(g * a)[:, None] // a) == (np.arange(g * b)[None, :] // b)
    return jnp.where(keep, jnp.dot(blocks.reshape(g * a, b), spread), 0.0)


def _halves_block_diag(blocks):
    g = blocks.shape[0]
    return jnp.stack([_block_diag(blocks[:g // 2]), _block_diag(blocks[g // 2:])])


def _trunk_group(x3, s_gla0, x0re, x0im, mem_k3, mem_v3, w):
    s, t, _ = x3.shape
    n = s * t
    tm = ROW_TILE
    long_seq = t >= tm
    if long_seq:
        assert s_gla0 is None and t % tm == 0
        h, us5, o_gla, st_new = _pre_gla_call(x3.reshape(n, D_MODEL), w, tm, t)
    else:
        assert GLA_CHUNK_ROWS % t == 0 and n % (SHORT_GLA_CHUNKS * GLA_CHUNK_ROWS) == 0
        h, qk, v, la, r, us5 = _pre_call(x3.reshape(n, D_MODEL), w['g_ffn1'], w['w_ffn1_gu'], w['w_ffn1_down'],
                                         w['g_mix'], w['w_in_main'], w['w_zg'], w['w_gate'], w['b_gate'], tm)
        rows = SHORT_GLA_CHUNKS * GLA_CHUNK_ROWS
        shp = lambda a: a.reshape(n // rows, rows, a.shape[-1])
        o_gla, st_new = _gla_call(shp(qk), shp(v), shp(la), shp(r), w['g_gla_out'], s_gla0, t, rows, rows // t)
    o_s5, xre, xim = _s5_call(us5.reshape(s, t, S5_WIDTH), x0re, x0im, w['s5_are'], w['s5_aim'], w['s5_bre'],
                              w['s5_bim'], w['s5_c'], w['s5_d'], w['s5_wglu'], w['s5_bglu'], min(t, S5_TILE_STEPS))
    if long_seq:
        y = _tail_call(h.reshape(s, t, D_MODEL), o_gla.reshape(s, t, GLA_WIDTH), o_s5, mem_k3, mem_v3, w, tm)
        return y, st_new, xre, xim
    o_s5 = o_s5.reshape(n, S5_WIDTH)
    h2, q = _mid_call(h, o_gla.reshape(n, GLA_WIDTH), o_s5, w['w_out'], w['g_cross'], w['w_cq'], tm)
    o_att = _attn_call(q.reshape(s, t, D_MODEL), mem_k3, mem_v3, SHORT_ATTN_SEQS)
    y = _post_call(h2, o_att.reshape(n, D_MODEL), w['w_co'], w['g_ffn2'], w['w_ffn2_gu'], w['w_ffn2_down'],
                   w['g_final'], tm)
    return y.reshape(s, t, D_MODEL), st_new, xre, xim


def kernel(x_prompt, x_sample, state_gla, state_s5_re, state_s5_im, cache_mem_k, cache_mem_v, mem_prompt, g_ffn1, w_ffn1_gu, w_ffn1_down, g_mix, w_in, w_gla_gate, b_gla_gate, g_gla_out, s5_lambda_re, s5_lambda_im, s5_log_dt, s5_b_re, s5_b_im, s5_c_re, s5_c_im, s5_d, s5_w_glu, s5_b_glu, w_out, g_cross, g_mem, w_cq, w_ckv, w_co, g_ffn2, w_ffn2_gu, w_ffn2_down, g_final):
    bp, tp, _ = x_prompt.shape
    bs, ts, _ = x_sample.shape
    l = 0
    row = lambda a: a.reshape(1, -1).astype(F32)
    bf = lambda a: a.astype(BF16)

    are, aim, bbre, bbim = _s5_disc_call(
        s5_lambda_re[l], s5_lambda_im[l], s5_log_dt[l].reshape(S5_GROUPS, 1),
        jnp.transpose(s5_b_re[l], (0, 2, 1)), jnp.transpose(s5_b_im[l], (0, 2, 1)))
    c_re_t = jnp.transpose(s5_c_re[l], (0, 2, 1))
    c_im_t = jnp.transpose(s5_c_im[l], (0, 2, 1))
    cre_h = _halves_block_diag(c_re_t)
    cim_h = _halves_block_diag(-c_im_t)
    w_in_l = w_in[l]
    zg_lo = 2 * GLA_QK_WIDTH + GLA_WIDTH
    zg_hi = zg_lo + GLA_GATE_RANK
    w = {
        'g_ffn1': row(g_ffn1[l]), 'w_ffn1_gu': bf(w_ffn1_gu[l]), 'w_ffn1_down': bf(w_ffn1_down[l]),
        'g_mix': row(g_mix[l]),
        'w_in_main': bf(jnp.concatenate([w_in_l[:, :zg_lo], w_in_l[:, zg_hi:]], axis=1)),
        'w_zg': bf(jnp.pad(w_in_l[:, zg_lo:zg_hi], ((0, 0), (0, LANES - GLA_GATE_RANK)))),
        'w_gate': bf(jnp.pad(w_gla_gate[l], ((0, LANES - GLA_GATE_RANK), (0, 0)))),
        'b_gate': row(b_gla_gate[l]),
        'g_gla_out': row(g_gla_out[l]),
        's5_are': are.reshape(1, S5_STATE), 's5_aim': aim.reshape(1, S5_STATE),
        's5_bre': bf(_halves_block_diag(bbre)), 's5_bim': bf(_halves_block_diag(bbim)),
        's5_c': bf(jnp.concatenate([cre_h, cim_h], axis=1)),
        's5_d': row(s5_d[l]), 's5_wglu': bf(_block_diag(s5_w_glu[l])), 's5_bglu': row(s5_b_glu[l]),
        'w_out': bf(w_out[l]), 'g_cross': row(g_cross[l]), 'w_cq': bf(w_cq[l]), 'w_co': bf(w_co[l]),
        'g_ffn2': row(g_ffn2[l]), 'w_ffn2_gu': bf(w_ffn2_gu[l]), 'w_ffn2_down': bf(w_ffn2_down[l]),
        'g_final': row(g_final),
    }

    mk, mv, mk4, mv4 = _memkv_call(mem_prompt.reshape(bp * MEM_LEN, D_MODEL), row(g_mem[l]), bf(w_ckv[l]), ROW_TILE)
    zeros_state = jnp.zeros((bp, S5_GROUPS, S5_N), F32)
    y_p, st_p, re_p, im_p = _trunk_group(
        x_prompt, None, zeros_state, zeros_state, mk.reshape(bp, MEM_LEN, D_MODEL), mv.reshape(bp, MEM_LEN, D_MODEL),
        w)

    s0 = state_gla.reshape(DEPTH * bs, GLA_HEADS, GLA_DK, GLA_DV)[l * bs:(l + 1) * bs]
    y_s, st_s, re_s, im_s = _trunk_group(
        x_sample, s0, state_s5_re.reshape(DEPTH * bs, S5_GROUPS, S5_N)[l * bs:(l + 1) * bs],
        state_s5_im.reshape(DEPTH * bs, S5_GROUPS, S5_N)[l * bs:(l + 1) * bs],
        cache_mem_k.reshape(DEPTH * bs, MEM_LEN, CROSS_HEADS, CROSS_DH)[l * bs:(l + 1) * bs],
        cache_mem_v.reshape(DEPTH * bs, MEM_LEN, CROSS_HEADS, CROSS_DH)[l * bs:(l + 1) * bs],
        w)

    def gla_state_out(st, nb):
        return st[None]

    s5_out = lambda a, nb: a[None]
    kv_out = lambda a: a.reshape(1, bp, MEM_LEN, CROSS_HEADS, CROSS_DH)
    return (y_p, y_s,
            gla_state_out(st_p, bp), s5_out(re_p, bp), s5_out(im_p, bp), kv_out(mk4), kv_out(mv4),
            gla_state_out(st_s, bs), s5_out(re_s, bs), s5_out(im_s, bs))
```

```python
import functools
import math

import numpy as np
import jax
import jax.numpy as jnp
from jax import lax
from jax.experimental import pallas as pl
from jax.experimental.pallas import tpu as pltpu

F32 = jnp.float32
BF16 = jnp.bfloat16

D_MODEL = 1024
DEPTH = 1
EPS = 1e-6
D_FF = 2816
GLA_WIDTH = 512
GLA_HEADS = 4
GLA_DV = 128
GLA_DK = 64
GLA_QK_WIDTH = 256
GLA_GATE_RANK = 16
GLA_TAU = 16.0
S5_WIDTH = 512
S5_P = 16
S5_GROUPS = 32
S5_N = 64
S5_STATE = S5_GROUPS * S5_N
CROSS_HEADS = 4
CROSS_DH = 256
MEM_LEN = 256

LANES = 128
SUBLANES = 8
MXU_DIM = 256
VMEM_LIMIT_BYTES = 56 * 1024 * 1024

ROW_TILE = 512
S5_TILE_STEPS = 128
SHORT_GLA_CHUNKS = 4
SHORT_ATTN_SEQS = 4
FF_CHUNK = MXU_DIM
GLA_CHUNK_ROWS = 64
S5_LANE_CHUNK = 512


def _cparams(n_grid_axes):
    return pltpu.CompilerParams(
        dimension_semantics=("arbitrary",) * n_grid_axes,
        vmem_limit_bytes=VMEM_LIMIT_BYTES,
    )


def _const_spec(shape):
    zeros = (0,) * len(shape)
    return pl.BlockSpec(shape, lambda *_: zeros, pipeline_mode=pl.Buffered(1))


def _dot(a, b):
    return jnp.dot(a, b, preferred_element_type=F32)


def _dot_nt(a, b):
    return lax.dot_general(a, b, (((1,), (1,)), ((), ())), preferred_element_type=F32)


def _dot_tn(a, b):
    return lax.dot_general(a, b, (((0,), (0,)), ((), ())), preferred_element_type=F32)


def _rms(x, g):
    return x * lax.rsqrt(jnp.mean(x * x, axis=-1, keepdims=True) + EPS) * g


def _swiglu(xn_bf, wgu_ref, wdn_ref, between=None):
    n_chunks = D_FF // FF_CHUNK

    def gate_up(c):
        lo, hi = c * FF_CHUNK, (c + 1) * FF_CHUNK
        return _dot(xn_bf, wgu_ref[:, lo:hi]), _dot(xn_bf, wgu_ref[:, D_FF + lo:D_FF + hi])

    acc = None
    nxt = gate_up(0)
    for c in range(n_chunks):
        g, u = nxt
        if c + 1 < n_chunks:
            nxt = gate_up(c + 1)
        a = (jax.nn.silu(g) * u).astype(BF16)
        d = _dot(a, wdn_ref[c * FF_CHUNK:(c + 1) * FF_CHUNK, :])
        acc = d if acc is None else acc + d
        if between is not None:
            between()
    return acc


def _log_sigmoid(z):
    return jnp.minimum(z, 0.0) - jnp.log1p(jnp.exp(-jnp.abs(z)))


def _pre_kernel(x_ref, g1_ref, wgu_ref, wdn_ref, gm_ref, win_ref, wzg_ref, wgate_ref, bgate_ref,
                h_ref, qk_ref, v_ref, la_ref, r_ref, us5_ref):
    x = x_ref[...]
    xn = _rms(x, g1_ref[...]).astype(BF16)
    h = x + 0.5 * _swiglu(xn, wgu_ref, wdn_ref)
    h_ref[...] = h
    un = _rms(h, gm_ref[...]).astype(BF16)
    p = _dot(un, win_ref[...])
    qk_ref[...] = p[:, 0:512]
    v_ref[...] = p[:, 512:1024]
    r_ref[...] = p[:, 1024:1536]
    us5_ref[...] = p[:, 1536:2048]
    zg = _dot(un, wzg_ref[...])
    z = _dot(zg.astype(BF16), wgate_ref[...]) + bgate_ref[...]
    la_ref[...] = _log_sigmoid(z) * (1.0 / GLA_TAU)


def _pre_call(x2d, g1, wgu, wdn, gm, win, wzg, wgate, bgate, tm):
    n = x2d.shape[0]
    row = lambda w: pl.BlockSpec((tm, w), lambda i: (i, 0))
    out_shapes = [jax.ShapeDtypeStruct((n, w), F32) for w in (D_MODEL, 512, 512, 256, 512, 512)]
    return pl.pallas_call(
        _pre_kernel,
        grid=(n // tm,),
        in_specs=[row(D_MODEL), _const_spec(g1.shape), _const_spec(wgu.shape), _const_spec(wdn.shape),
                  _const_spec(gm.shape), _const_spec(win.shape), _const_spec(wzg.shape),
                  _const_spec(wgate.shape), _const_spec(bgate.shape)],
        out_specs=[row(D_MODEL), row(512), row(512), row(256), row(512), row(512)],
        out_shape=out_shapes,
        compiler_params=_cparams(1),
        name="pre_ffn_proj",
    )(x2d, g1, wgu, wdn, gm, win, wzg, wgate, bgate)


def _memkv_kernel(m_ref, g_ref, w_ref, k_ref, v_ref, k4_ref, v4_ref):
    mn = _rms(m_ref[...], g_ref[...]).astype(BF16)
    kv = _dot(mn, w_ref[...])
    k_ref[...] = kv[:, :D_MODEL]
    v_ref[...] = kv[:, D_MODEL:]
    for hd in range(CROSS_HEADS):
        k4_ref[:, hd, :] = kv[:, hd * CROSS_DH:(hd + 1) * CROSS_DH]
        v4_ref[:, hd, :] = kv[:, D_MODEL + hd * CROSS_DH:D_MODEL + (hd + 1) * CROSS_DH]


def _memkv_call(mem2d, g, w, tm):
    n = mem2d.shape[0]
    row = pl.BlockSpec((tm, D_MODEL), lambda i: (i, 0))
    row4 = pl.BlockSpec((tm, CROSS_HEADS, CROSS_DH), lambda i: (i, 0, 0))
    return pl.pallas_call(
        _memkv_kernel,
        grid=(n // tm,),
        in_specs=[row, _const_spec(g.shape), _const_spec(w.shape)],
        out_specs=[row, row, row4, row4],
        out_shape=[jax.ShapeDtypeStruct((n, D_MODEL), F32)] * 2
        + [jax.ShapeDtypeStruct((n, CROSS_HEADS, CROSS_DH), F32)] * 2,
        compiler_params=_cparams(1),
        name="memory_kv",
    )(mem2d, g, w)


def _mid_kernel(h_ref, og_ref, os_ref, wo_ref, gc_ref, wq_ref, h2_ref, q_ref):
    h2 = (h_ref[...] + _dot(og_ref[...].astype(BF16), wo_ref[0:GLA_WIDTH, :])
          + _dot(os_ref[...].astype(BF16), wo_ref[GLA_WIDTH:, :]))
    h2_ref[...] = h2
    hn = _rms(h2, gc_ref[...]).astype(BF16)
    q_ref[...] = _dot(hn, wq_ref[...])


def _mid_call(h, og, os_, wo, gc, wq, tm):
    n = h.shape[0]
    row = lambda w: pl.BlockSpec((tm, w), lambda i: (i, 0))
    return pl.pallas_call(
        _mid_kernel,
        grid=(n // tm,),
        in_specs=[row(D_MODEL), row(512), row(512), _const_spec(wo.shape), _const_spec(gc.shape),
                  _const_spec(wq.shape)],
        out_specs=[row(D_MODEL), row(D_MODEL)],
        out_shape=[jax.ShapeDtypeStruct((n, D_MODEL), F32)] * 2,
        compiler_params=_cparams(1),
        name="mix_out_cross_q",
    )(h, og, os_, wo, gc, wq)


def _post_kernel(h2_ref, o_ref, wco_ref, g2_ref, wgu_ref, wdn_ref, gf_ref, y_ref):
    h3 = h2_ref[...] + _dot(o_ref[...].astype(BF16), wco_ref[...])
    hn = _rms(h3, g2_ref[...]).astype(BF16)
    h4 = h3 + 0.5 * _swiglu(hn, wgu_ref, wdn_ref)
    y_ref[...] = _rms(h4, gf_ref[...])


def _post_call(h2, o, wco, g2, wgu, wdn, gf, tm):
    n = h2.shape[0]
    row = pl.BlockSpec((tm, D_MODEL), lambda i: (i, 0))
    return pl.pallas_call(
        _post_kernel,
        grid=(n // tm,),
        in_specs=[row, row, _const_spec(wco.shape), _const_spec(g2.shape), _const_spec(wgu.shape),
                  _const_spec(wdn.shape), _const_spec(gf.shape)],
        out_specs=row,
        out_shape=jax.ShapeDtypeStruct((n, D_MODEL), F32),
        compiler_params=_cparams(1),
        name="post_ffn_final",
    )(h2, o, wco, g2, wgu, wdn, gf)


def _tail_kernel(h_ref, og_ref, os_ref, k_ref, v_ref, wo_ref, gc_ref, wq_ref, wco_ref, g2_ref, wgu_ref,
                 wdn_ref, gf_ref, y_ref):
    h2 = (h_ref[0] + _dot(og_ref[0].astype(BF16), wo_ref[0:GLA_WIDTH, :])
          + _dot(os_ref[0].astype(BF16), wo_ref[GLA_WIDTH:, :]))
    q = _dot(_rms(h2, gc_ref[...]).astype(BF16), wq_ref[...]).astype(BF16)
    heads = [slice(hd * CROSS_DH, (hd + 1) * CROSS_DH) for hd in range(CROSS_HEADS)]
    scores = [_dot_nt(q[:, sl], k_ref[0, :, sl].astype(BF16)) * (CROSS_DH ** -0.5) for sl in heads]
    probs = [_softmax(s).astype(BF16) for s in scores]
    outs = [_dot(p, v_ref[0, :, sl].astype(BF16)).astype(BF16) for p, sl in zip(probs, heads)]
    h3 = h2 + _dot(jnp.concatenate(outs, axis=-1), wco_ref[...])
    hn = _rms(h3, g2_ref[...]).astype(BF16)
    h4 = h3 + 0.5 * _swiglu(hn, wgu_ref, wdn_ref)
    y_ref[0] = _rms(h4, gf_ref[...])


def _tail_call(h3d, og3, os3, k3, v3, w, tm):
    s, t, _ = h3d.shape
    row = lambda wd: pl.BlockSpec((1, tm, wd), lambda i, j: (i, j, 0))
    kvspec = pl.BlockSpec((1, MEM_LEN, D_MODEL), lambda i, j: (i, 0, 0))
    consts = [w['w_out'], w['g_cross'], w['w_cq'], w['w_co'], w['g_ffn2'], w['w_ffn2_gu'], w['w_ffn2_down'],
              w['g_final']]
    return pl.pallas_call(
        _tail_kernel,
        grid=(s, t // tm),
        in_specs=[row(D_MODEL), row(GLA_WIDTH), row(S5_WIDTH), kvspec, kvspec] + [_const_spec(a.shape) for a in consts],
        out_specs=row(D_MODEL),
        out_shape=jax.ShapeDtypeStruct((s, t, D_MODEL), F32),
        compiler_params=_cparams(2),
        name="tail_attn_ffn",
    )(h3d, og3, os3, k3, v3, *consts)


def _softmax(s):
    e = jnp.exp(s - jnp.max(s, axis=-1, keepdims=True))
    return e / jnp.sum(e, axis=-1, keepdims=True)


def _load_mem(ref, j):
    heads_major = pltpu.einshape("mhd->hmd", ref[j])
    return jnp.concatenate([heads_major[hd] for hd in range(CROSS_HEADS)], axis=-1).astype(BF16)


def _attn_packed_kernel(q_ref, k_ref, v_ref, o_ref, *, n_seq):
    t = q_ref.shape[1]
    shape = (CROSS_HEADS * t, D_MODEL)
    q_mask = (_iota2(shape, 0) // t) == (_iota2(shape, 1) // CROSS_DH)
    o_lane_head = _iota2((t, D_MODEL), 1) // CROSS_DH
    scores = []
    for j in range(n_seq):
        q_stack = jnp.where(q_mask, jnp.concatenate([q_ref[j]] * CROSS_HEADS, axis=0), 0.0).astype(BF16)
        scores.append(_dot_nt(q_stack, _load_mem(k_ref, j)) * (CROSS_DH ** -0.5))
    probs = [_softmax(s).astype(BF16) for s in scores]
    for j in range(n_seq):
        full = _dot(probs[j], _load_mem(v_ref, j))
        o = jnp.zeros((t, D_MODEL), F32)
        for hd in range(CROSS_HEADS):
            o = jnp.where(o_lane_head == hd, full[hd * t:(hd + 1) * t, :], o)
        o_ref[j] = o.astype(o_ref.dtype)


def _attn_call(q3, k3, v3, n_seq):
    s, t, _ = q3.shape
    assert t * CROSS_HEADS <= MXU_DIM and s % n_seq == 0
    qspec = pl.BlockSpec((n_seq, t, D_MODEL), lambda i: (i, 0, 0))
    kvspec = pl.BlockSpec((n_seq, MEM_LEN, CROSS_HEADS, CROSS_DH), lambda i: (i, 0, 0, 0))
    return pl.pallas_call(
        functools.partial(_attn_packed_kernel, n_seq=n_seq),
        grid=(s // n_seq,),
        in_specs=[qspec, kvspec, kvspec],
        out_specs=qspec,
        out_shape=jax.ShapeDtypeStruct(q3.shape, F32),
        compiler_params=_cparams(1),
        name="cross_attention",
    )(q3, k3, v3)


def _split3(x):
    hi = x.astype(BF16)
    r1 = x - hi.astype(F32)
    mid = r1.astype(BF16)
    lo = (r1 - mid.astype(F32)).astype(BF16)
    return hi, mid, lo


def _dot01(m01, x):
    hi, mid, lo = _split3(x)
    return _dot(m01, hi) + _dot(m01, mid) + _dot(m01, lo)


def _iota2(shape, dim):
    return lax.broadcasted_iota(jnp.int32, shape, dim)


def _gla_stages(qk_ref, v_ref, la_ref, r_ref, o_ref, g_ref, tri_ref, lvl_ref, khm_ref, vhm_ref, shm_ref,
                states, put_states, *, seq_len, n_chunks):
    c_rows = GLA_CHUNK_ROWS
    n_sub = c_rows // seq_len
    log2 = lambda n: int(math.log2(n))
    tri = tri_ref[...]
    lvl = lvl_ref[...]
    rowq = _iota2((c_rows, GLA_QK_WIDTH), 0)
    u_head = [(_iota2((GLA_DV, GLA_QK_WIDTH), 1) >> log2(GLA_DK)) == hd for hd in range(GLA_HEADS)]

    def head_blocks(x, mask_ref):
        return jnp.concatenate([x.astype(BF16)] * GLA_HEADS, axis=0) * mask_ref[...]

    g_out = g_ref[...]
    chunks = range(n_chunks)
    rows_of = lambda c: slice(c * c_rows, (c + 1) * c_rows)
    sub_of = lambda j: slice(j * seq_len, (j + 1) * seq_len)
    q = [qk_ref[rows_of(c), 0:GLA_QK_WIDTH] * (GLA_DK ** -0.5) for c in chunks]
    k = [qk_ref[rows_of(c), GLA_QK_WIDTH:2 * GLA_QK_WIDTH] for c in chunks]
    v = [v_ref[rows_of(c), :] for c in chunks]
    b = [_dot01(tri, la_ref[rows_of(c), :]) for c in chunks]
    yield

    a_mat = [jnp.where(lvl == 0, _dot_nt(q[c].astype(BF16), head_blocks(k[c], khm_ref)), 0.0) for c in chunks]
    last = list(b)
    m = 1
    while m < seq_len:
        yield
        even = ((rowq >> log2(m)) & 1) == 0
        for c in chunks:
            prev_last = pltpu.roll(last[c], m, axis=0)
            e = jnp.exp(jnp.minimum(jnp.where(even, last[c] - b[c], b[c] - prev_last), 0.0))
            a_lvl = _dot_nt((q[c] * e).astype(BF16), head_blocks(k[c] * e, khm_ref))
            a_mat[c] = jnp.where(lvl == log2(m) + 1, a_lvl, a_mat[c])
            last[c] = jnp.where(even, pltpu.roll(last[c], c_rows - m, axis=0), last[c])
        m *= 2
    yield
    o_intra = [_dot(a_mat[c].astype(BF16), head_blocks(v[c], vhm_ref)) for c in chunks]
    yield

    narrow = (lambda a: a.astype(BF16)) if seq_len % 16 == 0 else (lambda a: a)
    q_dec = [narrow(q[c] * jnp.exp(b[c])) for c in chunks]
    k_dec = [narrow(k[c] * jnp.exp(jnp.minimum(last[c] - b[c], 0.0))) for c in chunks]
    u_t = {}
    for c in chunks:
        vc = narrow(v[c])
        for j in range(n_sub):
            u_full = _dot_tn(vc[sub_of(j)], k_dec[c][sub_of(j)])
            acc = jnp.zeros((GLA_DV, GLA_QK_WIDTH), F32)
            for hd in range(GLA_HEADS):
                acc = jnp.where(u_head[hd], u_full[hd * GLA_DV:(hd + 1) * GLA_DV, :], acc)
            u_t[c, j] = acc

    states = list(states)
    state_of = (lambda c, j: c * n_sub + j) if n_sub > 1 else (lambda c, j: 0)
    for c in chunks:
        if c % 2 == 0:
            yield
        o_parts = []
        for j in range(n_sub):
            sj = state_of(c, j)
            o_parts.append(_dot_nt(q_dec[c][sub_of(j)], head_blocks(states[sj], shm_ref)))
            decay = jnp.exp(last[c][j * seq_len:j * seq_len + 1, :])
            states[sj] = states[sj] * decay + u_t[c, j]
        o_inter = o_parts[0] if n_sub == 1 else jnp.concatenate(o_parts, axis=0)
        o = o_intra[c] + o_inter
        r = r_ref[rows_of(c), :]
        outs = []
        for hd in range(GLA_HEADS):
            sl = slice(hd * GLA_DV, (hd + 1) * GLA_DV)
            outs.append(_rms(o[:, sl], g_out) * jax.nn.silu(r[:, sl]))
        o_ref[rows_of(c), :] = jnp.concatenate(outs, axis=-1).astype(o_ref.dtype)
    put_states(states)


def _gla_kernel(*refs, seq_len, n_chunks, n_state, has_s0):
    ins, rest = refs[:10], refs[10:]
    qk_ref, v_ref, la_ref, r_ref, g_ref = ins[:5]
    if has_s0:
        s0_ref, o_ref, st_ref, s_scr = rest
    else:
        o_ref, st_ref, s_scr = rest
        s0_ref = None
    t_idx = pl.program_id(1)

    @pl.when(t_idx == 0)
    def _():
        if has_s0:
            for j in range(n_state):
                s_scr[j] = s0_ref[j].reshape(GLA_QK_WIDTH, GLA_DV).T
        else:
            s_scr[...] = jnp.zeros_like(s_scr)

    def put_states(states):
        for j, s_t in enumerate(states):
            s_scr[j] = s_t

    for _ in _gla_stages(qk_ref.at[0], v_ref.at[0], la_ref.at[0], r_ref.at[0], o_ref.at[0], g_ref, *ins[5:],
                         [s_scr[j] for j in range(n_state)], put_states, seq_len=seq_len, n_chunks=n_chunks):
        pass

    @pl.when(t_idx == pl.num_programs(1) - 1)
    def _():
        for j in range(n_state):
            st_ref[j] = s_scr[j].T.reshape(GLA_HEADS, GLA_DK, GLA_DV)


def _gla_index_constants(seq_len):
    c = GLA_CHUNK_ROWS
    i = np.arange(c)[:, None]
    s = np.arange(c)[None, :]
    same_seq = (i // seq_len) == (s // seq_len)
    tri = (same_seq & (s <= i)).astype(np.float32)
    top_bit = np.floor(np.log2(np.maximum(i ^ s, 1))).astype(np.int32)
    lvl = np.where(same_seq & (s < i), top_bit + 1, -1)
    lvl = np.where(i == s, 0, lvl).astype(np.int32)
    head_mask = lambda rows_per_head, lanes_per_head: (
        (np.arange(GLA_HEADS * rows_per_head)[:, None] // rows_per_head)
        == (np.arange(GLA_HEADS * lanes_per_head)[None, :] // lanes_per_head)).astype(np.float32)
    return [jnp.asarray(tri, BF16), jnp.asarray(np.tile(lvl, (1, GLA_HEADS))),
            jnp.asarray(head_mask(c, GLA_DK), BF16), jnp.asarray(head_mask(c, GLA_DV), BF16),
            jnp.asarray(head_mask(GLA_DV, GLA_DK), BF16)]


def _gla_call(qk3, v3, la3, r3, g_out, s0, seq_len, tile_rows, n_state):
    n_outer, rows, _ = qk3.shape
    n_chunks = tile_rows // GLA_CHUNK_ROWS
    spec = lambda w: pl.BlockSpec((1, tile_rows, w), lambda i, t: (i, t, 0))
    st_spec = pl.BlockSpec((n_state, GLA_HEADS, GLA_DK, GLA_DV), lambda i, t: (i, 0, 0, 0))
    has_s0 = s0 is not None
    consts = [g_out] + _gla_index_constants(seq_len)
    in_specs = [spec(512), spec(512), spec(256), spec(512)] + [_const_spec(a.shape) for a in consts]
    args = [qk3, v3, la3, r3] + consts
    if has_s0:
        in_specs.append(st_spec)
        args.append(s0)
    return pl.pallas_call(
        functools.partial(_gla_kernel, seq_len=seq_len, n_chunks=n_chunks, n_state=n_state, has_s0=has_s0),
        grid=(n_outer, rows // tile_rows),
        in_specs=in_specs,
        out_specs=[spec(512), st_spec],
        out_shape=[jax.ShapeDtypeStruct((n_outer, rows, GLA_WIDTH), BF16),
                   jax.ShapeDtypeStruct((n_outer * n_state, GLA_HEADS, GLA_DK, GLA_DV), F32)],
        scratch_shapes=[pltpu.VMEM((n_state, GLA_DV, GLA_QK_WIDTH), F32)],
        compiler_params=_cparams(2),
        name="gla_mixer",
    )(*args)


def _pre_gla_kernel(x_ref, g1_ref, wgu_ref, wdn_ref, gm_ref, win_ref, wzg_ref, wgate_ref, bgate_ref, gout_ref,
                    tri_ref, lvl_ref, khm_ref, vhm_ref, shm_ref,
                    h_ref, us5_ref, og_ref, st_ref,
                    qk_scr, v_scr, la_scr, r_scr, s_scr, *, tiles_per_seq):
    i = pl.program_id(0)
    last_step = pl.num_programs(0) - 1
    n_chunks = x_ref.shape[0] // GLA_CHUNK_ROWS

    @pl.when(i == 0)
    def _():
        for scr in (qk_scr, v_scr, la_scr, r_scr, s_scr):
            scr[...] = jnp.zeros_like(scr)

    def gla_stages():
        first_of_seq = lax.rem(i + (tiles_per_seq - 1), tiles_per_seq) == 0
        state = jnp.where(first_of_seq, 0.0, s_scr[0])

        def put_states(states):
            s_scr[0] = states[0]
            st_ref[0] = states[0].T.reshape(GLA_HEADS, GLA_DK, GLA_DV)

        return _gla_stages(qk_scr, v_scr, la_scr, r_scr, og_ref, gout_ref, tri_ref, lvl_ref, khm_ref, vhm_ref,
                           shm_ref, [state], put_states, seq_len=GLA_CHUNK_ROWS, n_chunks=n_chunks)

    @pl.when(i < last_step)
    def _():
        stages = gla_stages()
        next(stages)
        x = x_ref[...]
        xn = _rms(x, g1_ref[...]).astype(BF16)
        h = x + 0.5 * _swiglu(xn, wgu_ref, wdn_ref, between=lambda: next(stages, None))
        h_ref[...] = h
        un = _rms(h, gm_ref[...]).astype(BF16)
        p = []
        for grp in range(4):
            p.append(_dot(un, win_ref[:, grp * 512:(grp + 1) * 512]))
            next(stages, None)
        zg = _dot(un, wzg_ref[...])
        z = _dot(zg.astype(BF16), wgate_ref[...]) + bgate_ref[...]
        for _ in stages:
            pass
        qk_scr[...] = p[0]
        v_scr[...] = p[1]
        r_scr[...] = p[2]
        us5_ref[...] = p[3]
        la_scr[...] = _log_sigmoid(z) * (1.0 / GLA_TAU)

    @pl.when(i == last_step)
    def _():
        for _ in gla_stages():
            pass


def _pre_gla_call(x2d, w, tm, seq_rows):
    n = x2d.shape[0]
    n_tiles = n // tm
    tiles_per_seq = seq_rows // tm
    cur = lambda wd: pl.BlockSpec((tm, wd), lambda i: (jnp.minimum(i, n_tiles - 1), 0))
    prev = lambda wd: pl.BlockSpec((tm, wd), lambda i: (jnp.maximum(i - 1, 0), 0))
    st_spec = pl.BlockSpec((1, GLA_HEADS, GLA_DK, GLA_DV),
                           lambda i: (jnp.maximum(i - 1, 0) // tiles_per_seq, 0, 0, 0))
    consts = [w['g_ffn1'], w['w_ffn1_gu'], w['w_ffn1_down'], w['g_mix'], w['w_in_main'], w['w_zg'], w['w_gate'],
              w['b_gate'], w['g_gla_out']] + _gla_index_constants(GLA_CHUNK_ROWS)
    return pl.pallas_call(
        functools.partial(_pre_gla_kernel, tiles_per_seq=tiles_per_seq),
        grid=(n_tiles + 1,),
        in_specs=[cur(D_MODEL)] + [_const_spec(a.shape) for a in consts],
        out_specs=[cur(D_MODEL), cur(S5_WIDTH), prev(GLA_WIDTH), st_spec],
        out_shape=[jax.ShapeDtypeStruct((n, D_MODEL), F32), jax.ShapeDtypeStruct((n, S5_WIDTH), F32),
                   jax.ShapeDtypeStruct((n, GLA_WIDTH), BF16),
                   jax.ShapeDtypeStruct((n // seq_rows, GLA_HEADS, GLA_DK, GLA_DV), F32)],
        scratch_shapes=[pltpu.VMEM((tm, 2 * GLA_QK_WIDTH), F32), pltpu.VMEM((tm, GLA_WIDTH), F32),
                        pltpu.VMEM((tm, GLA_QK_WIDTH), F32), pltpu.VMEM((tm, GLA_WIDTH), F32),
                        pltpu.VMEM((1, GLA_DV, GLA_QK_WIDTH), F32)],
        compiler_params=_cparams(1),
        name="pre_ffn_gla",
    )(x2d, *consts)


def _s5_disc_kernel(lre_ref, lim_ref, ldt_ref, bre_ref, bim_ref, are_ref, aim_ref, bbre_ref, bbim_ref):
    lr = lre_ref[...]
    li = lim_ref[...]
    dt = jnp.exp(ldt_ref[...])
    mag = jnp.exp(dt * lr)
    ab_re = mag * jnp.cos(dt * li)
    ab_im = mag * jnp.sin(dt * li)
    den = lr * lr + li * li
    coef_re = ((ab_re - 1.0) * lr + ab_im * li) / den
    coef_im = (ab_im * lr - (ab_re - 1.0) * li) / den
    are_ref[...] = ab_re
    aim_ref[...] = ab_im
    br = bre_ref[...]
    bi = bim_ref[...]
    cr = coef_re[:, None, :]
    cim = coef_im[:, None, :]
    bbre_ref[...] = cr * br - cim * bi
    bbim_ref[...] = cr * bi + cim * br


def _s5_disc_call(lre, lim, ldt, bre_t, bim_t):
    gn = jax.ShapeDtypeStruct((S5_GROUPS, S5_N), F32)
    gpn = jax.ShapeDtypeStruct((S5_GROUPS, S5_P, S5_N), F32)
    return pl.pallas_call(
        _s5_disc_kernel,
        out_shape=[gn, gn, gpn, gpn],
        name="s5_discretise",
    )(lre, lim, ldt, bre_t, bim_t)


def _to_time_major(u_ref, perm_scr, n_seq, t_tile):
    if n_seq == SUBLANES:
        n_slab = S5_WIDTH // LANES
        for j in range(n_seq):
            for sb in range(n_slab):
                perm_scr[sb, pl.ds(j, t_tile, stride=n_seq), :] = u_ref[j, :, sb * LANES:(sb + 1) * LANES]
        return jnp.concatenate([perm_scr[sb] for sb in range(n_slab)], axis=-1)
    return jnp.concatenate([u_ref[:, t, :] for t in range(t_tile)], axis=0)


def _from_time_major(y, o_ref, perm_scr, n_seq, t_tile):
    if n_seq == SUBLANES:
        n_slab = S5_WIDTH // LANES
        for sb in range(n_slab):
            perm_scr[sb] = y[:, sb * LANES:(sb + 1) * LANES]
        for j in range(n_seq):
            for sb in range(n_slab):
                o_ref[j, :, sb * LANES:(sb + 1) * LANES] = perm_scr[sb, pl.ds(j, t_tile, stride=n_seq), :]
    else:
        for t in range(t_tile):
            o_ref[:, t, :] = y[t * n_seq:(t + 1) * n_seq, :]


def _s5_kernel(u_ref, x0re_ref, x0im_ref, are_ref, aim_ref, bre_ref, bim_ref, c_ref, d_ref, wglu_ref,
               bglu_ref, o_ref, xre_ref, xim_ref, sre_scr, sim_scr, xre_scr, xim_scr, perm_scr, *, n_seq, t_tile):
    half_w = S5_WIDTH // 2
    half_s = S5_STATE // 2

    @pl.when(pl.program_id(0) == 0)
    def _():
        xre_scr[...] = jnp.concatenate([x0re_ref[:, g, :] for g in range(S5_GROUPS)], axis=-1)
        xim_scr[...] = jnp.concatenate([x0im_ref[:, g, :] for g in range(S5_GROUPS)], axis=-1)

    rows = t_tile * n_seq
    u = _to_time_major(u_ref, perm_scr, n_seq, t_tile)
    ub = u.astype(BF16)
    for hf in range(2):
        uh = ub[:, hf * half_w:(hf + 1) * half_w]
        st = slice(hf * half_s, (hf + 1) * half_s)
        sre_scr[:, :, st] = _dot(uh, bre_ref[hf]).reshape(t_tile, n_seq, half_s)
        sim_scr[:, :, st] = _dot(uh, bim_ref[hf]).reshape(t_tile, n_seq, half_s)

    for lc in range(S5_STATE // S5_LANE_CHUNK):
        lanes = slice(lc * S5_LANE_CHUNK, (lc + 1) * S5_LANE_CHUNK)
        ar = jnp.broadcast_to(are_ref[:, lanes], (SUBLANES, S5_LANE_CHUNK))
        ai = jnp.broadcast_to(aim_ref[:, lanes], (SUBLANES, S5_LANE_CHUNK))
        for sc in range(n_seq // SUBLANES):
            srows = slice(sc * SUBLANES, (sc + 1) * SUBLANES)
            xr = xre_scr[srows, lanes]
            xi = xim_scr[srows, lanes]
            for t in range(t_tile):
                nxr = ar * xr - ai * xi + sre_scr[t, srows, lanes]
                nxi = ar * xi + ai * xr + sim_scr[t, srows, lanes]
                sre_scr[t, srows, lanes] = nxr
                sim_scr[t, srows, lanes] = nxi
                xr, xi = nxr, nxi
            xre_scr[srows, lanes] = xr
            xim_scr[srows, lanes] = xi

    xr_all = sre_scr[...].reshape(rows, S5_STATE).astype(BF16)
    xi_all = sim_scr[...].reshape(rows, S5_STATE).astype(BF16)
    ys = []
    for hf in range(2):
        st = slice(hf * half_s, (hf + 1) * half_s)
        ys.append(_dot(xr_all[:, st], c_ref[hf, 0:half_s, :]) + _dot(xi_all[:, st], c_ref[hf, half_s:, :]))
    y = jnp.concatenate(ys, axis=-1) + d_ref[...] * u
    z = jax.nn.gelu(y)
    gate = jax.nn.sigmoid(_dot(z.astype(BF16), wglu_ref[...]) + bglu_ref[...])
    _from_time_major(z * gate, o_ref, perm_scr, n_seq, t_tile)

    @pl.when(pl.program_id(0) == pl.num_programs(0) - 1)
    def _():
        for g in range(S5_GROUPS):
            xre_ref[:, g, :] = xre_scr[:, g * S5_N:(g + 1) * S5_N]
            xim_ref[:, g, :] = xim_scr[:, g * S5_N:(g + 1) * S5_N]


def _s5_call(u3, x0re, x0im, are, aim, bre, bim, cmat, dskip, wglu, bglu, t_tile):
    n_seq, t, _ = u3.shape
    uspec = pl.BlockSpec((n_seq, t_tile, S5_WIDTH), lambda i: (0, i, 0))
    perm_rows = t_tile * n_seq if n_seq == SUBLANES else SUBLANES
    xspec = pl.BlockSpec((n_seq, S5_GROUPS, S5_N), lambda i: (0, 0, 0))
    consts = [are, aim, bre, bim, cmat, dskip, wglu, bglu]
    return pl.pallas_call(
        functools.partial(_s5_kernel, n_seq=n_seq, t_tile=t_tile),
        grid=(t // t_tile,),
        in_specs=[uspec, xspec, xspec] + [_const_spec(a.shape) for a in consts],
        out_specs=[uspec, xspec, xspec],
        out_shape=[jax.ShapeDtypeStruct((n_seq, t, S5_WIDTH), F32),
                   jax.ShapeDtypeStruct((n_seq, S5_GROUPS, S5_N), F32),
                   jax.ShapeDtypeStruct((n_seq, S5_GROUPS, S5_N), F32)],
        scratch_shapes=[pltpu.VMEM((t_tile, n_seq, S5_STATE), F32), pltpu.VMEM((t_tile, n_seq, S5_STATE), F32),
                        pltpu.VMEM((n_seq, S5_STATE), F32), pltpu.VMEM((n_seq, S5_STATE), F32),
                        pltpu.VMEM((S5_WIDTH // LANES, perm_rows, LANES), F32)],
        compiler_params=_cparams(1),
        name="s5_mixer",
    )(u3, x0re, x0im, *consts)


def _block_diag(blocks):
    g, a, b = blocks.shape
    spread = np.tile(np.eye(b, dtype=np.float32), (1, g))
    keep = (np.arange(g * a)[:, None] // a) == (np.arange(g * b)[None, :] // b)
    return jnp.where(keep, jnp.dot(blocks.reshape(g * a, b), spread), 0.0)


def _halves_block_diag(blocks):
    g = blocks.shape[0]
    return jnp.stack([_block_diag(blocks[:g // 2]), _block_diag(blocks[g // 2:])])


def _trunk_group(x3, s_gla0, x0re, x0im, mem_k3, mem_v3, w):
    s, t, _ = x3.shape
    n = s * t
    tm = ROW_TILE
    long_seq = t >= tm
    if long_seq:
        assert s_gla0 is None and t % tm == 0
        h, us5, o_gla, st_new = _pre_gla_call(x3.reshape(n, D_MODEL), w, tm, t)
    else:
        assert GLA_CHUNK_ROWS % t == 0 and n % (SHORT_GLA_CHUNKS * GLA_CHUNK_ROWS) == 0
        h, qk, v, la, r, us5 = _pre_call(x3.reshape(n, D_MODEL), w['g_ffn1'], w['w_ffn1_gu'], w['w_ffn1_down'],
                                         w['g_mix'], w['w_in_main'], w['w_zg'], w['w_gate'], w['b_gate'], tm)
        rows = SHORT_GLA_CHUNKS * GLA_CHUNK_ROWS
        shp = lambda a: a.reshape(n // rows, rows, a.shape[-1])
        o_gla, st_new = _gla_call(shp(qk), shp(v), shp(la), shp(r), w['g_gla_out'], s_gla0, t, rows, rows // t)
    o_s5, xre, xim = _s5_call(us5.reshape(s, t, S5_WIDTH), x0re, x0im, w['s5_are'], w['s5_aim'], w['s5_bre'],
                              w['s5_bim'], w['s5_c'], w['s5_d'], w['s5_wglu'], w['s5_bglu'], min(t, S5_TILE_STEPS))
    if long_seq:
        y = _tail_call(h.reshape(s, t, D_MODEL), o_gla.reshape(s, t, GLA_WIDTH), o_s5, mem_k3, mem_v3, w, tm)
        return y, st_new, xre, xim
    o_s5 = o_s5.reshape(n, S5_WIDTH)
    h2, q = _mid_call(h, o_gla.reshape(n, GLA_WIDTH), o_s5, w['w_out'], w['g_cross'], w['w_cq'], tm)
    o_att = _attn_call(q.reshape(s, t, D_MODEL), mem_k3, mem_v3, SHORT_ATTN_SEQS)
    y = _post_call(h2, o_att.reshape(n, D_MODEL), w['w_co'], w['g_ffn2'], w['w_ffn2_gu'], w['w_ffn2_down'],
                   w['g_final'], tm)
    return y.reshape(s, t, D_MODEL), st_new, xre, xim


def kernel(x_prompt, x_sample, state_gla, state_s5_re, state_s5_im, cache_mem_k, cache_mem_v, mem_prompt, g_ffn1, w_ffn1_gu, w_ffn1_down, g_mix, w_in, w_gla_gate, b_gla_gate, g_gla_out, s5_lambda_re, s5_lambda_im, s5_log_dt, s5_b_re, s5_b_im, s5_c_re, s5_c_im, s5_d, s5_w_glu, s5_b_glu, w_out, g_cross, g_mem, w_cq, w_ckv, w_co, g_ffn2, w_ffn2_gu, w_ffn2_down, g_final):
    bp, tp, _ = x_prompt.shape
    bs, ts, _ = x_sample.shape
    l = 0
    row = lambda a: a.reshape(1, -1).astype(F32)
    bf = lambda a: a.astype(BF16)

    are, aim, bbre, bbim = _s5_disc_call(
        s5_lambda_re[l], s5_lambda_im[l], s5_log_dt[l].reshape(S5_GROUPS, 1),
        jnp.transpose(s5_b_re[l], (0, 2, 1)), jnp.transpose(s5_b_im[l], (0, 2, 1)))
    c_re_t = jnp.transpose(s5_c_re[l], (0, 2, 1))
    c_im_t = jnp.transpose(s5_c_im[l], (0, 2, 1))
    cre_h = _halves_block_diag(c_re_t)
    cim_h = _halves_block_diag(-c_im_t)
    w_in_l = w_in[l]
    zg_lo = 2 * GLA_QK_WIDTH + GLA_WIDTH
    zg_hi = zg_lo + GLA_GATE_RANK
    w = {
        'g_ffn1': row(g_ffn1[l]), 'w_ffn1_gu': bf(w_ffn1_gu[l]), 'w_ffn1_down': bf(w_ffn1_down[l]),
        'g_mix': row(g_mix[l]),
        'w_in_main': bf(jnp.concatenate([w_in_l[:, :zg_lo], w_in_l[:, zg_hi:]], axis=1)),
        'w_zg': bf(jnp.pad(w_in_l[:, zg_lo:zg_hi], ((0, 0), (0, LANES - GLA_GATE_RANK)))),
        'w_gate': bf(jnp.pad(w_gla_gate[l], ((0, LANES - GLA_GATE_RANK), (0, 0)))),
        'b_gate': row(b_gla_gate[l]),
        'g_gla_out': row(g_gla_out[l]),
        's5_are': are.reshape(1, S5_STATE), 's5_aim': aim.reshape(1, S5_STATE),
        's5_bre': bf(_halves_block_diag(bbre)), 's5_bim': bf(_halves_block_diag(bbim)),
        's5_c': bf(jnp.concatenate([cre_h, cim_h], axis=1)),
        's5_d': row(s5_d[l]), 's5_wglu': bf(_block_diag(s5_w_glu[l])), 's5_bglu': row(s5_b_glu[l]),
        'w_out': bf(w_out[l]), 'g_cross': row(g_cross[l]), 'w_cq': bf(w_cq[l]), 'w_co': bf(w_co[l]),
        'g_ffn2': row(g_ffn2[l]), 'w_ffn2_gu': bf(w_ffn2_gu[l]), 'w_ffn2_down': bf(w_ffn2_down[l]),
        'g_final': row(g_final),
    }

    mk, mv, mk4, mv4 = _memkv_call(mem_prompt.reshape(bp * MEM_LEN, D_MODEL), row(g_mem[l]), bf(w_ckv[l]), ROW_TILE)
    zeros_state = jnp.zeros((bp, S5_GROUPS, S5_N), F32)
    y_p, st_p, re_p, im_p = _trunk_group(
        x_prompt, None, zeros_state, zeros_state, mk.reshape(bp, MEM_LEN, D_MODEL), mv.reshape(bp, MEM_LEN, D_MODEL),
        w)

    s0 = state_gla.reshape(DEPTH * bs, GLA_HEADS, GLA_DK, GLA_DV)[l * bs:(l + 1) * bs]
    y_s, st_s, re_s, im_s = _trunk_group(
        x_sample, s0, state_s5_re.reshape(DEPTH * bs, S5_GROUPS, S5_N)[l * bs:(l + 1) * bs],
        state_s5_im.reshape(DEPTH * bs, S5_GROUPS, S5_N)[l * bs:(l + 1) * bs],
        cache_mem_k.reshape(DEPTH * bs, MEM_LEN, CROSS_HEADS, CROSS_DH)[l * bs:(l + 1) * bs],
        cache_mem_v.reshape(DEPTH * bs, MEM_LEN, CROSS_HEADS, CROSS_DH)[l * bs:(l + 1) * bs],
        w)

    def gla_state_out(st, nb):
        return st[None]

    s5_out = lambda a, nb: a[None]
    kv_out = lambda a: a.reshape(1, bp, MEM_LEN, CROSS_HEADS, CROSS_DH)
    return (y_p, y_s,
            gla_state_out(st_p, bp), s5_out(re_p, bp), s5_out(im_p, bp), kv_out(mk4), kv_out(mv4),
            gla_state_out(st_s, bs), s5_out(re_s, bs), s5_out(im_s, bs))
```

```python
import functools
import math

import numpy as np
import jax
import jax.numpy as jnp
from jax import lax
from jax.experimental import pallas as pl
from jax.experimental.pallas import tpu as pltpu

F32 = jnp.float32
BF16 = jnp.bfloat16

D_MODEL = 1024
DEPTH = 1
EPS = 1e-6
D_FF = 2816
GLA_WIDTH = 512
GLA_HEADS = 4
GLA_DV = 128
GLA_DK = 64
GLA_QK_WIDTH = 256
GLA_GATE_RANK = 16
GLA_TAU = 16.0
S5_WIDTH = 512
S5_P = 16
S5_GROUPS = 32
S5_N = 64
S5_STATE = S5_GROUPS * S5_N
CROSS_HEADS = 4
CROSS_DH = 256
MEM_LEN = 256

LANES = 128
SUBLANES = 8
MXU_DIM = 256
VMEM_LIMIT_BYTES = 56 * 1024 * 1024

ROW_TILE = 512
S5_TILE_STEPS = 128
SHORT_GLA_CHUNKS = 4
SHORT_ATTN_SEQS = 4
FF_CHUNK = MXU_DIM
GLA_CHUNK_ROWS = 64
S5_LANE_CHUNK = 512


def _cparams(n_grid_axes):
    return pltpu.CompilerParams(
        dimension_semantics=("arbitrary",) * n_grid_axes,
        vmem_limit_bytes=VMEM_LIMIT_BYTES,
    )


def _const_spec(shape):
    zeros = (0,) * len(shape)
    return pl.BlockSpec(shape, lambda *_: zeros, pipeline_mode=pl.Buffered(1))


def _dot(a, b):
    return jnp.dot(a, b, preferred_element_type=F32)


def _dot_nt(a, b):
    return lax.dot_general(a, b, (((1,), (1,)), ((), ())), preferred_element_type=F32)


def _dot_tn(a, b):
    return lax.dot_general(a, b, (((0,), (0,)), ((), ())), preferred_element_type=F32)


def _rms(x, g):
    return x * lax.rsqrt(jnp.mean(x * x, axis=-1, keepdims=True) + EPS) * g


def _swiglu(xn_bf, wgu_ref, wdn_ref, between=None):
    n_chunks = D_FF // FF_CHUNK

    def gate_up(c):
        lo, hi = c * FF_CHUNK, (c + 1) * FF_CHUNK
        return _dot(xn_bf, wgu_ref[:, lo:hi]), _dot(xn_bf, wgu_ref[:, D_FF + lo:D_FF + hi])

    acc = None
    nxt = gate_up(0)
    for c in range(n_chunks):
        g, u = nxt
        if c + 1 < n_chunks:
            nxt = gate_up(c + 1)
        a = (jax.nn.silu(g) * u).astype(BF16)
        d = _dot(a, wdn_ref[c * FF_CHUNK:(c + 1) * FF_CHUNK, :])
        acc = d if acc is None else acc + d
        if between is not None:
            between()
    return acc


def _log_sigmoid(z):
    return jnp.minimum(z, 0.0) - jnp.log1p(jnp.exp(-jnp.abs(z)))


def _pre_kernel(x_ref, g1_ref, wgu_ref, wdn_ref, gm_ref, win_ref, wzg_ref, wgate_ref, bgate_ref,
                h_ref, qk_ref, v_ref, la_ref, r_ref, us5_ref):
    x = x_ref[...]
    xn = _rms(x, g1_ref[...]).astype(BF16)
    h = x + 0.5 * _swiglu(xn, wgu_ref, wdn_ref)
    h_ref[...] = h
    un = _rms(h, gm_ref[...]).astype(BF16)
    p = _dot(un, win_ref[...])
    qk_ref[...] = p[:, 0:512]
    v_ref[...] = p[:, 512:1024]
    r_ref[...] = p[:, 1024:1536]
    us5_ref[...] = p[:, 1536:2048]
    zg = _dot(un, wzg_ref[...])
    z = _dot(zg.astype(BF16), wgate_ref[...]) + bgate_ref[...]
    la_ref[...] = _log_sigmoid(z) * (1.0 / GLA_TAU)


def _pre_call(x2d, g1, wgu, wdn, gm, win, wzg, wgate, bgate, tm):
    n = x2d.shape[0]
    row = lambda w: pl.BlockSpec((tm, w), lambda i: (i, 0))
    out_shapes = [jax.ShapeDtypeStruct((n, w), F32) for w in (D_MODEL, 512, 512, 256, 512, 512)]
    return pl.pallas_call(
        _pre_kernel,
        grid=(n // tm,),
        in_specs=[row(D_MODEL), _const_spec(g1.shape), _const_spec(wgu.shape), _const_spec(wdn.shape),
                  _const_spec(gm.shape), _const_spec(win.shape), _const_spec(wzg.shape),
                  _const_spec(wgate.shape), _const_spec(bgate.shape)],
        out_specs=[row(D_MODEL), row(512), row(512), row(256), row(512), row(512)],
        out_shape=out_shapes,
        compiler_params=_cparams(1),
        name="pre_ffn_proj",
    )(x2d, g1, wgu, wdn, gm, win, wzg, wgate, bgate)


def _memkv_kernel(m_ref, g_ref, w_ref, k_ref, v_ref, k4_ref, v4_ref):
    mn = _rms(m_ref[...], g_ref[...]).astype(BF16)
    kv = _dot(mn, w_ref[...])
    k_ref[...] = kv[:, :D_MODEL]
    v_ref[...] = kv[:, D_MODEL:]
    for hd in range(CROSS_HEADS):
        k4_ref[:, hd, :] = kv[:, hd * CROSS_DH:(hd + 1) * CROSS_DH]
        v4_ref[:, hd, :] = kv[:, D_MODEL + hd * CROSS_DH:D_MODEL + (hd + 1) * CROSS_DH]


def _memkv_call(mem2d, g, w, tm):
    n = mem2d.shape[0]
    row = pl.BlockSpec((tm, D_MODEL), lambda i: (i, 0))
    row4 = pl.BlockSpec((tm, CROSS_HEADS, CROSS_DH), lambda i: (i, 0, 0))
    return pl.pallas_call(
        _memkv_kernel,
        grid=(n // tm,),
        in_specs=[row, _const_spec(g.shape), _const_spec(w.shape)],
        out_specs=[row, row, row4, row4],
        out_shape=[jax.ShapeDtypeStruct((n, D_MODEL), F32)] * 2
        + [jax.ShapeDtypeStruct((n, CROSS_HEADS, CROSS_DH), F32)] * 2,
        compiler_params=_cparams(1),
        name="memory_kv",
    )(mem2d, g, w)


def _mid_kernel(h_ref, og_ref, os_ref, wo_ref, gc_ref, wq_ref, h2_ref, q_ref):
    h2 = (h_ref[...] + _dot(og_ref[...].astype(BF16), wo_ref[0:GLA_WIDTH, :])
          + _dot(os_ref[...].astype(BF16), wo_ref[GLA_WIDTH:, :]))
    h2_ref[...] = h2
    hn = _rms(h2, gc_ref[...]).astype(BF16)
    q_ref[...] = _dot(hn, wq_ref[...])


def _mid_call(h, og, os_, wo, gc, wq, tm):
    n = h.shape[0]
    row = lambda w: pl.BlockSpec((tm, w), lambda i: (i, 0))
    return pl.pallas_call(
        _mid_kernel,
        grid=(n // tm,),
        in_specs=[row(D_MODEL), row(512), row(512), _const_spec(wo.shape), _const_spec(gc.shape),
                  _const_spec(wq.shape)],
        out_specs=[row(D_MODEL), row(D_MODEL)],
        out_shape=[jax.ShapeDtypeStruct((n, D_MODEL), F32)] * 2,
        compiler_params=_cparams(1),
        name="mix_out_cross_q",
    )(h, og, os_, wo, gc, wq)


def _post_kernel(h2_ref, o_ref, wco_ref, g2_ref, wgu_ref, wdn_ref, gf_ref, y_ref):
    h3 = h2_ref[...] + _dot(o_ref[...].astype(BF16), wco_ref[...])
    hn = _rms(h3, g2_ref[...]).astype(BF16)
    h4 = h3 + 0.5 * _swiglu(hn, wgu_ref, wdn_ref)
    y_ref[...] = _rms(h4, gf_ref[...])


def _post_call(h2, o, wco, g2, wgu, wdn, gf, tm):
    n = h2.shape[0]
    row = pl.BlockSpec((tm, D_MODEL), lambda i: (i, 0))
    return pl.pallas_call(
        _post_kernel,
        grid=(n // tm,),
        in_specs=[row, row, _const_spec(wco.shape), _const_spec(g2.shape), _const_spec(wgu.shape),
                  _const_spec(wdn.shape), _const_spec(gf.shape)],
        out_specs=row,
        out_shape=jax.ShapeDtypeStruct((n, D_MODEL), F32),
        compiler_params=_cparams(1),
        name="post_ffn_final",
    )(h2, o, wco, g2, wgu, wdn, gf)


def _tail_kernel(h_ref, og_ref, os_ref, k_ref, v_ref, wo_ref, gc_ref, wq_ref, wco_ref, g2_ref, wgu_ref,
                 wdn_ref, gf_ref, y_ref):
    h2 = (h_ref[0] + _dot(og_ref[0].astype(BF16), wo_ref[0:GLA_WIDTH, :])
          + _dot(os_ref[0].astype(BF16), wo_ref[GLA_WIDTH:, :]))
    q = _dot(_rms(h2, gc_ref[...]).astype(BF16), wq_ref[...]).astype(BF16)
    heads = [slice(hd * CROSS_DH, (hd + 1) * CROSS_DH) for hd in range(CROSS_HEADS)]
    scores = [_dot_nt(q[:, sl], k_ref[0, :, sl].astype(BF16)) * (CROSS_DH ** -0.5) for sl in heads]
    probs = [_softmax(s).astype(BF16) for s in scores]
    outs = [_dot(p, v_ref[0, :, sl].astype(BF16)).astype(BF16) for p, sl in zip(probs, heads)]
    h3 = h2 + _dot(jnp.concatenate(outs, axis=-1), wco_ref[...])
    hn = _rms(h3, g2_ref[...]).astype(BF16)
    h4 = h3 + 0.5 * _swiglu(hn, wgu_ref, wdn_ref)
    y_ref[0] = _rms(h4, gf_ref[...])


def _tail_call(h3d, og3, os3, k3, v3, w, tm):
    s, t, _ = h3d.shape
    row = lambda wd: pl.BlockSpec((1, tm, wd), lambda i, j: (i, j, 0))
    kvspec = pl.BlockSpec((1, MEM_LEN, D_MODEL), lambda i, j: (i, 0, 0))
    consts = [w['w_out'], w['g_cross'], w['w_cq'], w['w_co'], w['g_ffn2'], w['w_ffn2_gu'], w['w_ffn2_down'],
              w['g_final']]
    return pl.pallas_call(
        _tail_kernel,
        grid=(s, t // tm),
        in_specs=[row(D_MODEL), row(GLA_WIDTH), row(S5_WIDTH), kvspec, kvspec] + [_const_spec(a.shape) for a in consts],
        out_specs=row(D_MODEL),
        out_shape=jax.ShapeDtypeStruct((s, t, D_MODEL), F32),
        compiler_params=_cparams(2),
        name="tail_attn_ffn",
    )(h3d, og3, os3, k3, v3, *consts)


def _softmax(s):
    e = jnp.exp(s - jnp.max(s, axis=-1, keepdims=True))
    return e / jnp.sum(e, axis=-1, keepdims=True)


def _load_mem(ref, j):
    heads_major = jnp.transpose(ref[j], (1, 0, 2))
    return jnp.concatenate([heads_major[hd] for hd in range(CROSS_HEADS)], axis=-1).astype(BF16)


def _attn_packed_kernel(q_ref, k_ref, v_ref, o_ref, *, n_seq):
    t = q_ref.shape[1]
    shape = (CROSS_HEADS * t, D_MODEL)
    q_mask = (_iota2(shape, 0) // t) == (_iota2(shape, 1) // CROSS_DH)
    o_lane_head = _iota2((t, D_MODEL), 1) // CROSS_DH
    scores = []
    for j in range(n_seq):
        q_stack = jnp.where(q_mask, jnp.concatenate([q_ref[j]] * CROSS_HEADS, axis=0), 0.0).astype(BF16)
        scores.append(_dot_nt(q_stack, _load_mem(k_ref, j)) * (CROSS_DH ** -0.5))
    probs = [_softmax(s).astype(BF16) for s in scores]
    for j in range(n_seq):
        full = _dot(probs[j], _load_mem(v_ref, j))
        o = jnp.zeros((t, D_MODEL), F32)
        for hd in range(CROSS_HEADS):
            o = jnp.where(o_lane_head == hd, full[hd * t:(hd + 1) * t, :], o)
        o_ref[j] = o.astype(o_ref.dtype)


def _attn_call(q3, k3, v3, n_seq):
    s, t, _ = q3.shape
    assert t * CROSS_HEADS <= MXU_DIM and s % n_seq == 0
    qspec = pl.BlockSpec((n_seq, t, D_MODEL), lambda i: (i, 0, 0))
    kvspec = pl.BlockSpec((n_seq, MEM_LEN, CROSS_HEADS, CROSS_DH), lambda i: (i, 0, 0, 0))
    return pl.pallas_call(
        functools.partial(_attn_packed_kernel, n_seq=n_seq),
        grid=(s // n_seq,),
        in_specs=[qspec, kvspec, kvspec],
        out_specs=qspec,
        out_shape=jax.ShapeDtypeStruct(q3.shape, F32),
        compiler_params=_cparams(1),
        name="cross_attention",
    )(q3, k3, v3)


def _split3(x):
    hi = x.astype(BF16)
    r1 = x - hi.astype(F32)
    mid = r1.astype(BF16)
    lo = (r1 - mid.astype(F32)).astype(BF16)
    return hi, mid, lo


def _dot01(m01, x):
    hi, mid, lo = _split3(x)
    return _dot(m01, hi) + _dot(m01, mid) + _dot(m01, lo)


def _iota2(shape, dim):
    return lax.broadcasted_iota(jnp.int32, shape, dim)


def _gla_stages(qk_ref, v_ref, la_ref, r_ref, o_ref, g_ref, tri_ref, lvl_ref, khm_ref, vhm_ref, shm_ref,
                states, put_states, *, seq_len, n_chunks):
    c_rows = GLA_CHUNK_ROWS
    n_sub = c_rows // seq_len
    log2 = lambda n: int(math.log2(n))
    tri = tri_ref[...]
    lvl = lvl_ref[...]
    rowq = _iota2((c_rows, GLA_QK_WIDTH), 0)
    u_head = [(_iota2((GLA_DV, GLA_QK_WIDTH), 1) >> log2(GLA_DK)) == hd for hd in range(GLA_HEADS)]

    def head_blocks(x, mask_ref):
        return jnp.concatenate([x.astype(BF16)] * GLA_HEADS, axis=0) * mask_ref[...]

    g_out = g_ref[...]
    chunks = range(n_chunks)
    rows_of = lambda c: slice(c * c_rows, (c + 1) * c_rows)
    sub_of = lambda j: slice(j * seq_len, (j + 1) * seq_len)
    q = [qk_ref[rows_of(c), 0:GLA_QK_WIDTH] * (GLA_DK ** -0.5) for c in chunks]
    k = [qk_ref[rows_of(c), GLA_QK_WIDTH:2 * GLA_QK_WIDTH] for c in chunks]
    v = [v_ref[rows_of(c), :] for c in chunks]
    b = [_dot01(tri, la_ref[rows_of(c), :]) for c in chunks]
    yield

    a_mat = [jnp.where(lvl == 0, _dot_nt(q[c].astype(BF16), head_blocks(k[c], khm_ref)), 0.0) for c in chunks]
    last = list(b)
    m = 1
    while m < seq_len:
        yield
        even = ((rowq >> log2(m)) & 1) == 0
        for c in chunks:
            prev_last = pltpu.roll(last[c], m, axis=0)
            e = jnp.exp(jnp.minimum(jnp.where(even, last[c] - b[c], b[c] - prev_last), 0.0))
            a_lvl = _dot_nt((q[c] * e).astype(BF16), head_blocks(k[c] * e, khm_ref))
            a_mat[c] = jnp.where(lvl == log2(m) + 1, a_lvl, a_mat[c])
            last[c] = jnp.where(even, pltpu.roll(last[c], c_rows - m, axis=0), last[c])
        m *= 2
    yield
    o_intra = [_dot(a_mat[c].astype(BF16), head_blocks(v[c], vhm_ref)) for c in chunks]
    yield

    narrow = (lambda a: a.astype(BF16)) if seq_len % 16 == 0 else (lambda a: a)
    q_dec = [narrow(q[c] * jnp.exp(b[c])) for c in chunks]
    k_dec = [narrow(k[c] * jnp.exp(jnp.minimum(last[c] - b[c], 0.0))) for c in chunks]
    u_t = {}
    for c in chunks:
        vc = narrow(v[c])
        for j in range(n_sub):
            u_full = _dot_tn(vc[sub_of(j)], k_dec[c][sub_of(j)])
            acc = jnp.zeros((GLA_DV, GLA_QK_WIDTH), F32)
            for hd in range(GLA_HEADS):
                acc = jnp.where(u_head[hd], u_full[hd * GLA_DV:(hd + 1) * GLA_DV, :], acc)
            u_t[c, j] = acc

    states = list(states)
    state_of = (lambda c, j: c * n_sub + j) if n_sub > 1 else (lambda c, j: 0)
    for c in chunks:
        if c % 2 == 0:
            yield
        o_parts = []
        for j in range(n_sub):
            sj = state_of(c, j)
            o_parts.append(_dot_nt(q_dec[c][sub_of(j)], head_blocks(states[sj], shm_ref)))
            decay = jnp.exp(last[c][j * seq_len:j * seq_len + 1, :])
            states[sj] = states[sj] * decay + u_t[c, j]
        o_inter = o_parts[0] if n_sub == 1 else jnp.concatenate(o_parts, axis=0)
        o = o_intra[c] + o_inter
        r = r_ref[rows_of(c), :]
        outs = []
        for hd in range(GLA_HEADS):
            sl = slice(hd * GLA_DV, (hd + 1) * GLA_DV)
            outs.append(_rms(o[:, sl], g_out) * jax.nn.silu(r[:, sl]))
        o_ref[rows_of(c), :] = jnp.concatenate(outs, axis=-1).astype(o_ref.dtype)
    put_states(states)


def _gla_kernel(*refs, seq_len, n_chunks, n_state, has_s0):
    ins, rest = refs[:10], refs[10:]
    qk_ref, v_ref, la_ref, r_ref, g_ref = ins[:5]
    if has_s0:
        s0_ref, o_ref, st_ref, s_scr = rest
    else:
        o_ref, st_ref, s_scr = rest
        s0_ref = None
    t_idx = pl.program_id(1)

    @pl.when(t_idx == 0)
    def _():
        if has_s0:
            for j in range(n_state):
                s_scr[j] = s0_ref[j].reshape(GLA_QK_WIDTH, GLA_DV).T
        else:
            s_scr[...] = jnp.zeros_like(s_scr)

    def put_states(states):
        for j, s_t in enumerate(states):
            s_scr[j] = s_t

    for _ in _gla_stages(qk_ref.at[0], v_ref.at[0], la_ref.at[0], r_ref.at[0], o_ref.at[0], g_ref, *ins[5:],
                         [s_scr[j] for j in range(n_state)], put_states, seq_len=seq_len, n_chunks=n_chunks):
        pass

    @pl.when(t_idx == pl.num_programs(1) - 1)
    def _():
        for j in range(n_state):
            st_ref[j] = s_scr[j].T.reshape(GLA_HEADS, GLA_DK, GLA_DV)


def _gla_index_constants(seq_len):
    c = GLA_CHUNK_ROWS
    i = np.arange(c)[:, None]
    s = np.arange(c)[None, :]
    same_seq = (i // seq_len) == (s // seq_len)
    tri = (same_seq & (s <= i)).astype(np.float32)
    top_bit = np.floor(np.log2(np.maximum(i ^ s, 1))).astype(np.int32)
    lvl = np.where(same_seq & (s < i), top_bit + 1, -1)
    lvl = np.where(i == s, 0, lvl).astype(np.int32)
    head_mask = lambda rows_per_head, lanes_per_head: (
        (np.arange(GLA_HEADS * rows_per_head)[:, None] // rows_per_head)
        == (np.arange(GLA_HEADS * lanes_per_head)[None, :] // lanes_per_head)).astype(np.float32)
    return [jnp.asarray(tri, BF16), jnp.asarray(np.tile(lvl, (1, GLA_HEADS))),
            jnp.asarray(head_mask(c, GLA_DK), BF16), jnp.asarray(head_mask(c, GLA_DV), BF16),
            jnp.asarray(head_mask(GLA_DV, GLA_DK), BF16)]


def _gla_call(qk3, v3, la3, r3, g_out, s0, seq_len, tile_rows, n_state):
    n_outer, rows, _ = qk3.shape
    n_chunks = tile_rows // GLA_CHUNK_ROWS
    spec = lambda w: pl.BlockSpec((1, tile_rows, w), lambda i, t: (i, t, 0))
    st_spec = pl.BlockSpec((n_state, GLA_HEADS, GLA_DK, GLA_DV), lambda i, t: (i, 0, 0, 0))
    has_s0 = s0 is not None
    consts = [g_out] + _gla_index_constants(seq_len)
    in_specs = [spec(512), spec(512), spec(256), spec(512)] + [_const_spec(a.shape) for a in consts]
    args = [qk3, v3, la3, r3] + consts
    if has_s0:
        in_specs.append(st_spec)
        args.append(s0)
    return pl.pallas_call(
        functools.partial(_gla_kernel, seq_len=seq_len, n_chunks=n_chunks, n_state=n_state, has_s0=has_s0),
        grid=(n_outer, rows // tile_rows),
        in_specs=in_specs,
        out_specs=[spec(512), st_spec],
        out_shape=[jax.ShapeDtypeStruct((n_outer, rows, GLA_WIDTH), BF16),
                   jax.ShapeDtypeStruct((n_outer * n_state, GLA_HEADS, GLA_DK, GLA_DV), F32)],
        scratch_shapes=[pltpu.VMEM((n_state, GLA_DV, GLA_QK_WIDTH), F32)],
        compiler_params=_cparams(2),
        name="gla_mixer",
    )(*args)


def _pre_gla_kernel(x_ref, g1_ref, wgu_ref, wdn_ref, gm_ref, win_ref, wzg_ref, wgate_ref, bgate_ref, gout_ref,
                    tri_ref, lvl_ref, khm_ref, vhm_ref, shm_ref,
                    h_ref, us5_ref, og_ref, st_ref,
                    qk_scr, v_scr, la_scr, r_scr, s_scr, *, tiles_per_seq):
    i = pl.program_id(0)
    last_step = pl.num_programs(0) - 1
    n_chunks = x_ref.shape[0] // GLA_CHUNK_ROWS

    @pl.when(i == 0)
    def _():
        for scr in (qk_scr, v_scr, la_scr, r_scr, s_scr):
            scr[...] = jnp.zeros_like(scr)

    def gla_stages():
        first_of_seq = lax.rem(i + (tiles_per_seq - 1), tiles_per_seq) == 0
        state = jnp.where(first_of_seq, 0.0, s_scr[0])

        def put_states(states):
            s_scr[0] = states[0]
            st_ref[0] = states[0].T.reshape(GLA_HEADS, GLA_DK, GLA_DV)

        return _gla_stages(qk_scr, v_scr, la_scr, r_scr, og_ref, gout_ref, tri_ref, lvl_ref, khm_ref, vhm_ref,
                           shm_ref, [state], put_states, seq_len=GLA_CHUNK_ROWS, n_chunks=n_chunks)

    @pl.when(i < last_step)
    def _():
        stages = gla_stages()
        next(stages)
        x = x_ref[...]
        xn = _rms(x, g1_ref[...]).astype(BF16)
        h = x + 0.5 * _swiglu(xn, wgu_ref, wdn_ref, between=lambda: next(stages, None))
        h_ref[...] = h
        un = _rms(h, gm_ref[...]).astype(BF16)
        p = []
        for grp in range(4):
            p.append(_dot(un, win_ref[:, grp * 512:(grp + 1) * 512]))
            next(stages, None)
        zg = _dot(un, wzg_ref[...])
        z = _dot(zg.astype(BF16), wgate_ref[...]) + bgate_ref[...]
        for _ in stages:
            pass
        qk_scr[...] = p[0]
        v_scr[...] = p[1]
        r_scr[...] = p[2]
        us5_ref[...] = p[3]
        la_scr[...] = _log_sigmoid(z) * (1.0 / GLA_TAU)

    @pl.when(i == last_step)
    def _():
        for _ in gla_stages():
            pass


def _pre_gla_call(x2d, w, tm, seq_rows):
    n = x2d.shape[0]
    n_tiles = n // tm
    tiles_per_seq = seq_rows // tm
    cur = lambda wd: pl.BlockSpec((tm, wd), lambda i: (jnp.minimum(i, n_tiles - 1), 0))
    prev = lambda wd: pl.BlockSpec((tm, wd), lambda i: (jnp.maximum(i - 1, 0), 0))
    st_spec = pl.BlockSpec((1, GLA_HEADS, GLA_DK, GLA_DV),
                           lambda i: (jnp.maximum(i - 1, 0) // tiles_per_seq, 0, 0, 0))
    consts = [w['g_ffn1'], w['w_ffn1_gu'], w['w_ffn1_down'], w['g_mix'], w['w_in_main'], w['w_zg'], w['w_gate'],
              w['b_gate'], w['g_gla_out']] + _gla_index_constants(GLA_CHUNK_ROWS)
    return pl.pallas_call(
        functools.partial(_pre_gla_kernel, tiles_per_seq=tiles_per_seq),
        grid=(n_tiles + 1,),
        in_specs=[cur(D_MODEL)] + [_const_spec(a.shape) for a in consts],
        out_specs=[cur(D_MODEL), cur(S5_WIDTH), prev(GLA_WIDTH), st_spec],
        out_shape=[jax.ShapeDtypeStruct((n, D_MODEL), F32), jax.ShapeDtypeStruct((n, S5_WIDTH), F32),
                   jax.ShapeDtypeStruct((n, GLA_WIDTH), BF16),
                   jax.ShapeDtypeStruct((n // seq_rows, GLA_HEADS, GLA_DK, GLA_DV), F32)],
        scratch_shapes=[pltpu.VMEM((tm, 2 * GLA_QK_WIDTH), F32), pltpu.VMEM((tm, GLA_WIDTH), F32),
                        pltpu.VMEM((tm, GLA_QK_WIDTH), F32), pltpu.VMEM((tm, GLA_WIDTH), F32),
                        pltpu.VMEM((1, GLA_DV, GLA_QK_WIDTH), F32)],
        compiler_params=_cparams(1),
        name="pre_ffn_gla",
    )(x2d, *consts)


def _s5_disc_kernel(lre_ref, lim_ref, ldt_ref, bre_ref, bim_ref, are_ref, aim_ref, bbre_ref, bbim_ref):
    lr = lre_ref[...]
    li = lim_ref[...]
    dt = jnp.exp(ldt_ref[...])
    mag = jnp.exp(dt * lr)
    ab_re = mag * jnp.cos(dt * li)
    ab_im = mag * jnp.sin(dt * li)
    den = lr * lr + li * li
    coef_re = ((ab_re - 1.0) * lr + ab_im * li) / den
    coef_im = (ab_im * lr - (ab_re - 1.0) * li) / den
    are_ref[...] = ab_re
    aim_ref[...] = ab_im
    br = bre_ref[...]
    bi = bim_ref[...]
    cr = coef_re[:, None, :]
    cim = coef_im[:, None, :]
    bbre_ref[...] = cr * br - cim * bi
    bbim_ref[...] = cr * bi + cim * br


def _s5_disc_call(lre, lim, ldt, bre_t, bim_t):
    gn = jax.ShapeDtypeStruct((S5_GROUPS, S5_N), F32)
    gpn = jax.ShapeDtypeStruct((S5_GROUPS, S5_P, S5_N), F32)
    return pl.pallas_call(
        _s5_disc_kernel,
        out_shape=[gn, gn, gpn, gpn],
        name="s5_discretise",
    )(lre, lim, ldt, bre_t, bim_t)


def _to_time_major(u_ref, perm_scr, n_seq, t_tile):
    if n_seq == SUBLANES:
        n_slab = S5_WIDTH // LANES
        for j in range(n_seq):
            for sb in range(n_slab):
                perm_scr[sb, pl.ds(j, t_tile, stride=n_seq), :] = u_ref[j, :, sb * LANES:(sb + 1) * LANES]
        return jnp.concatenate([perm_scr[sb] for sb in range(n_slab)], axis=-1)
    return jnp.concatenate([u_ref[:, t, :] for t in range(t_tile)], axis=0)


def _from_time_major(y, o_ref, perm_scr, n_seq, t_tile):
    if n_seq == SUBLANES:
        n_slab = S5_WIDTH // LANES
        for sb in range(n_slab):
            perm_scr[sb] = y[:, sb * LANES:(sb + 1) * LANES]
        for j in range(n_seq):
            for sb in range(n_slab):
                o_ref[j, :, sb * LANES:(sb + 1) * LANES] = perm_scr[sb, pl.ds(j, t_tile, stride=n_seq), :]
    else:
        for t in range(t_tile):
            o_ref[:, t, :] = y[t * n_seq:(t + 1) * n_seq, :]


def _s5_kernel(u_ref, x0re_ref, x0im_ref, are_ref, aim_ref, bre_ref, bim_ref, c_ref, d_ref, wglu_ref,
               bglu_ref, o_ref, xre_ref, xim_ref, sre_scr, sim_scr, xre_scr, xim_scr, perm_scr, *, n_seq, t_tile):
    half_w = S5_WIDTH // 2
    half_s = S5_STATE // 2

    @pl.when(pl.program_id(0) == 0)
    def _():
        xre_scr[...] = jnp.concatenate([x0re_ref[:, g, :] for g in range(S5_GROUPS)], axis=-1)
        xim_scr[...] = jnp.concatenate([x0im_ref[:, g, :] for g in range(S5_GROUPS)], axis=-1)

    rows = t_tile * n_seq
    u = _to_time_major(u_ref, perm_scr, n_seq, t_tile)
    ub = u.astype(BF16)
    for hf in range(2):
        uh = ub[:, hf * half_w:(hf + 1) * half_w]
        st = slice(hf * half_s, (hf + 1) * half_s)
        sre_scr[:, :, st] = _dot(uh, bre_ref[hf]).reshape(t_tile, n_seq, half_s)
        sim_scr[:, :, st] = _dot(uh, bim_ref[hf]).reshape(t_tile, n_seq, half_s)

    for lc in range(S5_STATE // S5_LANE_CHUNK):
        lanes = slice(lc * S5_LANE_CHUNK, (lc + 1) * S5_LANE_CHUNK)
        ar = jnp.broadcast_to(are_ref[:, lanes], (SUBLANES, S5_LANE_CHUNK))
        ai = jnp.broadcast_to(aim_ref[:, lanes], (SUBLANES, S5_LANE_CHUNK))
        for sc in range(n_seq // SUBLANES):
            srows = slice(sc * SUBLANES, (sc + 1) * SUBLANES)
            xr = xre_scr[srows, lanes]
            xi = xim_scr[srows, lanes]
            for t in range(t_tile):
                nxr = ar * xr - ai * xi + sre_scr[t, srows, lanes]
                nxi = ar * xi + ai * xr + sim_scr[t, srows, lanes]
                sre_scr[t, srows, lanes] = nxr
                sim_scr[t, srows, lanes] = nxi
                xr, xi = nxr, nxi
            xre_scr[srows, lanes] = xr
            xim_scr[srows, lanes] = xi

    xr_all = sre_scr[...].reshape(rows, S5_STATE).astype(BF16)
    xi_all = sim_scr[...].reshape(rows, S5_STATE).astype(BF16)
    ys = []
    for hf in range(2):
        st = slice(hf * half_s, (hf + 1) * half_s)
        ys.append(_dot(xr_all[:, st], c_ref[hf, 0:half_s, :]) + _dot(xi_all[:, st], c_ref[hf, half_s:, :]))
    y = jnp.concatenate(ys, axis=-1) + d_ref[...] * u
    z = jax.nn.gelu(y)
    gate = jax.nn.sigmoid(_dot(z.astype(BF16), wglu_ref[...]) + bglu_ref[...])
    _from_time_major(z * gate, o_ref, perm_scr, n_seq, t_tile)

    @pl.when(pl.program_id(0) == pl.num_programs(0) - 1)
    def _():
        for g in range(S5_GROUPS):
            xre_ref[:, g, :] = xre_scr[:, g * S5_N:(g + 1) * S5_N]
            xim_ref[:, g, :] = xim_scr[:, g * S5_N:(g + 1) * S5_N]


def _s5_call(u3, x0re, x0im, are, aim, bre, bim, cmat, dskip, wglu, bglu, t_tile):
    n_seq, t, _ = u3.shape
    uspec = pl.BlockSpec((n_seq, t_tile, S5_WIDTH), lambda i: (0, i, 0))
    perm_rows = t_tile * n_seq if n_seq == SUBLANES else SUBLANES
    xspec = pl.BlockSpec((n_seq, S5_GROUPS, S5_N), lambda i: (0, 0, 0))
    consts = [are, aim, bre, bim, cmat, dskip, wglu, bglu]
    return pl.pallas_call(
        functools.partial(_s5_kernel, n_seq=n_seq, t_tile=t_tile),
        grid=(t // t_tile,),
        in_specs=[uspec, xspec, xspec] + [_const_spec(a.shape) for a in consts],
        out_specs=[uspec, xspec, xspec],
        out_shape=[jax.ShapeDtypeStruct((n_seq, t, S5_WIDTH), F32),
                   jax.ShapeDtypeStruct((n_seq, S5_GROUPS, S5_N), F32),
                   jax.ShapeDtypeStruct((n_seq, S5_GROUPS, S5_N), F32)],
        scratch_shapes=[pltpu.VMEM((t_tile, n_seq, S5_STATE), F32), pltpu.VMEM((t_tile, n_seq, S5_STATE), F32),
                        pltpu.VMEM((n_seq, S5_STATE), F32), pltpu.VMEM((n_seq, S5_STATE), F32),
                        pltpu.VMEM((S5_WIDTH // LANES, perm_rows, LANES), F32)],
        compiler_params=_cparams(1),
        name="s5_mixer",
    )(u3, x0re, x0im, *consts)


def _block_diag(blocks):
    g, a, b = blocks.shape
    spread = np.tile(np.eye(b, dtype=np.float32), (1, g))
    keep = (np.arange(g * a)[:, None] // a) == (np.arange(g * b)[None, :] // b)
    return jnp.where(keep, jnp.dot(blocks.reshape(g * a, b), spread), 0.0)


def _halves_block_diag(blocks):
    g = blocks.shape[0]
    return jnp.stack([_block_diag(blocks[:g // 2]), _block_diag(blocks[g // 2:])])


def _trunk_group(x3, s_gla0, x0re, x0im, mem_k3, mem_v3, w):
    s, t, _ = x3.shape
    n = s * t
    tm = ROW_TILE
    long_seq = t >= tm
    if long_seq:
        assert s_gla0 is None and t % tm == 0
        h, us5, o_gla, st_new = _pre_gla_call(x3.reshape(n, D_MODEL), w, tm, t)
    else:
        assert GLA_CHUNK_ROWS % t == 0 and n % (SHORT_GLA_CHUNKS * GLA_CHUNK_ROWS) == 0
        h, qk, v, la, r, us5 = _pre_call(x3.reshape(n, D_MODEL), w['g_ffn1'], w['w_ffn1_gu'], w['w_ffn1_down'],
                                         w['g_mix'], w['w_in_main'], w['w_zg'], w['w_gate'], w['b_gate'], tm)
        rows = SHORT_GLA_CHUNKS * GLA_CHUNK_ROWS
        shp = lambda a: a.reshape(n // rows, rows, a.shape[-1])
        o_gla, st_new = _gla_call(shp(qk), shp(v), shp(la), shp(r), w['g_gla_out'], s_gla0, t, rows, rows // t)
    o_s5, xre, xim = _s5_call(us5.reshape(s, t, S5_WIDTH), x0re, x0im, w['s5_are'], w['s5_aim'], w['s5_bre'],
                              w['s5_bim'], w['s5_c'], w['s5_d'], w['s5_wglu'], w['s5_bglu'], min(t, S5_TILE_STEPS))
    if long_seq:
        y = _tail_call(h.reshape(s, t, D_MODEL), o_gla.reshape(s, t, GLA_WIDTH), o_s5, mem_k3, mem_v3, w, tm)
        return y, st_new, xre, xim
    o_s5 = o_s5.reshape(n, S5_WIDTH)
    h2, q = _mid_call(h, o_gla.reshape(n, GLA_WIDTH), o_s5, w['w_out'], w['g_cross'], w['w_cq'], tm)
    o_att = _attn_call(q.reshape(s, t, D_MODEL), mem_k3, mem_v3, SHORT_ATTN_SEQS)
    y = _post_call(h2, o_att.reshape(n, D_MODEL), w['w_co'], w['g_ffn2'], w['w_ffn2_gu'], w['w_ffn2_down'],
                   w['g_final'], tm)
    return y.reshape(s, t, D_MODEL), st_new, xre, xim


def kernel(x_prompt, x_sample, state_gla, state_s5_re, state_s5_im, cache_mem_k, cache_mem_v, mem_prompt, g_ffn1, w_ffn1_gu, w_ffn1_down, g_mix, w_in, w_gla_gate, b_gla_gate, g_gla_out, s5_lambda_re, s5_lambda_im, s5_log_dt, s5_b_re, s5_b_im, s5_c_re, s5_c_im, s5_d, s5_w_glu, s5_b_glu, w_out, g_cross, g_mem, w_cq, w_ckv, w_co, g_ffn2, w_ffn2_gu, w_ffn2_down, g_final):
    bp, tp, _ = x_prompt.shape
    bs, ts, _ = x_sample.shape
    l = 0
    row = lambda a: a.reshape(1, -1).astype(F32)
    bf = lambda a: a.astype(BF16)

    are, aim, bbre, bbim = _s5_disc_call(
        s5_lambda_re[l], s5_lambda_im[l], s5_log_dt[l].reshape(S5_GROUPS, 1),
        jnp.transpose(s5_b_re[l], (0, 2, 1)), jnp.transpose(s5_b_im[l], (0, 2, 1)))
    c_re_t = jnp.transpose(s5_c_re[l], (0, 2, 1))
    c_im_t = jnp.transpose(s5_c_im[l], (0, 2, 1))
    cre_h = _halves_block_diag(c_re_t)
    cim_h = _halves_block_diag(-c_im_t)
    w_in_l = w_in[l]
    zg_lo = 2 * GLA_QK_WIDTH + GLA_WIDTH
    zg_hi = zg_lo + GLA_GATE_RANK
    w = {
        'g_ffn1': row(g_ffn1[l]), 'w_ffn1_gu': bf(w_ffn1_gu[l]), 'w_ffn1_down': bf(w_ffn1_down[l]),
        'g_mix': row(g_mix[l]),
        'w_in_main': bf(jnp.concatenate([w_in_l[:, :zg_lo], w_in_l[:, zg_hi:]], axis=1)),
        'w_zg': bf(jnp.pad(w_in_l[:, zg_lo:zg_hi], ((0, 0), (0, LANES - GLA_GATE_RANK)))),
        'w_gate': bf(jnp.pad(w_gla_gate[l], ((0, LANES - GLA_GATE_RANK), (0, 0)))),
        'b_gate': row(b_gla_gate[l]),
        'g_gla_out': row(g_gla_out[l]),
        's5_are': are.reshape(1, S5_STATE), 's5_aim': aim.reshape(1, S5_STATE),
        's5_bre': bf(_halves_block_diag(bbre)), 's5_bim': bf(_halves_block_diag(bbim)),
        's5_c': bf(jnp.concatenate([cre_h, cim_h], axis=1)),
        's5_d': row(s5_d[l]), 's5_wglu': bf(_block_diag(s5_w_glu[l])), 's5_bglu': row(s5_b_glu[l]),
        'w_out': bf(w_out[l]), 'g_cross': row(g_cross[l]), 'w_cq': bf(w_cq[l]), 'w_co': bf(w_co[l]),
        'g_ffn2': row(g_ffn2[l]), 'w_ffn2_gu': bf(w_ffn2_gu[l]), 'w_ffn2_down': bf(w_ffn2_down[l]),
        'g_final': row(g_final),
    }

    mk, mv, mk4, mv4 = _memkv_call(mem_prompt.reshape(bp * MEM_LEN, D_MODEL), row(g_mem[l]), bf(w_ckv[l]), ROW_TILE)
    zeros_state = jnp.zeros((bp, S5_GROUPS, S5_N), F32)
    y_p, st_p, re_p, im_p = _trunk_group(
        x_prompt, None, zeros_state, zeros_state, mk.reshape(bp, MEM_LEN, D_MODEL), mv.reshape(bp, MEM_LEN, D_MODEL),
        w)

    s0 = state_gla.reshape(DEPTH * bs, GLA_HEADS, GLA_DK, GLA_DV)[l * bs:(l + 1) * bs]
    y_s, st_s, re_s, im_s = _trunk_group(
        x_sample, s0, state_s5_re.reshape(DEPTH * bs, S5_GROUPS, S5_N)[l * bs:(l + 1) * bs],
        state_s5_im.reshape(DEPTH * bs, S5_GROUPS, S5_N)[l * bs:(l + 1) * bs],
        cache_mem_k.reshape(DEPTH * bs, MEM_LEN, CROSS_HEADS, CROSS_DH)[l * bs:(l + 1) * bs],
        cache_mem_v.reshape(DEPTH * bs, MEM_LEN, CROSS_HEADS, CROSS_DH)[l * bs:(l + 1) * bs],
        w)

    def gla_state_out(st, nb):
        return st[None]

    s5_out = lambda a, nb: a[None]
    kv_out = lambda a: a.reshape(1, bp, MEM_LEN, CROSS_HEADS, CROSS_DH)
    return (y_p, y_s,
            gla_state_out(st_p, bp), s5_out(re_p, bp), s5_out(im_p, bp), kv_out(mk4), kv_out(mv4),
            gla_state_out(st_s, bs), s5_out(re_s, bs), s5_out(im_s, bs))
```

```python
import functools
import math

import numpy as np
import jax
import jax.numpy as jnp
from jax import lax
from jax.experimental import pallas as pl
from jax.experimental.pallas import tpu as pltpu

F32 = jnp.float32
BF16 = jnp.bfloat16

D_MODEL = 1024
DEPTH = 1
EPS = 1e-6
D_FF = 2816
GLA_WIDTH = 512
GLA_HEADS = 4
GLA_DV = 128
GLA_DK = 64
GLA_QK_WIDTH = 256
GLA_GATE_RANK = 16
GLA_TAU = 16.0
S5_WIDTH = 512
S5_P = 16
S5_GROUPS = 32
S5_N = 64
S5_STATE = S5_GROUPS * S5_N
CROSS_HEADS = 4
CROSS_DH = 256
MEM_LEN = 256

LANES = 128
SUBLANES = 8
MXU_DIM = 256
VMEM_LIMIT_BYTES = 56 * 1024 * 1024

ROW_TILE = 512
S5_TILE_STEPS = 128
SHORT_GLA_CHUNKS = 4
SHORT_ATTN_SEQS = 8
FF_CHUNK = MXU_DIM
GLA_CHUNK_ROWS = 64
S5_LANE_CHUNK = 512
PROJ_V = 2 * GLA_QK_WIDTH
PROJ_R = PROJ_V + GLA_WIDTH
PROJ_U = PROJ_R + GLA_WIDTH
PROJ_END = PROJ_U + S5_WIDTH


def _cparams(n_grid_axes):
    return pltpu.CompilerParams(
        dimension_semantics=("arbitrary",) * n_grid_axes,
        vmem_limit_bytes=VMEM_LIMIT_BYTES,
    )


def _const_spec(shape):
    zeros = (0,) * len(shape)
    return pl.BlockSpec(shape, lambda *_: zeros, pipeline_mode=pl.Buffered(1))


def _dot(a, b):
    return jnp.dot(a, b, preferred_element_type=F32)


def _dot_nt(a, b):
    return lax.dot_general(a, b, (((1,), (1,)), ((), ())), preferred_element_type=F32)


def _dot_tn(a, b):
    return lax.dot_general(a, b, (((0,), (0,)), ((), ())), preferred_element_type=F32)


def _rms(x, g):
    return x * lax.rsqrt(jnp.mean(x * x, axis=-1, keepdims=True) + EPS) * g


def _swiglu(xn_bf, wgu_ref, wdn_ref, between=None):
    n_chunks = D_FF // FF_CHUNK

    def gate_up(c):
        lo, hi = c * FF_CHUNK, (c + 1) * FF_CHUNK
        return _dot(xn_bf, wgu_ref[:, lo:hi]), _dot(xn_bf, wgu_ref[:, D_FF + lo:D_FF + hi])

    acc = None
    nxt = gate_up(0)
    for c in range(n_chunks):
        g, u = nxt
        if c + 1 < n_chunks:
            nxt = gate_up(c + 1)
        a = (jax.nn.silu(g) * u).astype(BF16)
        d = _dot(a, wdn_ref[c * FF_CHUNK:(c + 1) * FF_CHUNK, :])
        acc = d if acc is None else acc + d
        if between is not None:
            between()
    return acc


def _log_sigmoid(z):
    return jnp.minimum(z, 0.0) - jnp.log1p(jnp.exp(-jnp.abs(z)))


def _pre_kernel(x_ref, g1_ref, wgu_ref, wdn_ref, gm_ref, win_ref, wzg_ref, wgate_ref, bgate_ref,
                h_ref, qk_ref, v_ref, la_ref, r_ref, us5_ref):
    x = x_ref[...]
    xn = _rms(x, g1_ref[...]).astype(BF16)
    h = x + 0.5 * _swiglu(xn, wgu_ref, wdn_ref)
    h_ref[...] = h
    un = _rms(h, gm_ref[...]).astype(BF16)
    p = _dot(un, win_ref[...])
    qk_ref[...] = p[:, 0:PROJ_V]
    v_ref[...] = p[:, PROJ_V:PROJ_R]
    r_ref[...] = p[:, PROJ_R:PROJ_U]
    us5_ref[...] = p[:, PROJ_U:PROJ_END]
    zg = _dot(un, wzg_ref[...])
    z = _dot(zg.astype(BF16), wgate_ref[...]) + bgate_ref[...]
    la_ref[...] = _log_sigmoid(z) * (1.0 / GLA_TAU)


def _pre_call(x2d, g1, wgu, wdn, gm, win, wzg, wgate, bgate, tm):
    n = x2d.shape[0]
    row = lambda w: pl.BlockSpec((tm, w), lambda i: (i, 0))
    widths = (D_MODEL, 2 * GLA_QK_WIDTH, GLA_WIDTH, GLA_QK_WIDTH, GLA_WIDTH, S5_WIDTH)
    out_shapes = [jax.ShapeDtypeStruct((n, w), F32) for w in widths]
    return pl.pallas_call(
        _pre_kernel,
        grid=(n // tm,),
        in_specs=[row(D_MODEL), _const_spec(g1.shape), _const_spec(wgu.shape), _const_spec(wdn.shape),
                  _const_spec(gm.shape), _const_spec(win.shape), _const_spec(wzg.shape),
                  _const_spec(wgate.shape), _const_spec(bgate.shape)],
        out_specs=[row(w) for w in widths],
        out_shape=out_shapes,
        compiler_params=_cparams(1),
        name="pre_ffn_proj",
    )(x2d, g1, wgu, wdn, gm, win, wzg, wgate, bgate)


def _memkv_kernel(m_ref, g_ref, w_ref, k_ref, v_ref, k4_ref, v4_ref):
    mn = _rms(m_ref[...], g_ref[...]).astype(BF16)
    kv = _dot(mn, w_ref[...])
    k_ref[...] = kv[:, :D_MODEL]
    v_ref[...] = kv[:, D_MODEL:]
    def head_split(x):
        return jnp.transpose(jnp.stack([x[:, hd * CROSS_DH:(hd + 1) * CROSS_DH] for hd in range(CROSS_HEADS)]),
                             (1, 0, 2))

    k4_ref[...] = head_split(kv[:, :D_MODEL])
    v4_ref[...] = head_split(kv[:, D_MODEL:])


def _memkv_call(mem2d, g, w, tm):
    n = mem2d.shape[0]
    row = pl.BlockSpec((tm, D_MODEL), lambda i: (i, 0))
    row4 = pl.BlockSpec((tm, CROSS_HEADS, CROSS_DH), lambda i: (i, 0, 0))
    return pl.pallas_call(
        _memkv_kernel,
        grid=(n // tm,),
        in_specs=[row, _const_spec(g.shape), _const_spec(w.shape)],
        out_specs=[row, row, row4, row4],
        out_shape=[jax.ShapeDtypeStruct((n, D_MODEL), F32)] * 2
        + [jax.ShapeDtypeStruct((n, CROSS_HEADS, CROSS_DH), F32)] * 2,
        compiler_params=_cparams(1),
        name="memory_kv",
    )(mem2d, g, w)


def _mid_kernel(h_ref, og_ref, os_ref, wo_ref, gc_ref, wq_ref, h2_ref, q_ref):
    h2 = (h_ref[...] + _dot(og_ref[...].astype(BF16), wo_ref[0:GLA_WIDTH, :])
          + _dot(os_ref[...].astype(BF16), wo_ref[GLA_WIDTH:, :]))
    h2_ref[...] = h2
    hn = _rms(h2, gc_ref[...]).astype(BF16)
    q_ref[...] = _dot(hn, wq_ref[...])


def _mid_call(h, og, os_, wo, gc, wq, tm):
    n = h.shape[0]
    row = lambda w: pl.BlockSpec((tm, w), lambda i: (i, 0))
    return pl.pallas_call(
        _mid_kernel,
        grid=(n // tm,),
        in_specs=[row(D_MODEL), row(GLA_WIDTH), row(S5_WIDTH), _const_spec(wo.shape), _const_spec(gc.shape),
                  _const_spec(wq.shape)],
        out_specs=[row(D_MODEL), row(D_MODEL)],
        out_shape=[jax.ShapeDtypeStruct((n, D_MODEL), F32)] * 2,
        compiler_params=_cparams(1),
        name="mix_out_cross_q",
    )(h, og, os_, wo, gc, wq)


def _post_kernel(h2_ref, o_ref, wco_ref, g2_ref, wgu_ref, wdn_ref, gf_ref, y_ref):
    h3 = h2_ref[...] + _dot(o_ref[...].astype(BF16), wco_ref[...])
    hn = _rms(h3, g2_ref[...]).astype(BF16)
    h4 = h3 + 0.5 * _swiglu(hn, wgu_ref, wdn_ref)
    y_ref[...] = _rms(h4, gf_ref[...])


def _post_call(h2, o, wco, g2, wgu, wdn, gf, tm):
    n = h2.shape[0]
    row = pl.BlockSpec((tm, D_MODEL), lambda i: (i, 0))
    return pl.pallas_call(
        _post_kernel,
        grid=(n // tm,),
        in_specs=[row, row, _const_spec(wco.shape), _const_spec(g2.shape), _const_spec(wgu.shape),
                  _const_spec(wdn.shape), _const_spec(gf.shape)],
        out_specs=row,
        out_shape=jax.ShapeDtypeStruct((n, D_MODEL), F32),
        compiler_params=_cparams(1),
        name="post_ffn_final",
    )(h2, o, wco, g2, wgu, wdn, gf)


def _tail_kernel(h_ref, og_ref, os_ref, k_ref, v_ref, wo_ref, gc_ref, wq_ref, wco_ref, g2_ref, wgu_ref,
                 wdn_ref, gf_ref, y_ref):
    h2 = (h_ref[0] + _dot(og_ref[0].astype(BF16), wo_ref[0:GLA_WIDTH, :])
          + _dot(os_ref[0].astype(BF16), wo_ref[GLA_WIDTH:, :]))
    q = _dot(_rms(h2, gc_ref[...]).astype(BF16), wq_ref[...]).astype(BF16)
    heads = [slice(hd * CROSS_DH, (hd + 1) * CROSS_DH) for hd in range(CROSS_HEADS)]
    scores = [_dot_nt(q[:, sl], k_ref[0, :, sl].astype(BF16)) * (CROSS_DH ** -0.5) for sl in heads]
    probs = [_softmax(s).astype(BF16) for s in scores]
    outs = [_dot(p, v_ref[0, :, sl].astype(BF16)).astype(BF16) for p, sl in zip(probs, heads)]
    h3 = h2 + _dot(jnp.concatenate(outs, axis=-1), wco_ref[...])
    hn = _rms(h3, g2_ref[...]).astype(BF16)
    h4 = h3 + 0.5 * _swiglu(hn, wgu_ref, wdn_ref)
    y_ref[0] = _rms(h4, gf_ref[...])


def _tail_call(h3d, og3, os3, k3, v3, w, tm):
    s, t, _ = h3d.shape
    row = lambda wd: pl.BlockSpec((1, tm, wd), lambda i, j: (i, j, 0))
    kvspec = pl.BlockSpec((1, MEM_LEN, D_MODEL), lambda i, j: (i, 0, 0))
    consts = [w['w_out'], w['g_cross'], w['w_cq'], w['w_co'], w['g_ffn2'], w['w_ffn2_gu'], w['w_ffn2_down'],
              w['g_final']]
    return pl.pallas_call(
        _tail_kernel,
        grid=(s, t // tm),
        in_specs=[row(D_MODEL), row(GLA_WIDTH), row(S5_WIDTH), kvspec, kvspec] + [_const_spec(a.shape) for a in consts],
        out_specs=row(D_MODEL),
        out_shape=jax.ShapeDtypeStruct((s, t, D_MODEL), F32),
        compiler_params=_cparams(2),
        name="tail_attn_ffn",
    )(h3d, og3, os3, k3, v3, *consts)


def _softmax(s):
    e = jnp.exp(s - jnp.max(s, axis=-1, keepdims=True))
    return e / jnp.sum(e, axis=-1, keepdims=True)


def _load_mem(ref, j):
    heads_major = jnp.transpose(ref[j], (1, 0, 2))
    return jnp.concatenate([heads_major[hd] for hd in range(CROSS_HEADS)], axis=-1).astype(BF16)


def _attn_packed_kernel(q_ref, k_ref, v_ref, o_ref, *, n_seq):
    t = q_ref.shape[1]
    shape = (CROSS_HEADS * t, D_MODEL)
    q_mask = (_iota2(shape, 0) // t) == (_iota2(shape, 1) // CROSS_DH)
    o_lane_head = _iota2((t, D_MODEL), 1) // CROSS_DH
    scores = []
    for j in range(n_seq):
        q_stack = jnp.where(q_mask, jnp.concatenate([q_ref[j]] * CROSS_HEADS, axis=0), 0.0).astype(BF16)
        scores.append(_dot_nt(q_stack, _load_mem(k_ref, j)) * (CROSS_DH ** -0.5))
    probs = [_softmax(s).astype(BF16) for s in scores]
    for j in range(n_seq):
        full = _dot(probs[j], _load_mem(v_ref, j))
        o = jnp.zeros((t, D_MODEL), F32)
        for hd in range(CROSS_HEADS):
            o = jnp.where(o_lane_head == hd, full[hd * t:(hd + 1) * t, :], o)
        o_ref[j] = o.astype(o_ref.dtype)


def _attn_call(q3, k3, v3, n_seq):
    s, t, _ = q3.shape
    assert t * CROSS_HEADS <= MXU_DIM and s % n_seq == 0
    qspec = pl.BlockSpec((n_seq, t, D_MODEL), lambda i: (i, 0, 0))
    kvspec = pl.BlockSpec((n_seq, MEM_LEN, CROSS_HEADS, CROSS_DH), lambda i: (i, 0, 0, 0))
    return pl.pallas_call(
        functools.partial(_attn_packed_kernel, n_seq=n_seq),
        grid=(s // n_seq,),
        in_specs=[qspec, kvspec, kvspec],
        out_specs=qspec,
        out_shape=jax.ShapeDtypeStruct(q3.shape, F32),
        compiler_params=_cparams(1),
        name="cross_attention",
    )(q3, k3, v3)


def _split3(x):
    hi = x.astype(BF16)
    r1 = x - hi.astype(F32)
    mid = r1.astype(BF16)
    lo = (r1 - mid.astype(F32)).astype(BF16)
    return hi, mid, lo


def _dot01(m01, x):
    hi, mid, lo = _split3(x)
    return _dot(m01, hi) + _dot(m01, mid) + _dot(m01, lo)


def _iota2(shape, dim):
    return lax.broadcasted_iota(jnp.int32, shape, dim)


def _gla_stages(qk_ref, v_ref, la_ref, r_ref, o_ref, g_ref, tri_ref, lvl_ref, khm_ref, vhm_ref, shm_ref,
                states, put_states, *, seq_len, n_chunks):
    c_rows = GLA_CHUNK_ROWS
    n_sub = c_rows // seq_len
    log2 = lambda n: int(math.log2(n))
    tri = tri_ref[...]
    lvl = lvl_ref[...]
    rowq = _iota2((c_rows, GLA_QK_WIDTH), 0)
    u_head = [(_iota2((GLA_DV, GLA_QK_WIDTH), 1) >> log2(GLA_DK)) == hd for hd in range(GLA_HEADS)]

    def head_blocks(x, mask_ref):
        return jnp.concatenate([x.astype(BF16)] * GLA_HEADS, axis=0) * mask_ref[...]

    g_out = g_ref[...]
    chunks = range(n_chunks)
    rows_of = lambda c: slice(c * c_rows, (c + 1) * c_rows)
    sub_of = lambda j: slice(j * seq_len, (j + 1) * seq_len)
    q = [qk_ref[rows_of(c), 0:GLA_QK_WIDTH] * (GLA_DK ** -0.5) for c in chunks]
    k = [qk_ref[rows_of(c), GLA_QK_WIDTH:2 * GLA_QK_WIDTH] for c in chunks]
    v = [v_ref[rows_of(c), :] for c in chunks]
    b = [_dot01(tri, la_ref[rows_of(c), :]) for c in chunks]
    yield

    a_mat = [jnp.where(lvl == 0, _dot_nt(q[c].astype(BF16), head_blocks(k[c], khm_ref)), 0.0) for c in chunks]
    last = list(b)
    m = 1
    while m < seq_len:
        yield
        even = ((rowq >> log2(m)) & 1) == 0
        for c in chunks:
            prev_last = pltpu.roll(last[c], m, axis=0)
            e = jnp.exp(jnp.minimum(jnp.where(even, last[c] - b[c], b[c] - prev_last), 0.0))
            a_lvl = _dot_nt((q[c] * e).astype(BF16), head_blocks(k[c] * e, khm_ref))
            a_mat[c] = jnp.where(lvl == log2(m) + 1, a_lvl, a_mat[c])
            last[c] = jnp.where(even, pltpu.roll(last[c], c_rows - m, axis=0), last[c])
        m *= 2
    yield
    o_intra = [_dot(a_mat[c].astype(BF16), head_blocks(v[c], vhm_ref)) for c in chunks]
    yield

    narrow = (lambda a: a.astype(BF16)) if seq_len % 16 == 0 else (lambda a: a)
    q_dec = [narrow(q[c] * jnp.exp(b[c])) for c in chunks]
    k_dec = [narrow(k[c] * jnp.exp(jnp.minimum(last[c] - b[c], 0.0))) for c in chunks]
    u_t = {}
    for c in chunks:
        vc = narrow(v[c])
        for j in range(n_sub):
            u_full = _dot_tn(vc[sub_of(j)], k_dec[c][sub_of(j)])
            acc = jnp.zeros((GLA_DV, GLA_QK_WIDTH), F32)
            for hd in range(GLA_HEADS):
                acc = jnp.where(u_head[hd], u_full[hd * GLA_DV:(hd + 1) * GLA_DV, :], acc)
            u_t[c, j] = acc

    states = list(states)
    state_of = (lambda c, j: c * n_sub + j) if n_sub > 1 else (lambda c, j: 0)
    for c in chunks:
        if c % 2 == 0:
            yield
        o_parts = []
        for j in range(n_sub):
            sj = state_of(c, j)
            o_parts.append(_dot_nt(q_dec[c][sub_of(j)], head_blocks(states[sj], shm_ref)))
            decay = jnp.exp(last[c][j * seq_len:j * seq_len + 1, :])
            states[sj] = states[sj] * decay + u_t[c, j]
        o_inter = o_parts[0] if n_sub == 1 else jnp.concatenate(o_parts, axis=0)
        o = o_intra[c] + o_inter
        r = r_ref[rows_of(c), :]
        outs = []
        for hd in range(GLA_HEADS):
            sl = slice(hd * GLA_DV, (hd + 1) * GLA_DV)
            outs.append(_rms(o[:, sl], g_out) * jax.nn.silu(r[:, sl]))
        o_ref[rows_of(c), :] = jnp.concatenate(outs, axis=-1).astype(o_ref.dtype)
    put_states(states)


def _gla_kernel(*refs, seq_len, n_chunks, n_state, has_s0):
    ins, rest = refs[:10], refs[10:]
    qk_ref, v_ref, la_ref, r_ref, g_ref = ins[:5]
    if has_s0:
        s0_ref, o_ref, st_ref, s_scr = rest
    else:
        o_ref, st_ref, s_scr = rest
        s0_ref = None
    t_idx = pl.program_id(1)

    @pl.when(t_idx == 0)
    def _():
        if has_s0:
            for j in range(n_state):
                s_scr[j] = s0_ref[j].reshape(GLA_QK_WIDTH, GLA_DV).T
        else:
            s_scr[...] = jnp.zeros_like(s_scr)

    def put_states(states):
        for j, s_t in enumerate(states):
            s_scr[j] = s_t

    for _ in _gla_stages(qk_ref.at[0], v_ref.at[0], la_ref.at[0], r_ref.at[0], o_ref.at[0], g_ref, *ins[5:],
                         [s_scr[j] for j in range(n_state)], put_states, seq_len=seq_len, n_chunks=n_chunks):
        pass

    @pl.when(t_idx == pl.num_programs(1) - 1)
    def _():
        for j in range(n_state):
            st_ref[j] = s_scr[j].T.reshape(GLA_HEADS, GLA_DK, GLA_DV)


def _gla_index_constants(seq_len):
    c = GLA_CHUNK_ROWS
    i = np.arange(c)[:, None]
    s = np.arange(c)[None, :]
    same_seq = (i // seq_len) == (s // seq_len)
    tri = (same_seq & (s <= i)).astype(np.float32)
    top_bit = np.floor(np.log2(np.maximum(i ^ s, 1))).astype(np.int32)
    lvl = np.where(same_seq & (s < i), top_bit + 1, -1)
    lvl = np.where(i == s, 0, lvl).astype(np.int32)
    head_mask = lambda rows_per_head, lanes_per_head: (
        (np.arange(GLA_HEADS * rows_per_head)[:, None] // rows_per_head)
        == (np.arange(GLA_HEADS * lanes_per_head)[None, :] // lanes_per_head)).astype(np.float32)
    return [jnp.asarray(tri, BF16), jnp.asarray(np.tile(lvl, (1, GLA_HEADS))),
            jnp.asarray(head_mask(c, GLA_DK), BF16), jnp.asarray(head_mask(c, GLA_DV), BF16),
            jnp.asarray(head_mask(GLA_DV, GLA_DK), BF16)]


def _gla_call(qk3, v3, la3, r3, g_out, s0, seq_len, tile_rows, n_state):
    n_outer, rows, _ = qk3.shape
    n_chunks = tile_rows // GLA_CHUNK_ROWS
    spec = lambda w: pl.BlockSpec((1, tile_rows, w), lambda i, t: (i, t, 0))
    st_spec = pl.BlockSpec((n_state, GLA_HEADS, GLA_DK, GLA_DV), lambda i, t: (i, 0, 0, 0))
    has_s0 = s0 is not None
    consts = [g_out] + _gla_index_constants(seq_len)
    in_specs = ([spec(2 * GLA_QK_WIDTH), spec(GLA_WIDTH), spec(GLA_QK_WIDTH), spec(GLA_WIDTH)]
                + [_const_spec(a.shape) for a in consts])
    args = [qk3, v3, la3, r3] + consts
    if has_s0:
        in_specs.append(st_spec)
        args.append(s0)
    return pl.pallas_call(
        functools.partial(_gla_kernel, seq_len=seq_len, n_chunks=n_chunks, n_state=n_state, has_s0=has_s0),
        grid=(n_outer, rows // tile_rows),
        in_specs=in_specs,
        out_specs=[spec(GLA_WIDTH), st_spec],
        out_shape=[jax.ShapeDtypeStruct((n_outer, rows, GLA_WIDTH), BF16),
                   jax.ShapeDtypeStruct((n_outer * n_state, GLA_HEADS, GLA_DK, GLA_DV), F32)],
        scratch_shapes=[pltpu.VMEM((n_state, GLA_DV, GLA_QK_WIDTH), F32)],
        compiler_params=_cparams(2),
        name="gla_mixer",
    )(*args)


def _pre_gla_kernel(x_ref, g1_ref, wgu_ref, wdn_ref, gm_ref, win_ref, wzg_ref, wgate_ref, bgate_ref, gout_ref,
                    tri_ref, lvl_ref, khm_ref, vhm_ref, shm_ref,
                    h_ref, us5_ref, og_ref, st_ref,
                    qk_scr, v_scr, la_scr, r_scr, s_scr, *, tiles_per_seq):
    i = pl.program_id(0)
    last_step = pl.num_programs(0) - 1
    n_chunks = x_ref.shape[0] // GLA_CHUNK_ROWS

    @pl.when(i == 0)
    def _():
        for scr in (qk_scr, v_scr, la_scr, r_scr, s_scr):
            scr[...] = jnp.zeros_like(scr)

    def gla_stages():
        first_of_seq = lax.rem(i + (tiles_per_seq - 1), tiles_per_seq) == 0
        state = jnp.where(first_of_seq, 0.0, s_scr[0])

        def put_states(states):
            s_scr[0] = states[0]
            st_ref[0] = states[0].T.reshape(GLA_HEADS, GLA_DK, GLA_DV)

        return _gla_stages(qk_scr, v_scr, la_scr, r_scr, og_ref, gout_ref, tri_ref, lvl_ref, khm_ref, vhm_ref,
                           shm_ref, [state], put_states, seq_len=GLA_CHUNK_ROWS, n_chunks=n_chunks)

    @pl.when(i < last_step)
    def _():
        stages = gla_stages()
        next(stages)
        x = x_ref[...]
        xn = _rms(x, g1_ref[...]).astype(BF16)
        h = x + 0.5 * _swiglu(xn, wgu_ref, wdn_ref, between=lambda: next(stages, None))
        h_ref[...] = h
        un = _rms(h, gm_ref[...]).astype(BF16)
        p = []
        for lo, hi in ((0, PROJ_V), (PROJ_V, PROJ_R), (PROJ_R, PROJ_U), (PROJ_U, PROJ_END)):
            p.append(_dot(un, win_ref[:, lo:hi]))
            next(stages, None)
        zg = _dot(un, wzg_ref[...])
        z = _dot(zg.astype(BF16), wgate_ref[...]) + bgate_ref[...]
        for _ in stages:
            pass
        qk_scr[...] = p[0]
        v_scr[...] = p[1]
        r_scr[...] = p[2]
        us5_ref[...] = p[3]
        la_scr[...] = _log_sigmoid(z) * (1.0 / GLA_TAU)

    @pl.when(i == last_step)
    def _():
        for _ in gla_stages():
            pass


def _pre_gla_call(x2d, w, tm, seq_rows):
    n = x2d.shape[0]
    n_tiles = n // tm
    tiles_per_seq = seq_rows // tm
    cur = lambda wd: pl.BlockSpec((tm, wd), lambda i: (jnp.minimum(i, n_tiles - 1), 0))
    prev = lambda wd: pl.BlockSpec((tm, wd), lambda i: (jnp.maximum(i - 1, 0), 0))
    st_spec = pl.BlockSpec((1, GLA_HEADS, GLA_DK, GLA_DV),
                           lambda i: (jnp.maximum(i - 1, 0) // tiles_per_seq, 0, 0, 0))
    consts = [w['g_ffn1'], w['w_ffn1_gu'], w['w_ffn1_down'], w['g_mix'], w['w_in_main'], w['w_zg'], w['w_gate'],
              w['b_gate'], w['g_gla_out']] + _gla_index_constants(GLA_CHUNK_ROWS)
    return pl.pallas_call(
        functools.partial(_pre_gla_kernel, tiles_per_seq=tiles_per_seq),
        grid=(n_tiles + 1,),
        in_specs=[cur(D_MODEL)] + [_const_spec(a.shape) for a in consts],
        out_specs=[cur(D_MODEL), cur(S5_WIDTH), prev(GLA_WIDTH), st_spec],
        out_shape=[jax.ShapeDtypeStruct((n, D_MODEL), F32), jax.ShapeDtypeStruct((n, S5_WIDTH), F32),
                   jax.ShapeDtypeStruct((n, GLA_WIDTH), BF16),
                   jax.ShapeDtypeStruct((n // seq_rows, GLA_HEADS, GLA_DK, GLA_DV), F32)],
        scratch_shapes=[pltpu.VMEM((tm, 2 * GLA_QK_WIDTH), F32), pltpu.VMEM((tm, GLA_WIDTH), F32),
                        pltpu.VMEM((tm, GLA_QK_WIDTH), F32), pltpu.VMEM((tm, GLA_WIDTH), F32),
                        pltpu.VMEM((1, GLA_DV, GLA_QK_WIDTH), F32)],
        compiler_params=_cparams(1),
        name="pre_ffn_gla",
    )(x2d, *consts)


def _s5_disc_kernel(lre_ref, lim_ref, ldt_ref, bre_ref, bim_ref, are_ref, aim_ref, bbre_ref, bbim_ref):
    lr = lre_ref[...]
    li = lim_ref[...]
    dt = jnp.exp(ldt_ref[...])
    mag = jnp.exp(dt * lr)
    ab_re = mag * jnp.cos(dt * li)
    ab_im = mag * jnp.sin(dt * li)
    den = lr * lr + li * li
    coef_re = ((ab_re - 1.0) * lr + ab_im * li) / den
    coef_im = (ab_im * lr - (ab_re - 1.0) * li) / den
    are_ref[...] = ab_re
    aim_ref[...] = ab_im
    br = bre_ref[...]
    bi = bim_ref[...]
    cr = coef_re[:, None, :]
    cim = coef_im[:, None, :]
    bbre_ref[...] = cr * br - cim * bi
    bbim_ref[...] = cr * bi + cim * br


def _s5_disc_call(lre, lim, ldt, bre_t, bim_t):
    gn = jax.ShapeDtypeStruct((S5_GROUPS, S5_N), F32)
    gpn = jax.ShapeDtypeStruct((S5_GROUPS, S5_P, S5_N), F32)
    return pl.pallas_call(
        _s5_disc_kernel,
        out_shape=[gn, gn, gpn, gpn],
        name="s5_discretise",
    )(lre, lim, ldt, bre_t, bim_t)


def _to_time_major(u_ref, perm_scr, n_seq, t_tile):
    if n_seq == SUBLANES:
        n_slab = S5_WIDTH // LANES
        for j in range(n_seq):
            for sb in range(n_slab):
                perm_scr[sb, pl.ds(j, t_tile, stride=n_seq), :] = u_ref[j, :, sb * LANES:(sb + 1) * LANES]
        return jnp.concatenate([perm_scr[sb] for sb in range(n_slab)], axis=-1)
    return jnp.concatenate([u_ref[:, t, :] for t in range(t_tile)], axis=0)


def _from_time_major(y, o_ref, perm_scr, n_seq, t_tile):
    if n_seq == SUBLANES:
        n_slab = S5_WIDTH // LANES
        for sb in range(n_slab):
            perm_scr[sb] = y[:, sb * LANES:(sb + 1) * LANES]
        for j in range(n_seq):
            for sb in range(n_slab):
                o_ref[j, :, sb * LANES:(sb + 1) * LANES] = perm_scr[sb, pl.ds(j, t_tile, stride=n_seq), :]
    else:
        for t in range(t_tile):
            o_ref[:, t, :] = y[t * n_seq:(t + 1) * n_seq, :]


def _s5_kernel(u_ref, x0re_ref, x0im_ref, are_ref, aim_ref, bre_ref, bim_ref, c_ref, d_ref, wglu_ref,
               bglu_ref, o_ref, xre_ref, xim_ref, sre_scr, sim_scr, xre_scr, xim_scr, perm_scr, *, n_seq, t_tile):
    half_w = S5_WIDTH // 2
    half_s = S5_STATE // 2
    state_major = x0re_ref.shape == (S5_STATE, n_seq)

    @pl.when(pl.program_id(0) == 0)
    def _():
        xre_scr[...] = x0re_ref[...].T if state_major else x0re_ref[...]
        xim_scr[...] = x0im_ref[...].T if state_major else x0im_ref[...]

    rows = t_tile * n_seq
    u = _to_time_major(u_ref, perm_scr, n_seq, t_tile)
    ub = u.astype(BF16)
    for hf in range(2):
        uh = ub[:, hf * half_w:(hf + 1) * half_w]
        st = slice(hf * half_s, (hf + 1) * half_s)
        sre_scr[:, :, st] = _dot(uh, bre_ref[hf]).reshape(t_tile, n_seq, half_s)
        sim_scr[:, :, st] = _dot(uh, bim_ref[hf]).reshape(t_tile, n_seq, half_s)

    for lc in range(S5_STATE // S5_LANE_CHUNK):
        lanes = slice(lc * S5_LANE_CHUNK, (lc + 1) * S5_LANE_CHUNK)
        ar = jnp.broadcast_to(are_ref[:, lanes], (SUBLANES, S5_LANE_CHUNK))
        ai = jnp.broadcast_to(aim_ref[:, lanes], (SUBLANES, S5_LANE_CHUNK))
        for sc in range(n_seq // SUBLANES):
            srows = slice(sc * SUBLANES, (sc + 1) * SUBLANES)
            xr = xre_scr[srows, lanes]
            xi = xim_scr[srows, lanes]
            for t in range(t_tile):
                nxr = ar * xr - ai * xi + sre_scr[t, srows, lanes]
                nxi = ar * xi + ai * xr + sim_scr[t, srows, lanes]
                sre_scr[t, srows, lanes] = nxr
                sim_scr[t, srows, lanes] = nxi
                xr, xi = nxr, nxi
            xre_scr[srows, lanes] = xr
            xim_scr[srows, lanes] = xi

    xr_all = sre_scr[...].reshape(rows, S5_STATE).astype(BF16)
    xi_all = sim_scr[...].reshape(rows, S5_STATE).astype(BF16)
    ys = []
    for hf in range(2):
        st = slice(hf * half_s, (hf + 1) * half_s)
        ys.append(_dot(xr_all[:, st], c_ref[hf, 0:half_s, :]) + _dot(xi_all[:, st], c_ref[hf, half_s:, :]))
    y = jnp.concatenate(ys, axis=-1) + d_ref[...] * u
    z = jax.nn.gelu(y)
    gate = jax.nn.sigmoid(_dot(z.astype(BF16), wglu_ref[...]) + bglu_ref[...])
    _from_time_major(z * gate, o_ref, perm_scr, n_seq, t_tile)

    @pl.when(pl.program_id(0) == pl.num_programs(0) - 1)
    def _():
        xre_ref[...] = xre_scr[...].T if state_major else xre_scr[...]
        xim_ref[...] = xim_scr[...].T if state_major else xim_scr[...]


def _s5_call(u3, x0re, x0im, are, aim, bre, bim, cmat, dskip, wglu, bglu, t_tile):
    n_seq, t, _ = u3.shape
    uspec = pl.BlockSpec((n_seq, t_tile, S5_WIDTH), lambda i: (0, i, 0))
    perm_rows = t_tile * n_seq if n_seq == SUBLANES else SUBLANES
    assert x0re.shape in ((n_seq, S5_STATE), (S5_STATE, n_seq)) and x0im.shape == x0re.shape
    xspec = pl.BlockSpec(x0re.shape, lambda i: (0, 0))
    consts = [are, aim, bre, bim, cmat, dskip, wglu, bglu]
    return pl.pallas_call(
        functools.partial(_s5_kernel, n_seq=n_seq, t_tile=t_tile),
        grid=(t // t_tile,),
        in_specs=[uspec, xspec, xspec] + [_const_spec(a.shape) for a in consts],
        out_specs=[uspec, xspec, xspec],
        out_shape=[jax.ShapeDtypeStruct((n_seq, t, S5_WIDTH), F32),
                   jax.ShapeDtypeStruct(x0re.shape, F32),
                   jax.ShapeDtypeStruct(x0re.shape, F32)],
        scratch_shapes=[pltpu.VMEM((t_tile, n_seq, S5_STATE), F32), pltpu.VMEM((t_tile, n_seq, S5_STATE), F32),
                        pltpu.VMEM((n_seq, S5_STATE), F32), pltpu.VMEM((n_seq, S5_STATE), F32),
                        pltpu.VMEM((S5_WIDTH // LANES, perm_rows, LANES), F32)],
        compiler_params=_cparams(1),
        name="s5_mixer",
    )(u3, x0re, x0im, *consts)


def _block_diag(blocks):
    g, a, b = blocks.shape
    spread = np.tile(np.eye(b, dtype=np.float32), (1, g))
    keep = (np.arange(g * a)[:, None] // a) == (np.arange(g * b)[None, :] // b)
    return jnp.where(keep, jnp.dot(blocks.reshape(g * a, b), spread), 0.0)


def _halves_block_diag(blocks):
    g = blocks.shape[0]
    return jnp.stack([_block_diag(blocks[:g // 2]), _block_diag(blocks[g // 2:])])


def _trunk_group(x3, s_gla0, x0re, x0im, mem_k3, mem_v3, w):
    s, t, _ = x3.shape
    n = s * t
    tm = ROW_TILE
    long_seq = t >= tm
    if long_seq:
        assert s_gla0 is None and t % tm == 0
        h, us5, o_gla, st_new = _pre_gla_call(x3.reshape(n, D_MODEL), w, tm, t)
    else:
        assert GLA_CHUNK_ROWS % t == 0 and n % (SHORT_GLA_CHUNKS * GLA_CHUNK_ROWS) == 0
        h, qk, v, la, r, us5 = _pre_call(x3.reshape(n, D_MODEL), w['g_ffn1'], w['w_ffn1_gu'], w['w_ffn1_down'],
                                         w['g_mix'], w['w_in_main'], w['w_zg'], w['w_gate'], w['b_gate'], tm)
        rows = SHORT_GLA_CHUNKS * GLA_CHUNK_ROWS
        shp = lambda a: a.reshape(n // rows, rows, a.shape[-1])
        o_gla, st_new = _gla_call(shp(qk), shp(v), shp(la), shp(r), w['g_gla_out'], s_gla0, t, rows, rows // t)
    o_s5, xre, xim = _s5_call(us5.reshape(s, t, S5_WIDTH), x0re, x0im, w['s5_are'], w['s5_aim'], w['s5_bre'],
                              w['s5_bim'], w['s5_c'], w['s5_d'], w['s5_wglu'], w['s5_bglu'], min(t, S5_TILE_STEPS))
    if long_seq:
        y = _tail_call(h.reshape(s, t, D_MODEL), o_gla.reshape(s, t, GLA_WIDTH), o_s5, mem_k3, mem_v3, w, tm)
        return y, st_new, xre, xim
    o_s5 = o_s5.reshape(n, S5_WIDTH)
    h2, q = _mid_call(h, o_gla.reshape(n, GLA_WIDTH), o_s5, w['w_out'], w['g_cross'], w['w_cq'], tm)
    o_att = _attn_call(q.reshape(s, t, D_MODEL), mem_k3, mem_v3, SHORT_ATTN_SEQS)
    y = _post_call(h2, o_att.reshape(n, D_MODEL), w['w_co'], w['g_ffn2'], w['w_ffn2_gu'], w['w_ffn2_down'],
                   w['g_final'], tm)
    return y.reshape(s, t, D_MODEL), st_new, xre, xim


def kernel(x_prompt, x_sample, state_gla, state_s5_re, state_s5_im, cache_mem_k, cache_mem_v, mem_prompt, g_ffn1, w_ffn1_gu, w_ffn1_down, g_mix, w_in, w_gla_gate, b_gla_gate, g_gla_out, s5_lambda_re, s5_lambda_im, s5_log_dt, s5_b_re, s5_b_im, s5_c_re, s5_c_im, s5_d, s5_w_glu, s5_b_glu, w_out, g_cross, g_mem, w_cq, w_ckv, w_co, g_ffn2, w_ffn2_gu, w_ffn2_down, g_final):
    bp, tp, _ = x_prompt.shape
    bs, ts, _ = x_sample.shape
    l = 0
    row = lambda a: a.reshape(1, -1).astype(F32)
    bf = lambda a: a.astype(BF16)

    are, aim, bbre, bbim = _s5_disc_call(
        s5_lambda_re[l], s5_lambda_im[l], s5_log_dt[l].reshape(S5_GROUPS, 1),
        jnp.transpose(s5_b_re[l], (0, 2, 1)), jnp.transpose(s5_b_im[l], (0, 2, 1)))
    c_re_t = jnp.transpose(s5_c_re[l], (0, 2, 1))
    c_im_t = jnp.transpose(s5_c_im[l], (0, 2, 1))
    cre_h = _halves_block_diag(c_re_t)
    cim_h = _halves_block_diag(-c_im_t)
    w_in_l = w_in[l]
    zg_lo = 2 * GLA_QK_WIDTH + GLA_WIDTH
    zg_hi = zg_lo + GLA_GATE_RANK
    w = {
        'g_ffn1': row(g_ffn1[l]), 'w_ffn1_gu': bf(w_ffn1_gu[l]), 'w_ffn1_down': bf(w_ffn1_down[l]),
        'g_mix': row(g_mix[l]),
        'w_in_main': bf(jnp.concatenate([w_in_l[:, :zg_lo], w_in_l[:, zg_hi:]], axis=1)),
        'w_zg': bf(jnp.pad(w_in_l[:, zg_lo:zg_hi], ((0, 0), (0, LANES - GLA_GATE_RANK)))),
        'w_gate': bf(jnp.pad(w_gla_gate[l], ((0, LANES - GLA_GATE_RANK), (0, 0)))),
        'b_gate': row(b_gla_gate[l]),
        'g_gla_out': row(g_gla_out[l]),
        's5_are': are.reshape(1, S5_STATE), 's5_aim': aim.reshape(1, S5_STATE),
        's5_bre': bf(_halves_block_diag(bbre)), 's5_bim': bf(_halves_block_diag(bbim)),
        's5_c': bf(jnp.concatenate([cre_h, cim_h], axis=1)),
        's5_d': row(s5_d[l]), 's5_wglu': bf(_block_diag(s5_w_glu[l])), 's5_bglu': row(s5_b_glu[l]),
        'w_out': bf(w_out[l]), 'g_cross': row(g_cross[l]), 'w_cq': bf(w_cq[l]), 'w_co': bf(w_co[l]),
        'g_ffn2': row(g_ffn2[l]), 'w_ffn2_gu': bf(w_ffn2_gu[l]), 'w_ffn2_down': bf(w_ffn2_down[l]),
        'g_final': row(g_final),
    }

    mk, mv, mk4, mv4 = _memkv_call(mem_prompt.reshape(bp * MEM_LEN, D_MODEL), row(g_mem[l]), bf(w_ckv[l]), ROW_TILE)
    zeros_state = jnp.zeros((bp, S5_STATE), F32)
    y_p, st_p, re_p, im_p = _trunk_group(
        x_prompt, None, zeros_state, zeros_state, mk.reshape(bp, MEM_LEN, D_MODEL), mv.reshape(bp, MEM_LEN, D_MODEL),
        w)

    if bs % LANES == 0:
        state_major = lambda a: jnp.transpose(a[l], (1, 2, 0)).reshape(S5_STATE, bs)
    else:
        state_major = lambda a: a[l].reshape(bs, S5_STATE)
    s0 = state_gla.reshape(DEPTH * bs, GLA_HEADS, GLA_DK, GLA_DV)[l * bs:(l + 1) * bs]
    y_s, st_s, re_s, im_s = _trunk_group(
        x_sample, s0, state_major(state_s5_re), state_major(state_s5_im),
        cache_mem_k.reshape(DEPTH * bs, MEM_LEN, CROSS_HEADS, CROSS_DH)[l * bs:(l + 1) * bs],
        cache_mem_v.reshape(DEPTH * bs, MEM_LEN, CROSS_HEADS, CROSS_DH)[l * bs:(l + 1) * bs],
        w)

    def gla_state_out(st, nb):
        return st[None]

    def s5_out(a, nb):
        if a.shape[0] != nb:
            a = jnp.transpose(a.reshape(S5_GROUPS, S5_N, nb), (2, 0, 1))
        return a.reshape(1, nb, S5_GROUPS, S5_N)
    kv_out = lambda a: a.reshape(1, bp, MEM_LEN, CROSS_HEADS, CROSS_DH)
    return (y_p, y_s,
            gla_state_out(st_p, bp), s5_out(re_p, bp), s5_out(im_p, bp), kv_out(mk4), kv_out(mv4),
            gla_state_out(st_s, bs), s5_out(re_s, bs), s5_out(im_s, bs))
```

```python
import functools
import math

import numpy as np
import jax
import jax.numpy as jnp
from jax import lax
from jax.experimental import pallas as pl
from jax.experimental.pallas import tpu as pltpu

F32 = jnp.float32
BF16 = jnp.bfloat16

D_MODEL = 1024
DEPTH = 1
EPS = 1e-6
D_FF = 2816
GLA_WIDTH = 512
GLA_HEADS = 4
GLA_DV = 128
GLA_DK = 64
GLA_QK_WIDTH = 256
GLA_GATE_RANK = 16
GLA_TAU = 16.0
S5_WIDTH = 512
S5_P = 16
S5_GROUPS = 32
S5_N = 64
S5_STATE = S5_GROUPS * S5_N
CROSS_HEADS = 4
CROSS_DH = 256
MEM_LEN = 256

LANES = 128
SUBLANES = 8
MXU_DIM = 256
VMEM_LIMIT_BYTES = 56 * 1024 * 1024

ROW_TILE = 512
S5_TILE_STEPS = 128
SHORT_GLA_CHUNKS = 4
SHORT_ATTN_SEQS = 8
FF_CHUNK = MXU_DIM
GLA_CHUNK_ROWS = 64
S5_LANE_CHUNK = 512
PROJ_V = 2 * GLA_QK_WIDTH
PROJ_R = PROJ_V + GLA_WIDTH
PROJ_U = PROJ_R + GLA_WIDTH
PROJ_END = PROJ_U + S5_WIDTH


def _cparams(n_grid_axes):
    return pltpu.CompilerParams(
        dimension_semantics=("arbitrary",) * n_grid_axes,
        vmem_limit_bytes=VMEM_LIMIT_BYTES,
    )


def _const_spec(shape):
    zeros = (0,) * len(shape)
    return pl.BlockSpec(shape, lambda *_: zeros, pipeline_mode=pl.Buffered(1))


def _dot(a, b):
    return jnp.dot(a, b, preferred_element_type=F32)


def _dot_nt(a, b):
    return lax.dot_general(a, b, (((1,), (1,)), ((), ())), preferred_element_type=F32)


def _dot_tn(a, b):
    return lax.dot_general(a, b, (((0,), (0,)), ((), ())), preferred_element_type=F32)


def _rms(x, g):
    return x * lax.rsqrt(jnp.mean(x * x, axis=-1, keepdims=True) + EPS) * g


def _swiglu(xn_bf, wgu_ref, wdn_ref, between=None):
    n_chunks = D_FF // FF_CHUNK

    def gate_up(c):
        lo, hi = c * FF_CHUNK, (c + 1) * FF_CHUNK
        return _dot(xn_bf, wgu_ref[:, lo:hi]), _dot(xn_bf, wgu_ref[:, D_FF + lo:D_FF + hi])

    acc = None
    nxt = gate_up(0)
    for c in range(n_chunks):
        g, u = nxt
        if c + 1 < n_chunks:
            nxt = gate_up(c + 1)
        a = (jax.nn.silu(g) * u).astype(BF16)
        d = _dot(a, wdn_ref[c * FF_CHUNK:(c + 1) * FF_CHUNK, :])
        acc = d if acc is None else acc + d
        if between is not None:
            between()
    return acc


def _log_sigmoid(z):
    return jnp.minimum(z, 0.0) - jnp.log1p(jnp.exp(-jnp.abs(z)))


def _pre_kernel(x_ref, g1_ref, wgu_ref, wdn_ref, gm_ref, win_ref, wzg_ref, wgate_ref, bgate_ref,
                h_ref, qk_ref, v_ref, la_ref, r_ref, us5_ref):
    x = x_ref[...]
    xn = _rms(x, g1_ref[...]).astype(BF16)
    h = x + 0.5 * _swiglu(xn, wgu_ref, wdn_ref)
    h_ref[...] = h
    un = _rms(h, gm_ref[...]).astype(BF16)
    p = _dot(un, win_ref[...])
    qk_ref[...] = p[:, 0:PROJ_V]
    v_ref[...] = p[:, PROJ_V:PROJ_R]
    r_ref[...] = p[:, PROJ_R:PROJ_U]
    us5_ref[...] = p[:, PROJ_U:PROJ_END]
    zg = _dot(un, wzg_ref[...])
    z = _dot(zg.astype(BF16), wgate_ref[...]) + bgate_ref[...]
    la_ref[...] = _log_sigmoid(z) * (1.0 / GLA_TAU)


def _pre_call(x2d, g1, wgu, wdn, gm, win, wzg, wgate, bgate, tm):
    n = x2d.shape[0]
    row = lambda w: pl.BlockSpec((tm, w), lambda i: (i, 0))
    widths = (D_MODEL, 2 * GLA_QK_WIDTH, GLA_WIDTH, GLA_QK_WIDTH, GLA_WIDTH, S5_WIDTH)
    out_shapes = [jax.ShapeDtypeStruct((n, w), F32) for w in widths]
    return pl.pallas_call(
        _pre_kernel,
        grid=(n // tm,),
        in_specs=[row(D_MODEL), _const_spec(g1.shape), _const_spec(wgu.shape), _const_spec(wdn.shape),
                  _const_spec(gm.shape), _const_spec(win.shape), _const_spec(wzg.shape),
                  _const_spec(wgate.shape), _const_spec(bgate.shape)],
        out_specs=[row(w) for w in widths],
        out_shape=out_shapes,
        compiler_params=_cparams(1),
        name="pre_ffn_proj",
    )(x2d, g1, wgu, wdn, gm, win, wzg, wgate, bgate)


def _head_split(x):
    return jnp.transpose(jnp.stack([x[:, hd * CROSS_DH:(hd + 1) * CROSS_DH] for hd in range(CROSS_HEADS)]),
                         (1, 0, 2))


def _mid_kernel(h_ref, og_ref, os_ref, wo_ref, gc_ref, wq_ref, h2_ref, q_ref):
    h2 = (h_ref[...] + _dot(og_ref[...].astype(BF16), wo_ref[0:GLA_WIDTH, :])
          + _dot(os_ref[...].astype(BF16), wo_ref[GLA_WIDTH:, :]))
    h2_ref[...] = h2
    hn = _rms(h2, gc_ref[...]).astype(BF16)
    q_ref[...] = _dot(hn, wq_ref[...])


def _mid_call(h, og, os_, wo, gc, wq, tm):
    n = h.shape[0]
    row = lambda w: pl.BlockSpec((tm, w), lambda i: (i, 0))
    return pl.pallas_call(
        _mid_kernel,
        grid=(n // tm,),
        in_specs=[row(D_MODEL), row(GLA_WIDTH), row(S5_WIDTH), _const_spec(wo.shape), _const_spec(gc.shape),
                  _const_spec(wq.shape)],
        out_specs=[row(D_MODEL), row(D_MODEL)],
        out_shape=[jax.ShapeDtypeStruct((n, D_MODEL), F32)] * 2,
        compiler_params=_cparams(1),
        name="mix_out_cross_q",
    )(h, og, os_, wo, gc, wq)


def _post_kernel(h2_ref, o_ref, wco_ref, g2_ref, wgu_ref, wdn_ref, gf_ref, y_ref):
    h3 = h2_ref[...] + _dot(o_ref[...].astype(BF16), wco_ref[...])
    hn = _rms(h3, g2_ref[...]).astype(BF16)
    h4 = h3 + 0.5 * _swiglu(hn, wgu_ref, wdn_ref)
    y_ref[...] = _rms(h4, gf_ref[...])


def _post_call(h2, o, wco, g2, wgu, wdn, gf, tm):
    n = h2.shape[0]
    row = pl.BlockSpec((tm, D_MODEL), lambda i: (i, 0))
    return pl.pallas_call(
        _post_kernel,
        grid=(n // tm,),
        in_specs=[row, row, _const_spec(wco.shape), _const_spec(g2.shape), _const_spec(wgu.shape),
                  _const_spec(wdn.shape), _const_spec(gf.shape)],
        out_specs=row,
        out_shape=jax.ShapeDtypeStruct((n, D_MODEL), F32),
        compiler_params=_cparams(1),
        name="post_ffn_final",
    )(h2, o, wco, g2, wgu, wdn, gf)


def _tail_kernel(h_ref, og_ref, os_ref, mem_ref, gmem_ref, wckv_ref, wo_ref, gc_ref, wq_ref, wco_ref, g2_ref,
                 wgu_ref, wdn_ref, gf_ref, y_ref, k4_ref, v4_ref, kv_scr):
    @pl.when(pl.program_id(1) == 0)
    def _():
        kv = _dot(_rms(mem_ref[0], gmem_ref[...]).astype(BF16), wckv_ref[...])
        kv_scr[...] = kv.astype(BF16)
        k4_ref[...] = _head_split(kv[:, :D_MODEL])
        v4_ref[...] = _head_split(kv[:, D_MODEL:])

    h2 = (h_ref[0] + _dot(og_ref[0].astype(BF16), wo_ref[0:GLA_WIDTH, :])
          + _dot(os_ref[0].astype(BF16), wo_ref[GLA_WIDTH:, :]))
    q = _dot(_rms(h2, gc_ref[...]).astype(BF16), wq_ref[...]).astype(BF16)
    heads = [slice(hd * CROSS_DH, (hd + 1) * CROSS_DH) for hd in range(CROSS_HEADS)]
    scores = [_dot_nt(q[:, sl], kv_scr[:, sl]) * (CROSS_DH ** -0.5) for sl in heads]
    probs = [_softmax(s).astype(BF16) for s in scores]
    outs = [_dot(p, kv_scr[:, D_MODEL + sl.start:D_MODEL + sl.stop]).astype(BF16) for p, sl in zip(probs, heads)]
    h3 = h2 + _dot(jnp.concatenate(outs, axis=-1), wco_ref[...])
    hn = _rms(h3, g2_ref[...]).astype(BF16)
    h4 = h3 + 0.5 * _swiglu(hn, wgu_ref, wdn_ref)
    y_ref[0] = _rms(h4, gf_ref[...])


def _tail_call(h3d, og3, os3, mem3, w, tm):
    s, t, _ = h3d.shape
    row = lambda wd: pl.BlockSpec((1, tm, wd), lambda i, j: (i, j, 0))
    memspec = pl.BlockSpec((1, MEM_LEN, D_MODEL), lambda i, j: (i, 0, 0))
    kv4spec = pl.BlockSpec((MEM_LEN, CROSS_HEADS, CROSS_DH), lambda i, j: (i, 0, 0))
    kv4shape = jax.ShapeDtypeStruct((s * MEM_LEN, CROSS_HEADS, CROSS_DH), F32)
    consts = [w['g_mem'], w['w_ckv'], w['w_out'], w['g_cross'], w['w_cq'], w['w_co'], w['g_ffn2'], w['w_ffn2_gu'],
              w['w_ffn2_down'], w['g_final']]
    return pl.pallas_call(
        _tail_kernel,
        grid=(s, t // tm),
        in_specs=[row(D_MODEL), row(GLA_WIDTH), row(S5_WIDTH), memspec] + [_const_spec(a.shape) for a in consts],
        out_specs=[row(D_MODEL), kv4spec, kv4spec],
        out_shape=[jax.ShapeDtypeStruct((s, t, D_MODEL), F32), kv4shape, kv4shape],
        scratch_shapes=[pltpu.VMEM((MEM_LEN, 2 * D_MODEL), BF16)],
        compiler_params=_cparams(2),
        name="tail_attn_ffn",
    )(h3d, og3, os3, mem3, *consts)


def _softmax(s):
    e = jnp.exp(s - jnp.max(s, axis=-1, keepdims=True))
    return e / jnp.sum(e, axis=-1, keepdims=True)


def _load_mem(ref, j):
    heads_major = jnp.transpose(ref[j], (1, 0, 2))
    return jnp.concatenate([heads_major[hd] for hd in range(CROSS_HEADS)], axis=-1).astype(BF16)


def _attn_packed_kernel(q_ref, k_ref, v_ref, o_ref, *, n_seq):
    t = q_ref.shape[1]
    shape = (CROSS_HEADS * t, D_MODEL)
    q_mask = (_iota2(shape, 0) // t) == (_iota2(shape, 1) // CROSS_DH)
    o_lane_head = _iota2((t, D_MODEL), 1) // CROSS_DH
    scores = []
    for j in range(n_seq):
        q_stack = jnp.where(q_mask, jnp.concatenate([q_ref[j]] * CROSS_HEADS, axis=0), 0.0).astype(BF16)
        scores.append(_dot_nt(q_stack, _load_mem(k_ref, j)) * (CROSS_DH ** -0.5))
    probs = [_softmax(s).astype(BF16) for s in scores]
    for j in range(n_seq):
        full = _dot(probs[j], _load_mem(v_ref, j))
        o = jnp.zeros((t, D_MODEL), F32)
        for hd in range(CROSS_HEADS):
            o = jnp.where(o_lane_head == hd, full[hd * t:(hd + 1) * t, :], o)
        o_ref[j] = o.astype(o_ref.dtype)


def _attn_call(q3, k3, v3, n_seq):
    s, t, _ = q3.shape
    assert t * CROSS_HEADS <= MXU_DIM and s % n_seq == 0
    qspec = pl.BlockSpec((n_seq, t, D_MODEL), lambda i: (i, 0, 0))
    kvspec = pl.BlockSpec((n_seq, MEM_LEN, CROSS_HEADS, CROSS_DH), lambda i: (i, 0, 0, 0))
    return pl.pallas_call(
        functools.partial(_attn_packed_kernel, n_seq=n_seq),
        grid=(s // n_seq,),
        in_specs=[qspec, kvspec, kvspec],
        out_specs=qspec,
        out_shape=jax.ShapeDtypeStruct(q3.shape, F32),
        compiler_params=_cparams(1),
        name="cross_attention",
    )(q3, k3, v3)


def _split3(x):
    hi = x.astype(BF16)
    r1 = x - hi.astype(F32)
    mid = r1.astype(BF16)
    lo = (r1 - mid.astype(F32)).astype(BF16)
    return hi, mid, lo


def _dot01(m01, x):
    hi, mid, lo = _split3(x)
    return _dot(m01, hi) + _dot(m01, mid) + _dot(m01, lo)


def _iota2(shape, dim):
    return lax.broadcasted_iota(jnp.int32, shape, dim)


def _gla_stages(qk_ref, v_ref, la_ref, r_ref, o_ref, g_ref, tri_ref, lvl_ref, khm_ref, vhm_ref, shm_ref,
                states, put_states, *, seq_len, n_chunks):
    c_rows = GLA_CHUNK_ROWS
    n_sub = c_rows // seq_len
    log2 = lambda n: int(math.log2(n))
    tri = tri_ref[...]
    lvl = lvl_ref[...]
    rowq = _iota2((c_rows, GLA_QK_WIDTH), 0)
    u_head = [(_iota2((GLA_DV, GLA_QK_WIDTH), 1) >> log2(GLA_DK)) == hd for hd in range(GLA_HEADS)]

    def head_blocks(x, mask_ref):
        return jnp.concatenate([x.astype(BF16)] * GLA_HEADS, axis=0) * mask_ref[...]

    g_out = g_ref[...]
    chunks = range(n_chunks)
    rows_of = lambda c: slice(c * c_rows, (c + 1) * c_rows)
    sub_of = lambda j: slice(j * seq_len, (j + 1) * seq_len)
    q = [qk_ref[rows_of(c), 0:GLA_QK_WIDTH] * (GLA_DK ** -0.5) for c in chunks]
    k = [qk_ref[rows_of(c), GLA_QK_WIDTH:2 * GLA_QK_WIDTH] for c in chunks]
    v = [v_ref[rows_of(c), :] for c in chunks]
    b = [_dot01(tri, la_ref[rows_of(c), :]) for c in chunks]
    yield

    a_mat = [jnp.where(lvl == 0, _dot_nt(q[c].astype(BF16), head_blocks(k[c], khm_ref)), 0.0) for c in chunks]
    last = list(b)
    m = 1
    while m < seq_len:
        yield
        even = ((rowq >> log2(m)) & 1) == 0
        for c in chunks:
            prev_last = pltpu.roll(last[c], m, axis=0)
            e = jnp.exp(jnp.minimum(jnp.where(even, last[c] - b[c], b[c] - prev_last), 0.0))
            a_lvl = _dot_nt((q[c] * e).astype(BF16), head_blocks(k[c] * e, khm_ref))
            a_mat[c] = jnp.where(lvl == log2(m) + 1, a_lvl, a_mat[c])
            last[c] = jnp.where(even, pltpu.roll(last[c], c_rows - m, axis=0), last[c])
        m *= 2
    yield
    o_intra = [_dot(a_mat[c].astype(BF16), head_blocks(v[c], vhm_ref)) for c in chunks]
    yield

    narrow = (lambda a: a.astype(BF16)) if seq_len % 16 == 0 else (lambda a: a)
    q_dec = [narrow(q[c] * jnp.exp(b[c])) for c in chunks]
    k_dec = [narrow(k[c] * jnp.exp(jnp.minimum(last[c] - b[c], 0.0))) for c in chunks]
    u_t = {}
    for c in chunks:
        vc = narrow(v[c])
        for j in range(n_sub):
            u_full = _dot_tn(vc[sub_of(j)], k_dec[c][sub_of(j)])
            acc = jnp.zeros((GLA_DV, GLA_QK_WIDTH), F32)
            for hd in range(GLA_HEADS):
                acc = jnp.where(u_head[hd], u_full[hd * GLA_DV:(hd + 1) * GLA_DV, :], acc)
            u_t[c, j] = acc

    states = list(states)
    state_of = (lambda c, j: c * n_sub + j) if n_sub > 1 else (lambda c, j: 0)
    for c in chunks:
        if c % 2 == 0:
            yield
        o_parts = []
        for j in range(n_sub):
            sj = state_of(c, j)
            o_parts.append(_dot_nt(q_dec[c][sub_of(j)], head_blocks(states[sj], shm_ref)))
            decay = jnp.exp(last[c][j * seq_len:j * seq_len + 1, :])
            states[sj] = states[sj] * decay + u_t[c, j]
        o_inter = o_parts[0] if n_sub == 1 else jnp.concatenate(o_parts, axis=0)
        o = o_intra[c] + o_inter
        r = r_ref[rows_of(c), :]
        outs = []
        for hd in range(GLA_HEADS):
            sl = slice(hd * GLA_DV, (hd + 1) * GLA_DV)
            outs.append(_rms(o[:, sl], g_out) * jax.nn.silu(r[:, sl]))
        o_ref[rows_of(c), :] = jnp.concatenate(outs, axis=-1).astype(o_ref.dtype)
    put_states(states)


def _gla_kernel(*refs, seq_len, n_chunks, n_state, has_s0):
    ins, rest = refs[:10], refs[10:]
    qk_ref, v_ref, la_ref, r_ref, g_ref = ins[:5]
    if has_s0:
        s0_ref, o_ref, st_ref, s_scr = rest
    else:
        o_ref, st_ref, s_scr = rest
        s0_ref = None
    t_idx = pl.program_id(1)

    @pl.when(t_idx == 0)
    def _():
        if has_s0:
            for j in range(n_state):
                s_scr[j] = s0_ref[j].reshape(GLA_QK_WIDTH, GLA_DV).T
        else:
            s_scr[...] = jnp.zeros_like(s_scr)

    def put_states(states):
        for j, s_t in enumerate(states):
            s_scr[j] = s_t

    for _ in _gla_stages(qk_ref.at[0], v_ref.at[0], la_ref.at[0], r_ref.at[0], o_ref.at[0], g_ref, *ins[5:],
                         [s_scr[j] for j in range(n_state)], put_states, seq_len=seq_len, n_chunks=n_chunks):
        pass

    @pl.when(t_idx == pl.num_programs(1) - 1)
    def _():
        for j in range(n_state):
            st_ref[j] = s_scr[j].T.reshape(GLA_HEADS, GLA_DK, GLA_DV)


def _gla_index_constants(seq_len):
    c = GLA_CHUNK_ROWS
    i = np.arange(c)[:, None]
    s = np.arange(c)[None, :]
    same_seq = (i // seq_len) == (s // seq_len)
    tri = (same_seq & (s <= i)).astype(np.float32)
    top_bit = np.floor(np.log2(np.maximum(i ^ s, 1))).astype(np.int32)
    lvl = np.where(same_seq & (s < i), top_bit + 1, -1)
    lvl = np.where(i == s, 0, lvl).astype(np.int32)
    head_mask = lambda rows_per_head, lanes_per_head: (
        (np.arange(GLA_HEADS * rows_per_head)[:, None] // rows_per_head)
        == (np.arange(GLA_HEADS * lanes_per_head)[None, :] // lanes_per_head)).astype(np.float32)
    return [jnp.asarray(tri, BF16), jnp.asarray(np.tile(lvl, (1, GLA_HEADS))),
            jnp.asarray(head_mask(c, GLA_DK), BF16), jnp.asarray(head_mask(c, GLA_DV), BF16),
            jnp.asarray(head_mask(GLA_DV, GLA_DK), BF16)]


def _gla_call(qk3, v3, la3, r3, g_out, s0, seq_len, tile_rows, n_state):
    n_outer, rows, _ = qk3.shape
    n_chunks = tile_rows // GLA_CHUNK_ROWS
    spec = lambda w: pl.BlockSpec((1, tile_rows, w), lambda i, t: (i, t, 0))
    st_spec = pl.BlockSpec((n_state, GLA_HEADS, GLA_DK, GLA_DV), lambda i, t: (i, 0, 0, 0))
    has_s0 = s0 is not None
    consts = [g_out] + _gla_index_constants(seq_len)
    in_specs = ([spec(2 * GLA_QK_WIDTH), spec(GLA_WIDTH), spec(GLA_QK_WIDTH), spec(GLA_WIDTH)]
                + [_const_spec(a.shape) for a in consts])
    args = [qk3, v3, la3, r3] + consts
    if has_s0:
        in_specs.append(st_spec)
        args.append(s0)
    return pl.pallas_call(
        functools.partial(_gla_kernel, seq_len=seq_len, n_chunks=n_chunks, n_state=n_state, has_s0=has_s0),
        grid=(n_outer, rows // tile_rows),
        in_specs=in_specs,
        out_specs=[spec(GLA_WIDTH), st_spec],
        out_shape=[jax.ShapeDtypeStruct((n_outer, rows, GLA_WIDTH), BF16),
                   jax.ShapeDtypeStruct((n_outer * n_state, GLA_HEADS, GLA_DK, GLA_DV), F32)],
        scratch_shapes=[pltpu.VMEM((n_state, GLA_DV, GLA_QK_WIDTH), F32)],
        compiler_params=_cparams(2),
        name="gla_mixer",
    )(*args)


def _pre_gla_kernel(x_ref, g1_ref, wgu_ref, wdn_ref, gm_ref, win_ref, wzg_ref, wgate_ref, bgate_ref, gout_ref,
                    tri_ref, lvl_ref, khm_ref, vhm_ref, shm_ref,
                    h_ref, us5_ref, og_ref, st_ref,
                    qk_scr, v_scr, la_scr, r_scr, s_scr, *, tiles_per_seq):
    i = pl.program_id(0)
    last_step = pl.num_programs(0) - 1
    n_chunks = x_ref.shape[0] // GLA_CHUNK_ROWS

    @pl.when(i == 0)
    def _():
        for scr in (qk_scr, v_scr, la_scr, r_scr, s_scr):
            scr[...] = jnp.zeros_like(scr)

    def gla_stages():
        first_of_seq = lax.rem(i + (tiles_per_seq - 1), tiles_per_seq) == 0
        state = jnp.where(first_of_seq, 0.0, s_scr[0])

        def put_states(states):
            s_scr[0] = states[0]
            st_ref[0] = states[0].T.reshape(GLA_HEADS, GLA_DK, GLA_DV)

        return _gla_stages(qk_scr, v_scr, la_scr, r_scr, og_ref, gout_ref, tri_ref, lvl_ref, khm_ref, vhm_ref,
                           shm_ref, [state], put_states, seq_len=GLA_CHUNK_ROWS, n_chunks=n_chunks)

    @pl.when(i < last_step)
    def _():
        stages = gla_stages()
        next(stages)
        x = x_ref[...]
        xn = _rms(x, g1_ref[...]).astype(BF16)
        h = x + 0.5 * _swiglu(xn, wgu_ref, wdn_ref, between=lambda: next(stages, None))
        h_ref[...] = h
        un = _rms(h, gm_ref[...]).astype(BF16)
        p = []
        for lo, hi in ((0, PROJ_V), (PROJ_V, PROJ_R), (PROJ_R, PROJ_U), (PROJ_U, PROJ_END)):
            p.append(_dot(un, win_ref[:, lo:hi]))
            next(stages, None)
        zg = _dot(un, wzg_ref[...])
        z = _dot(zg.astype(BF16), wgate_ref[...]) + bgate_ref[...]
        for _ in stages:
            pass
        qk_scr[...] = p[0]
        v_scr[...] = p[1]
        r_scr[...] = p[2]
        us5_ref[...] = p[3]
        la_scr[...] = _log_sigmoid(z) * (1.0 / GLA_TAU)

    @pl.when(i == last_step)
    def _():
        for _ in gla_stages():
            pass


def _pre_gla_call(x2d, w, tm, seq_rows):
    n = x2d.shape[0]
    n_tiles = n // tm
    tiles_per_seq = seq_rows // tm
    cur = lambda wd: pl.BlockSpec((tm, wd), lambda i: (jnp.minimum(i, n_tiles - 1), 0))
    prev = lambda wd: pl.BlockSpec((tm, wd), lambda i: (jnp.maximum(i - 1, 0), 0))
    st_spec = pl.BlockSpec((1, GLA_HEADS, GLA_DK, GLA_DV),
                           lambda i: (jnp.maximum(i - 1, 0) // tiles_per_seq, 0, 0, 0))
    consts = [w['g_ffn1'], w['w_ffn1_gu'], w['w_ffn1_down'], w['g_mix'], w['w_in_main'], w['w_zg'], w['w_gate'],
              w['b_gate'], w['g_gla_out']] + _gla_index_constants(GLA_CHUNK_ROWS)
    return pl.pallas_call(
        functools.partial(_pre_gla_kernel, tiles_per_seq=tiles_per_seq),
        grid=(n_tiles + 1,),
        in_specs=[cur(D_MODEL)] + [_const_spec(a.shape) for a in consts],
        out_specs=[cur(D_MODEL), cur(S5_WIDTH), prev(GLA_WIDTH), st_spec],
        out_shape=[jax.ShapeDtypeStruct((n, D_MODEL), F32), jax.ShapeDtypeStruct((n, S5_WIDTH), F32),
                   jax.ShapeDtypeStruct((n, GLA_WIDTH), BF16),
                   jax.ShapeDtypeStruct((n // seq_rows, GLA_HEADS, GLA_DK, GLA_DV), F32)],
        scratch_shapes=[pltpu.VMEM((tm, 2 * GLA_QK_WIDTH), F32), pltpu.VMEM((tm, GLA_WIDTH), F32),
                        pltpu.VMEM((tm, GLA_QK_WIDTH), F32), pltpu.VMEM((tm, GLA_WIDTH), F32),
                        pltpu.VMEM((1, GLA_DV, GLA_QK_WIDTH), F32)],
        compiler_params=_cparams(1),
        name="pre_ffn_gla",
    )(x2d, *consts)


def _s5_disc_kernel(lre_ref, lim_ref, ldt_ref, bre_ref, bim_ref, are_ref, aim_ref, bbre_ref, bbim_ref):
    lr = lre_ref[...]
    li = lim_ref[...]
    dt = jnp.exp(ldt_ref[...])
    mag = jnp.exp(dt * lr)
    ab_re = mag * jnp.cos(dt * li)
    ab_im = mag * jnp.sin(dt * li)
    den = lr * lr + li * li
    coef_re = ((ab_re - 1.0) * lr + ab_im * li) / den
    coef_im = (ab_im * lr - (ab_re - 1.0) * li) / den
    are_ref[...] = ab_re
    aim_ref[...] = ab_im
    br = bre_ref[...]
    bi = bim_ref[...]
    cr = coef_re[:, None, :]
    cim = coef_im[:, None, :]
    bbre_ref[...] = cr * br - cim * bi
    bbim_ref[...] = cr * bi + cim * br


def _s5_disc_call(lre, lim, ldt, bre_t, bim_t):
    gn = jax.ShapeDtypeStruct((S5_GROUPS, S5_N), F32)
    gpn = jax.ShapeDtypeStruct((S5_GROUPS, S5_P, S5_N), F32)
    return pl.pallas_call(
        _s5_disc_kernel,
        out_shape=[gn, gn, gpn, gpn],
        name="s5_discretise",
    )(lre, lim, ldt, bre_t, bim_t)


def _to_time_major(u_ref, perm_scr, n_seq, t_tile):
    if n_seq == SUBLANES:
        n_slab = S5_WIDTH // LANES
        for j in range(n_seq):
            for sb in range(n_slab):
                perm_scr[sb, pl.ds(j, t_tile, stride=n_seq), :] = u_ref[j, :, sb * LANES:(sb + 1) * LANES]
        return jnp.concatenate([perm_scr[sb] for sb in range(n_slab)], axis=-1)
    return jnp.concatenate([u_ref[:, t, :] for t in range(t_tile)], axis=0)


def _from_time_major(y, o_ref, perm_scr, n_seq, t_tile):
    if n_seq == SUBLANES:
        n_slab = S5_WIDTH // LANES
        for sb in range(n_slab):
            perm_scr[sb] = y[:, sb * LANES:(sb + 1) * LANES]
        for j in range(n_seq):
            for sb in range(n_slab):
                o_ref[j, :, sb * LANES:(sb + 1) * LANES] = perm_scr[sb, pl.ds(j, t_tile, stride=n_seq), :]
    else:
        for t in range(t_tile):
            o_ref[:, t, :] = y[t * n_seq:(t + 1) * n_seq, :]


def _s5_kernel(u_ref, x0re_ref, x0im_ref, are_ref, aim_ref, bre_ref, bim_ref, c_ref, d_ref, wglu_ref,
               bglu_ref, o_ref, xre_ref, xim_ref, sre_scr, sim_scr, xre_scr, xim_scr, perm_scr, *, n_seq, t_tile):
    half_w = S5_WIDTH // 2
    half_s = S5_STATE // 2
    state_major = x0re_ref.shape == (S5_STATE, n_seq)

    @pl.when(pl.program_id(0) == 0)
    def _():
        xre_scr[...] = x0re_ref[...].T if state_major else x0re_ref[...]
        xim_scr[...] = x0im_ref[...].T if state_major else x0im_ref[...]

    rows = t_tile * n_seq
    u = _to_time_major(u_ref, perm_scr, n_seq, t_tile)
    ub = u.astype(BF16)
    for hf in range(2):
        uh = ub[:, hf * half_w:(hf + 1) * half_w]
        st = slice(hf * half_s, (hf + 1) * half_s)
        sre_scr[:, :, st] = _dot(uh, bre_ref[hf]).reshape(t_tile, n_seq, half_s)
        sim_scr[:, :, st] = _dot(uh, bim_ref[hf]).reshape(t_tile, n_seq, half_s)

    for lc in range(S5_STATE // S5_LANE_CHUNK):
        lanes = slice(lc * S5_LANE_CHUNK, (lc + 1) * S5_LANE_CHUNK)
        ar = jnp.broadcast_to(are_ref[:, lanes], (SUBLANES, S5_LANE_CHUNK))
        ai = jnp.broadcast_to(aim_ref[:, lanes], (SUBLANES, S5_LANE_CHUNK))
        for sc in range(n_seq // SUBLANES):
            srows = slice(sc * SUBLANES, (sc + 1) * SUBLANES)
            xr = xre_scr[srows, lanes]
            xi = xim_scr[srows, lanes]
            for t in range(t_tile):
                nxr = ar * xr - ai * xi + sre_scr[t, srows, lanes]
                nxi = ar * xi + ai * xr + sim_scr[t, srows, lanes]
                sre_scr[t, srows, lanes] = nxr
                sim_scr[t, srows, lanes] = nxi
                xr, xi = nxr, nxi
            xre_scr[srows, lanes] = xr
            xim_scr[srows, lanes] = xi

    xr_all = sre_scr[...].reshape(rows, S5_STATE).astype(BF16)
    xi_all = sim_scr[...].reshape(rows, S5_STATE).astype(BF16)
    ys = []
    for hf in range(2):
        st = slice(hf * half_s, (hf + 1) * half_s)
        ys.append(_dot(xr_all[:, st], c_ref[hf, 0:half_s, :]) + _dot(xi_all[:, st], c_ref[hf, half_s:, :]))
    y = jnp.concatenate(ys, axis=-1) + d_ref[...] * u
    z = jax.nn.gelu(y)
    gate = jax.nn.sigmoid(_dot(z.astype(BF16), wglu_ref[...]) + bglu_ref[...])
    _from_time_major(z * gate, o_ref, perm_scr, n_seq, t_tile)

    @pl.when(pl.program_id(0) == pl.num_programs(0) - 1)
    def _():
        xre_ref[...] = xre_scr[...].T if state_major else xre_scr[...]
        xim_ref[...] = xim_scr[...].T if state_major else xim_scr[...]


def _s5_call(u3, x0re, x0im, are, aim, bre, bim, cmat, dskip, wglu, bglu, t_tile):
    n_seq, t, _ = u3.shape
    uspec = pl.BlockSpec((n_seq, t_tile, S5_WIDTH), lambda i: (0, i, 0))
    perm_rows = t_tile * n_seq if n_seq == SUBLANES else SUBLANES
    assert x0re.shape in ((n_seq, S5_STATE), (S5_STATE, n_seq)) and x0im.shape == x0re.shape
    xspec = pl.BlockSpec(x0re.shape, lambda i: (0, 0))
    consts = [are, aim, bre, bim, cmat, dskip, wglu, bglu]
    return pl.pallas_call(
        functools.partial(_s5_kernel, n_seq=n_seq, t_tile=t_tile),
        grid=(t // t_tile,),
        in_specs=[uspec, xspec, xspec] + [_const_spec(a.shape) for a in consts],
        out_specs=[uspec, xspec, xspec],
        out_shape=[jax.ShapeDtypeStruct((n_seq, t, S5_WIDTH), F32),
                   jax.ShapeDtypeStruct(x0re.shape, F32),
                   jax.ShapeDtypeStruct(x0re.shape, F32)],
        scratch_shapes=[pltpu.VMEM((t_tile, n_seq, S5_STATE), F32), pltpu.VMEM((t_tile, n_seq, S5_STATE), F32),
                        pltpu.VMEM((n_seq, S5_STATE), F32), pltpu.VMEM((n_seq, S5_STATE), F32),
                        pltpu.VMEM((S5_WIDTH // LANES, perm_rows, LANES), F32)],
        compiler_params=_cparams(1),
        name="s5_mixer",
    )(u3, x0re, x0im, *consts)


def _block_diag(blocks):
    g, a, b = blocks.shape
    spread = np.tile(np.eye(b, dtype=np.float32), (1, g))
    keep = (np.arange(g * a)[:, None] // a) == (np.arange(g * b)[None, :] // b)
    return jnp.where(keep, jnp.dot(blocks.reshape(g * a, b), spread), 0.0)


def _halves_block_diag(blocks):
    g = blocks.shape[0]
    return jnp.stack([_block_diag(blocks[:g // 2]), _block_diag(blocks[g // 2:])])


def _trunk_group(x3, s_gla0, x0re, x0im, mem_k3, mem_v3, w, mem_tokens=None):
    s, t, _ = x3.shape
    n = s * t
    tm = ROW_TILE
    long_seq = t >= tm
    if long_seq:
        assert s_gla0 is None and t % tm == 0
        h, us5, o_gla, st_new = _pre_gla_call(x3.reshape(n, D_MODEL), w, tm, t)
    else:
        assert GLA_CHUNK_ROWS % t == 0 and n % (SHORT_GLA_CHUNKS * GLA_CHUNK_ROWS) == 0
        h, qk, v, la, r, us5 = _pre_call(x3.reshape(n, D_MODEL), w['g_ffn1'], w['w_ffn1_gu'], w['w_ffn1_down'],
                                         w['g_mix'], w['w_in_main'], w['w_zg'], w['w_gate'], w['b_gate'], tm)
        rows = SHORT_GLA_CHUNKS * GLA_CHUNK_ROWS
        shp = lambda a: a.reshape(n // rows, rows, a.shape[-1])
        o_gla, st_new = _gla_call(shp(qk), shp(v), shp(la), shp(r), w['g_gla_out'], s_gla0, t, rows, rows // t)
    o_s5, xre, xim = _s5_call(us5.reshape(s, t, S5_WIDTH), x0re, x0im, w['s5_are'], w['s5_aim'], w['s5_bre'],
                              w['s5_bim'], w['s5_c'], w['s5_d'], w['s5_wglu'], w['s5_bglu'], min(t, S5_TILE_STEPS))
    if long_seq:
        y, k4, v4 = _tail_call(h.reshape(s, t, D_MODEL), o_gla.reshape(s, t, GLA_WIDTH), o_s5, mem_tokens, w, tm)
        return y, st_new, xre, xim, k4, v4
    o_s5 = o_s5.reshape(n, S5_WIDTH)
    h2, q = _mid_call(h, o_gla.reshape(n, GLA_WIDTH), o_s5, w['w_out'], w['g_cross'], w['w_cq'], tm)
    o_att = _attn_call(q.reshape(s, t, D_MODEL), mem_k3, mem_v3, SHORT_ATTN_SEQS)
    y = _post_call(h2, o_att.reshape(n, D_MODEL), w['w_co'], w['g_ffn2'], w['w_ffn2_gu'], w['w_ffn2_down'],
                   w['g_final'], tm)
    return y.reshape(s, t, D_MODEL), st_new, xre, xim, None, None


def kernel(x_prompt, x_sample, state_gla, state_s5_re, state_s5_im, cache_mem_k, cache_mem_v, mem_prompt, g_ffn1, w_ffn1_gu, w_ffn1_down, g_mix, w_in, w_gla_gate, b_gla_gate, g_gla_out, s5_lambda_re, s5_lambda_im, s5_log_dt, s5_b_re, s5_b_im, s5_c_re, s5_c_im, s5_d, s5_w_glu, s5_b_glu, w_out, g_cross, g_mem, w_cq, w_ckv, w_co, g_ffn2, w_ffn2_gu, w_ffn2_down, g_final):
    bp, tp, _ = x_prompt.shape
    bs, ts, _ = x_sample.shape
    l = 0
    row = lambda a: a.reshape(1, -1).astype(F32)
    bf = lambda a: a.astype(BF16)

    are, aim, bbre, bbim = _s5_disc_call(
        s5_lambda_re[l], s5_lambda_im[l], s5_log_dt[l].reshape(S5_GROUPS, 1),
        jnp.transpose(s5_b_re[l], (0, 2, 1)), jnp.transpose(s5_b_im[l], (0, 2, 1)))
    c_re_t = jnp.transpose(s5_c_re[l], (0, 2, 1))
    c_im_t = jnp.transpose(s5_c_im[l], (0, 2, 1))
    cre_h = _halves_block_diag(c_re_t)
    cim_h = _halves_block_diag(-c_im_t)
    w_in_l = w_in[l]
    zg_lo = 2 * GLA_QK_WIDTH + GLA_WIDTH
    zg_hi = zg_lo + GLA_GATE_RANK
    w = {
        'g_ffn1': row(g_ffn1[l]), 'w_ffn1_gu': bf(w_ffn1_gu[l]), 'w_ffn1_down': bf(w_ffn1_down[l]),
        'g_mix': row(g_mix[l]),
        'w_in_main': bf(jnp.concatenate([w_in_l[:, :zg_lo], w_in_l[:, zg_hi:]], axis=1)),
        'w_zg': bf(jnp.pad(w_in_l[:, zg_lo:zg_hi], ((0, 0), (0, LANES - GLA_GATE_RANK)))),
        'w_gate': bf(jnp.pad(w_gla_gate[l], ((0, LANES - GLA_GATE_RANK), (0, 0)))),
        'b_gate': row(b_gla_gate[l]),
        'g_gla_out': row(g_gla_out[l]),
        's5_are': are.reshape(1, S5_STATE), 's5_aim': aim.reshape(1, S5_STATE),
        's5_bre': bf(_halves_block_diag(bbre)), 's5_bim': bf(_halves_block_diag(bbim)),
        's5_c': bf(jnp.concatenate([cre_h, cim_h], axis=1)),
        's5_d': row(s5_d[l]), 's5_wglu': bf(_block_diag(s5_w_glu[l])), 's5_bglu': row(s5_b_glu[l]),
        'w_out': bf(w_out[l]), 'g_cross': row(g_cross[l]), 'w_cq': bf(w_cq[l]), 'w_co': bf(w_co[l]),
        'g_ffn2': row(g_ffn2[l]), 'w_ffn2_gu': bf(w_ffn2_gu[l]), 'w_ffn2_down': bf(w_ffn2_down[l]),
        'g_final': row(g_final), 'g_mem': row(g_mem[l]), 'w_ckv': bf(w_ckv[l]),
    }

    zeros_state = jnp.zeros((bp, S5_STATE), F32)
    y_p, st_p, re_p, im_p, mk4, mv4 = _trunk_group(x_prompt, None, zeros_state, zeros_state, None, None, w,
                                                  mem_tokens=mem_prompt)

    if bs % LANES == 0:
        state_major = lambda a: jnp.transpose(a[l], (1, 2, 0)).reshape(S5_STATE, bs)
    else:
        state_major = lambda a: a[l].reshape(bs, S5_STATE)
    s0 = state_gla.reshape(DEPTH * bs, GLA_HEADS, GLA_DK, GLA_DV)[l * bs:(l + 1) * bs]
    y_s, st_s, re_s, im_s, _, _ = _trunk_group(
        x_sample, s0, state_major(state_s5_re), state_major(state_s5_im),
        cache_mem_k.reshape(DEPTH * bs, MEM_LEN, CROSS_HEADS, CROSS_DH)[l * bs:(l + 1) * bs],
        cache_mem_v.reshape(DEPTH * bs, MEM_LEN, CROSS_HEADS, CROSS_DH)[l * bs:(l + 1) * bs],
        w)

    def gla_state_out(st, nb):
        return st[None]

    def s5_out(a, nb):
        if a.shape[0] != nb:
            a = jnp.transpose(a.reshape(S5_GROUPS, S5_N, nb), (2, 0, 1))
        return a.reshape(1, nb, S5_GROUPS, S5_N)
    kv_out = lambda a: a.reshape(1, bp, MEM_LEN, CROSS_HEADS, CROSS_DH)
    return (y_p, y_s,
            gla_state_out(st_p, bp), s5_out(re_p, bp), s5_out(im_p, bp), kv_out(mk4), kv_out(mv4),
            gla_state_out(st_s, bs), s5_out(re_s, bs), s5_out(im_s, bs))
```
